```python
import math
import numpy as np
import jax
import jax.numpy as jnp
from jax import lax

D_MODEL = 4096
BATCH = 8
SEQ = 2048
DEPTH = 2

HEAD_DIM = 128
N_HEADS_DIFF = 8
N_HEADS_FOX = 7
N_HEADS_NSA = 8
N_KV_NSA = 2
N_HEADS_DIL = 9
DIFF_HALF = HEAD_DIM // 2
D_FF = 11008
PLE_DIM = 256
ROPE_THETA = 10000.0
EPS = 1e-6
Q_BLOCK = 128
CMP_LEN = 32
CMP_STRIDE = 16
SEL_LEN = 64
SEL_TOPN = 16
WIN_LEN = 512
SEL_Q_BLOCK = 64
DIL_PATTERNS = ((128, 1), (512, 4), (2048, 16))
DIL_Q_BLOCK = 64
NEG_INF = -1e30
POS_BIG = 1e30

kernel_name = "hymba_style_diff_fox_nsa_dilated_macaron"


def _in_proj_sizes():
    hd = HEAD_DIM
    kv = N_KV_NSA * hd
    return (
        N_HEADS_DIFF * hd, N_HEADS_DIFF * hd, N_HEADS_DIFF * hd,
        N_HEADS_FOX * hd, N_HEADS_FOX * hd, N_HEADS_FOX * hd, N_HEADS_FOX,
        N_HEADS_NSA * hd, kv, kv, kv, kv, kv, kv, 3 * N_HEADS_NSA,
        N_HEADS_DIL * hd, N_HEADS_DIL * hd, N_HEADS_DIL * hd,
    )


def _out_width():
    return (N_HEADS_DIFF + N_HEADS_FOX + N_HEADS_NSA + N_HEADS_DIL // len(DIL_PATTERNS)) * HEAD_DIM


def rms_norm(x, g):
    xf = x.astype(jnp.float32)
    y = xf * lax.rsqrt(jnp.mean(xf * xf, axis=-1, keepdims=True) + EPS)
    return (y * g.astype(jnp.float32)).astype(x.dtype)


def rope(x, pos):
    d = x.shape[-1]
    half = d // 2
    inv = ROPE_THETA ** (-jnp.arange(half, dtype=jnp.float32) * 2.0 / d)
    ang = pos.astype(jnp.float32)[:, None] * inv[None, :]
    cos = jnp.cos(ang)[None, :, None, :]
    sin = jnp.sin(ang)[None, :, None, :]
    xf = x.astype(jnp.float32)
    x1, x2 = xf[..., :half], xf[..., half:]
    return jnp.concatenate([x1 * cos - x2 * sin, x2 * cos + x1 * sin], axis=-1).astype(x.dtype)


def swiglu(x, w_gate, w_up, w_down):
    return (jax.nn.silu(x @ w_gate) * (x @ w_up)) @ w_down


def masked_softmax(s, valid):
    s = jnp.where(valid, s, NEG_INF)
    m = jnp.max(s, axis=-1, keepdims=True)
    e = jnp.where(valid, jnp.exp(s - m), 0.0)
    den = jnp.sum(e, axis=-1, keepdims=True)
    return e / jnp.where(den > 0, den, 1.0), m, den


def sweep_blocks(fn, n_blocks):
    out = lax.map(fn, jnp.arange(n_blocks))
    out = jnp.moveaxis(out, 0, 1)
    return out.reshape((out.shape[0], out.shape[1] * out.shape[2]) + out.shape[3:])


def diff_attention(q, k, v, lam, lam_init, subln_g):
    B, T, H, _, dq = q.shape
    scale = dq ** -0.5
    vf = v.astype(jnp.float32)
    kpos = jnp.arange(T)

    def block(i):
        t0 = i * Q_BLOCK
        qb = lax.dynamic_slice_in_dim(q, t0, Q_BLOCK, axis=1)
        s = jnp.einsum('bqhcd,bkhcd->bhcqk', qb, k, preferred_element_type=jnp.float32) * scale
        valid = kpos[None, :] <= (t0 + jnp.arange(Q_BLOCK))[:, None]
        pr, _, _ = masked_softmax(s, valid)
        a = pr[:, :, 0] - lam * pr[:, :, 1]
        o = jnp.einsum('bhqk,bkhd->bqhd', a, vf)
        return rms_norm(o, subln_g) * (1.0 - lam_init)

    return sweep_blocks(block, T // Q_BLOCK).astype(v.dtype)


def forgetting_attention(q, k, v, f_logit):
    B, T, H, dh = q.shape
    scale = dh ** -0.5
    c = jnp.cumsum(jax.nn.log_sigmoid(f_logit.astype(jnp.float32)), axis=1)
    c_k = jnp.transpose(c, (0, 2, 1))[:, :, None, :]
    vf = v.astype(jnp.float32)
    kpos = jnp.arange(T)

    def block(i):
        t0 = i * Q_BLOCK
        qb = lax.dynamic_slice_in_dim(q, t0, Q_BLOCK, axis=1)
        c_q = jnp.transpose(lax.dynamic_slice_in_dim(c, t0, Q_BLOCK, axis=1), (0, 2, 1))[..., None]
        s = jnp.einsum('bqhd,bkhd->bhqk', qb, k, preferred_element_type=jnp.float32) * scale + (c_q - c_k)
        valid = kpos[None, :] <= (t0 + jnp.arange(Q_BLOCK))[:, None]
        pr, _, _ = masked_softmax(s, valid)
        return jnp.einsum('bhqk,bkhd->bqhd', pr, vf)

    return sweep_blocks(block, T // Q_BLOCK).astype(v.dtype)


def nsa_attention(q, k_cmp_raw, v_cmp_raw, k_sel, v_sel, k_win, v_win, gates,
                  pe_k, pe_v, wk1, wk2, wv1, wv2):
    B, T, H, dh = q.shape
    G = k_sel.shape[2]
    R = H // G
    scale = dh ** -0.5
    qg = q.reshape(B, T, G, R, dh)
    tpos = jnp.arange(T)

    n_cmp = (T - CMP_LEN) // CMP_STRIDE + 1
    cmp_idx = np.arange(n_cmp)[:, None] * CMP_STRIDE + np.arange(CMP_LEN)[None, :]

    def compress(xk, pe, w1, w2):
        blocks = xk[:, cmp_idx] + pe[None, None, :, None, :]
        flat = jnp.moveaxis(blocks, 3, 2).reshape(B, n_cmp, G, CMP_LEN * dh)
        return jax.nn.gelu(flat @ w1) @ w2

    kc = compress(k_cmp_raw, pe_k, wk1, wk2)
    vc = compress(v_cmp_raw, pe_v, wv1, wv2)
    cmp_valid = jnp.asarray(cmp_idx[:, -1])[None, :] <= tpos[:, None]
    s_cmp = jnp.einsum('btgrd,bcgd->bgrtc', qg, kc, preferred_element_type=jnp.float32) * scale
    p_cmp, _, _ = masked_softmax(s_cmp, cmp_valid)
    o_cmp = jnp.einsum('bgrtc,bcgd->btgrd', p_cmp, vc.astype(jnp.float32))

    n_sel = T // SEL_LEN
    sel_start = np.arange(n_sel) * SEL_LEN
    cmp_start = np.arange(n_cmp) * CMP_STRIDE
    overlap = ((cmp_start[None, :] < sel_start[:, None] + SEL_LEN) &
               (cmp_start[None, :] + CMP_LEN > sel_start[:, None])).astype(np.float32)
    imp = jnp.einsum('bgrtc,sc->bgts', p_cmp, jnp.asarray(overlap))
    blk = jnp.arange(n_sel)[None, :]
    cur = (tpos // SEL_LEN)[:, None]
    forced = (blk == 0) | (blk == cur) | (blk == cur - 1)
    causal_blk = jnp.asarray(sel_start)[None, :] <= tpos[:, None]
    imp = jnp.where(forced, POS_BIG, jnp.where(causal_blk, imp, NEG_INF))
    n_top = min(SEL_TOPN, n_sel)
    _, sel_idx = lax.top_k(imp, n_top)

    ks_blk = jnp.transpose(k_sel.reshape(B, n_sel, SEL_LEN, G, dh), (0, 3, 1, 2, 4))
    vs_blk = jnp.transpose(v_sel.reshape(B, n_sel, SEL_LEN, G, dh), (0, 3, 1, 2, 4))
    b_ix = jnp.arange(B)[:, None, None, None]
    g_ix = jnp.arange(G)[None, :, None, None]

    def sel_block(i):
        t0 = i * SEL_Q_BLOCK
        qb = lax.dynamic_slice_in_dim(qg, t0, SEL_Q_BLOCK, axis=1)
        ib = lax.dynamic_slice_in_dim(sel_idx, t0, SEL_Q_BLOCK, axis=2)
        kb = ks_blk[b_ix, g_ix, ib].reshape(B, G, SEL_Q_BLOCK, n_top * SEL_LEN, dh)
        vb = vs_blk[b_ix, g_ix, ib].reshape(B, G, SEL_Q_BLOCK, n_top * SEL_LEN, dh)
        kpos = (ib[..., None] * SEL_LEN + jnp.arange(SEL_LEN)).reshape(B, G, SEL_Q_BLOCK, n_top * SEL_LEN)
        valid = kpos <= (t0 + jnp.arange(SEL_Q_BLOCK))[None, None, :, None]
        s = jnp.einsum('bqgrd,bgqkd->bgrqk', qb, kb, preferred_element_type=jnp.float32) * scale
        pr, _, _ = masked_softmax(s, valid[:, :, None])
        return jnp.einsum('bgrqk,bgqkd->bqgrd', pr, vb.astype(jnp.float32))

    o_sel = sweep_blocks(sel_block, T // SEL_Q_BLOCK)

    kw_pad = jnp.pad(k_win, ((0, 0), (WIN_LEN, 0), (0, 0), (0, 0)))
    vw_pad = jnp.pad(v_win, ((0, 0), (WIN_LEN, 0), (0, 0), (0, 0)))

    def win_block(i):
        t0 = i * Q_BLOCK
        qb = lax.dynamic_slice_in_dim(qg, t0, Q_BLOCK, axis=1)
        kb = lax.dynamic_slice_in_dim(kw_pad, t0, WIN_LEN + Q_BLOCK, axis=1)
        vb = lax.dynamic_slice_in_dim(vw_pad, t0, WIN_LEN + Q_BLOCK, axis=1)
        kpos = t0 - WIN_LEN + jnp.arange(WIN_LEN + Q_BLOCK)
        tq = (t0 + jnp.arange(Q_BLOCK))[:, None]
        valid = (kpos[None, :] <= tq) & (kpos[None, :] > tq - WIN_LEN) & (kpos[None, :] >= 0)
        s = jnp.einsum('bqgrd,bkgd->bgrqk', qb, kb, preferred_element_type=jnp.float32) * scale
        pr, _, _ = masked_softmax(s, valid)
        return jnp.einsum('bgrqk,bkgd->bqgrd', pr, vb.astype(jnp.float32))

    o_win = sweep_blocks(win_block, T // Q_BLOCK)

    g = gates.astype(jnp.float32).reshape(B, T, G, R, 3)
    o = g[..., 0:1] * o_cmp + g[..., 1:2] * o_sel + g[..., 2:3] * o_win
    return o.reshape(B, T, H, dh).astype(q.dtype)


def dilated_attention(q, k, v):
    B, T, H, dh = q.shape
    n_g = len(DIL_PATTERNS)
    hg = H // n_g
    scale = dh ** -0.5
    kg = k.reshape(B, T, n_g, hg, dh)
    vg = v.astype(jnp.float32).reshape(B, T, n_g, hg, dh)

    def block(i):
        t0 = i * DIL_Q_BLOCK
        qb = lax.dynamic_slice_in_dim(q, t0, DIL_Q_BLOCK, axis=1).reshape(B, DIL_Q_BLOCK, n_g, hg, dh)
        tq = t0 + jnp.arange(DIL_Q_BLOCK)
        outs, lses = [], []
        for g, (w, r) in enumerate(DIL_PATTERNS):
            nk = w // r + 1
            kpos = tq[:, None] - jnp.arange(nk)[None, :] * r
            valid = kpos >= 0
            kidx = jnp.maximum(kpos, 0)
            kb = kg[:, :, g][:, kidx]
            vb = vg[:, :, g][:, kidx]
            s = jnp.einsum('bqhd,bqkhd->bhqk', qb[:, :, g], kb, preferred_element_type=jnp.float32) * scale
            pr, m, den = masked_softmax(s, valid)
            outs.append(jnp.einsum('bhqk,bqkhd->bqhd', pr, vb))
            lses.append(jnp.transpose((m + jnp.log(den))[..., 0], (0, 2, 1)))
        alpha = jax.nn.softmax(jnp.stack(lses, axis=0), axis=0)[..., None]
        return jnp.sum(alpha * jnp.stack(outs, axis=0), axis=0)

    return sweep_blocks(block, T // DIL_Q_BLOCK).astype(q.dtype)


def setup_inputs(seed: int = 0) -> dict:
    key = jax.random.key(seed)
    keys = iter(jax.random.split(key, 48))

    def nrm(shape, scale):
        return scale * jax.random.normal(next(keys), shape, dtype=jnp.float32)

    def gain(shape):
        return 1.0 + 0.05 * jax.random.normal(next(keys), shape, dtype=jnp.float32)

    d, f, hd, L = D_MODEL, D_FF, HEAD_DIM, DEPTH
    n_in = sum(_in_proj_sizes())
    n_out = _out_width()
    return {
        "x": nrm((BATCH, SEQ, d), 1.0),
        "p": nrm((DEPTH, BATCH, SEQ, PLE_DIM), 1.0),
        "ffn1_pre_g": gain((L, d)),
        "ffn1_w_gate": nrm((L, d, f), d ** -0.5),
        "ffn1_w_up": nrm((L, d, f), d ** -0.5),
        "ffn1_w_down": nrm((L, f, d), f ** -0.5),
        "ffn1_post_g": gain((L, d)),
        "mix_pre_g": gain((L, d)),
        "w_in": nrm((L, d, n_in), d ** -0.5),
        "fox_bf": jnp.linspace(1.0, 6.0, N_HEADS_FOX, dtype=jnp.float32)[None, :] + nrm((L, N_HEADS_FOX), 0.1),
        "diff_lam_q1": nrm((L, DIFF_HALF), 0.1),
        "diff_lam_k1": nrm((L, DIFF_HALF), 0.1),
        "diff_lam_q2": nrm((L, DIFF_HALF), 0.1),
        "diff_lam_k2": nrm((L, DIFF_HALF), 0.1),
        "diff_subln_g": gain((L, hd)),
        "nsa_pe_k": nrm((L, CMP_LEN, hd), 0.1),
        "nsa_pe_v": nrm((L, CMP_LEN, hd), 0.1),
        "nsa_wk1": nrm((L, CMP_LEN * hd, hd), (CMP_LEN * hd) ** -0.5),
        "nsa_wk2": nrm((L, hd, hd), hd ** -0.5),
        "nsa_wv1": nrm((L, CMP_LEN * hd, hd), (CMP_LEN * hd) ** -0.5),
        "nsa_wv2": nrm((L, hd, hd), hd ** -0.5),
        "w_out": nrm((L, n_out, d), n_out ** -0.5),
        "mix_post_g": gain((L, d)),
        "ffn2_pre_g": gain((L, d)),
        "ffn2_w_gate": nrm((L, d, f), d ** -0.5),
        "ffn2_w_up": nrm((L, d, f), d ** -0.5),
        "ffn2_w_down": nrm((L, f, d), f ** -0.5),
        "ffn2_post_g": gain((L, d)),
        "ple_pre_g": gain((L, d)),
        "ple_w_gate": nrm((L, d, d), d ** -0.5),
        "ple_w_proj": nrm((L, PLE_DIM, d), PLE_DIM ** -0.5),
        "ple_post_g": gain((L, d)),
    }


def reference(x, p, ffn1_pre_g, ffn1_w_gate, ffn1_w_up, ffn1_w_down, ffn1_post_g,
              mix_pre_g, w_in, fox_bf, diff_lam_q1, diff_lam_k1, diff_lam_q2, diff_lam_k2,
              diff_subln_g, nsa_pe_k, nsa_pe_v, nsa_wk1, nsa_wk2, nsa_wv1, nsa_wv2,
              w_out, mix_post_g, ffn2_pre_g, ffn2_w_gate, ffn2_w_up, ffn2_w_down, ffn2_post_g,
              ple_pre_g, ple_w_gate, ple_w_proj, ple_post_g):
    B, T, _ = x.shape
    hd = HEAD_DIM
    pos = jnp.arange(T)
    offsets = np.cumsum(_in_proj_sizes())[:-1].tolist()
    h = x
    for li in range(DEPTH):
        f1 = swiglu(rms_norm(h, ffn1_pre_g[li]), ffn1_w_gate[li], ffn1_w_up[li], ffn1_w_down[li])
        h = h + 0.5 * rms_norm(f1, ffn1_post_g[li])

        u = rms_norm(h, mix_pre_g[li])
        z = u @ w_in[li]
        (qa, ka, va, qb, kb, vb, fb, qc, kcc, vcc, ksc, vsc, kwc, vwc, gc,
         qd, kd, vd) = jnp.split(z, offsets, axis=-1)

        lam_init = 0.8 - 0.6 * math.exp(-0.3 * li)
        lam = (jnp.exp(jnp.sum(diff_lam_q1[li].astype(jnp.float32) * diff_lam_k1[li].astype(jnp.float32)))
               - jnp.exp(jnp.sum(diff_lam_q2[li].astype(jnp.float32) * diff_lam_k2[li].astype(jnp.float32)))
               + lam_init)
        qa = rope(qa.reshape(B, T, 2 * N_HEADS_DIFF, DIFF_HALF), pos).reshape(B, T, N_HEADS_DIFF, 2, DIFF_HALF)
        ka = rope(ka.reshape(B, T, 2 * N_HEADS_DIFF, DIFF_HALF), pos).reshape(B, T, N_HEADS_DIFF, 2, DIFF_HALF)
        o_a = diff_attention(qa, ka, va.reshape(B, T, N_HEADS_DIFF, hd), lam, lam_init, diff_subln_g[li])

        o_b = forgetting_attention(qb.reshape(B, T, N_HEADS_FOX, hd), kb.reshape(B, T, N_HEADS_FOX, hd),
                                   vb.reshape(B, T, N_HEADS_FOX, hd), fb + fox_bf[li])

        kvs = (B, T, N_KV_NSA, hd)
        o_c = nsa_attention(rope(qc.reshape(B, T, N_HEADS_NSA, hd), pos),
                            rope(kcc.reshape(kvs), pos), vcc.reshape(kvs),
                            rope(ksc.reshape(kvs), pos), vsc.reshape(kvs),
                            rope(kwc.reshape(kvs), pos), vwc.reshape(kvs),
                            jax.nn.sigmoid(gc.reshape(B, T, N_HEADS_NSA, 3)),
                            nsa_pe_k[li], nsa_pe_v[li], nsa_wk1[li], nsa_wk2[li], nsa_wv1[li], nsa_wv2[li])

        o_d = dilated_attention(rope(qd.reshape(B, T, N_HEADS_DIL, hd), pos),
                                rope(kd.reshape(B, T, N_HEADS_DIL, hd), pos),
                                vd.reshape(B, T, N_HEADS_DIL, hd))

        o = jnp.concatenate([o_a.reshape(B, T, -1), o_b.reshape(B, T, -1),
                             o_c.reshape(B, T, -1), o_d.reshape(B, T, -1)], axis=-1)
        h = h + rms_norm(o @ w_out[li], mix_post_g[li])

        f2 = swiglu(rms_norm(h, ffn2_pre_g[li]), ffn2_w_gate[li], ffn2_w_up[li], ffn2_w_down[li])
        h = h + 0.5 * rms_norm(f2, ffn2_post_g[li])

        gate = jax.nn.sigmoid(rms_norm(h, ple_pre_g[li]) @ ple_w_gate[li])
        h = h + rms_norm(gate * (p[li] @ ple_w_proj[li]), ple_post_g[li])
    return h
```

```python
import functools
import math

import numpy as np
import jax
import jax.numpy as jnp
from jax import lax
from jax.experimental import pallas as pl
from jax.experimental.pallas import tpu as pltpu

F32 = jnp.float32
BF16 = jnp.bfloat16

HEAD_DIM = 128
N_HEADS_DIFF = 8
N_HEADS_FOX = 7
N_HEADS_NSA = 8
N_KV_NSA = 2
N_HEADS_DIL = 9
ROPE_THETA = 10000.0
EPS = 1e-6
CMP_LEN = 32
CMP_STRIDE = 16
SEL_LEN = 64
SEL_TOPN = 16
WIN_LEN = 512
DIL_PATTERNS = ((128, 1), (512, 4), (2048, 16))
NEG_INF = -1e30
POS_BIG = 1e30

V7X_VMEM_LIMIT_BYTES = 56 * 1024 * 1024
LANES = 128
FF_PAD = 512
ATT_TQ = 256
ATT_TK = 256

COL_QA, COL_KA = 0, 8
COL_QC, COL_KCC, COL_KSC, COL_KWC, COL_QD, COL_KD = 16, 24, 26, 28, 30, 39
COL_VA, COL_QB, COL_KB, COL_VB, COL_VCC, COL_VSC, COL_VWC, COL_VD = 48, 56, 63, 70, 77, 79, 81, 83
N_COLBLK = 92
N_ROPE64_BLK = 16
N_ROPE128_BLK = 32
IN_TN = 512


def _cparams(sem):
    return pltpu.CompilerParams(dimension_semantics=sem, vmem_limit_bytes=V7X_VMEM_LIMIT_BYTES)


def _pick(n, prefs):
    for p in prefs:
        if n % p == 0:
            return p
    return n


def _rmsnorm_kernel(x_ref, g_ref, o_ref):
    x = x_ref[...]
    ms = jnp.mean(x * x, axis=-1, keepdims=True)
    o_ref[...] = (x * lax.rsqrt(ms + EPS) * g_ref[...]).astype(o_ref.dtype)


def _rmsnorm(x, g):
    m, d = x.shape
    bm = _pick(m, (256, 128, 64, 32, 16, 8))
    return pl.pallas_call(
        _rmsnorm_kernel,
        grid=(m // bm,),
        in_specs=[pl.BlockSpec((bm, d), lambda i: (i, 0)), pl.BlockSpec((1, d), lambda i: (0, 0))],
        out_specs=pl.BlockSpec((bm, d), lambda i: (i, 0)),
        out_shape=jax.ShapeDtypeStruct((m, d), BF16),
        compiler_params=_cparams(("parallel",)),
        name="rmsnorm",
    )(x, g.reshape(1, d))


def _gateup_kernel(x_ref, wg_ref, wu_ref, o_ref):
    x = x_ref[...]
    g = jnp.dot(x, wg_ref[...], preferred_element_type=F32)
    u = jnp.dot(x, wu_ref[...], preferred_element_type=F32)
    o_ref[...] = (g * jax.nn.sigmoid(g) * u).astype(o_ref.dtype)


def _gateup(xn, wg, wu):
    m, d = xn.shape
    f = wg.shape[1]
    bm = _pick(m, (1024, 512, 256, 128))
    bf = _pick(f, (512, 256, 128))
    return pl.pallas_call(
        _gateup_kernel,
        grid=(m // bm, f // bf),
        in_specs=[pl.BlockSpec((bm, d), lambda i, j: (i, 0)),
                  pl.BlockSpec((d, bf), lambda i, j: (0, j)),
                  pl.BlockSpec((d, bf), lambda i, j: (0, j))],
        out_specs=pl.BlockSpec((bm, bf), lambda i, j: (i, j)),
        out_shape=jax.ShapeDtypeStruct((m, f), BF16),
        compiler_params=_cparams(("parallel", "arbitrary")),
        name="ffn_gateup",
    )(xn, wg, wu)


EPI_ROWS = 32


def _proj_kernel(*refs, ple, coef, nk, emit_next):
    a_ref, w_ref, h_ref, gpost_ref, gnext_ref = refs[:5]
    refs = refs[5:]
    if ple:
        p_ref, wp_ref = refs[:2]
        refs = refs[2:]
    o_ref = refs[0]
    xn_ref = refs[1] if emit_next else None
    k = pl.program_id(1)
    d = jnp.dot(a_ref[...], w_ref[...], preferred_element_type=F32)

    @pl.when(k == 0)
    def _():
        o_ref[...] = d

    @pl.when(k > 0)
    def _():
        o_ref[...] += d

    @pl.when(k == nk - 1)
    def _():
        bm = o_ref.shape[0]
        rows = min(EPI_ROWS, bm)

        def body(r, carry):
            r0 = pl.multiple_of(r * rows, rows)
            sl = pl.ds(r0, rows)
            f = o_ref[sl, :]
            if ple:
                pp = jnp.dot(p_ref[sl, :].astype(BF16), wp_ref[...], preferred_element_type=F32)
                f = jax.nn.sigmoid(f) * pp
            y = f * lax.rsqrt(jnp.mean(f * f, axis=-1, keepdims=True) + EPS) * gpost_ref[...]
            hn = h_ref[sl, :] + coef * y
            o_ref[sl, :] = hn
            if emit_next:
                ms = jnp.mean(hn * hn, axis=-1, keepdims=True)
                xn_ref[sl, :] = (hn * lax.rsqrt(ms + EPS) * gnext_ref[...]).astype(xn_ref.dtype)
            return carry

        lax.fori_loop(0, bm // rows, body, 0)


def _proj_residual(a, w, h, g_post, g_next, coef, p=None, wp=None):
    m, kdim = a.shape
    d = w.shape[1]
    ple = p is not None
    emit_next = g_next is not None
    bm = _pick(m, (512, 256, 128))
    bk = _pick(kdim, (512, 256, 128))
    nk = kdim // bk
    in_specs = [pl.BlockSpec((bm, bk), lambda i, k: (i, k)),
                pl.BlockSpec((bk, d), lambda i, k: (k, 0)),
                pl.BlockSpec((bm, d), lambda i, k: (i, 0), pipeline_mode=pl.Buffered(1)),
                pl.BlockSpec((1, d), lambda i, k: (0, 0)),
                pl.BlockSpec((1, d), lambda i, k: (0, 0))]
    args = [a, w, h, g_post.reshape(1, d), (g_next if emit_next else g_post).reshape(1, d)]
    if ple:
        in_specs += [pl.BlockSpec((bm, p.shape[1]), lambda i, k: (i, 0)),
                     pl.BlockSpec(wp.shape, lambda i, k: (0, 0))]
        args += [p, wp]
    out_specs = [pl.BlockSpec((bm, d), lambda i, k: (i, 0))]
    out_shape = [jax.ShapeDtypeStruct((m, d), F32)]
    if emit_next:
        out_specs.append(pl.BlockSpec((bm, d), lambda i, k: (i, 0)))
        out_shape.append(jax.ShapeDtypeStruct((m, d), BF16))
    res = pl.pallas_call(
        functools.partial(_proj_kernel, ple=ple, coef=coef, nk=nk, emit_next=emit_next),
        grid=(m // bm, nk),
        in_specs=in_specs,
        out_specs=out_specs,
        out_shape=out_shape,
        compiler_params=_cparams(("parallel", "arbitrary")),
        name="ple_residual" if ple else "proj_residual",
    )(*args)
    return (res[0], res[1]) if emit_next else (res[0], None)


def _rope_tile(acc, cos_ref, sin_ref, half, n_slabs):
    cos = cos_ref[...]
    sin = sin_ref[...]
    outs = []
    for s in range(n_slabs):
        a = acc[:, s * LANES:(s + 1) * LANES]
        if half == LANES // 2:
            partner = pltpu.roll(a, LANES // 2, axis=1)
        else:
            lane = lax.broadcasted_iota(jnp.int32, a.shape, 1)
            fwd = pltpu.roll(a, LANES - half, axis=1)
            bwd = pltpu.roll(a, half, axis=1)
            partner = jnp.where((lane & (2 * half - 1)) < half, fwd, bwd)
        outs.append(a * cos + partner * sin)
    return jnp.concatenate(outs, axis=1)


def _inproj_kernel(x_ref, w_ref, ws_ref, scale_ref, c64_ref, s64_ref, c128_ref, s128_ref,
                   o_ref, os_ref, *, n64, n128):
    j = pl.program_id(1)
    x = x_ref[...]
    acc = jnp.dot(x, w_ref[...], preferred_element_type=F32)
    n_slabs = acc.shape[1] // LANES
    scale = scale_ref[...]

    @pl.when(j == 0)
    def _():
        os_ref[...] = jnp.dot(x, ws_ref[...], preferred_element_type=F32)

    @pl.when(j < n64)
    def _():
        o_ref[...] = (_rope_tile(acc, c64_ref, s64_ref, 32, n_slabs) * scale).astype(o_ref.dtype)

    @pl.when((j >= n64) & (j < n64 + n128))
    def _():
        o_ref[...] = (_rope_tile(acc, c128_ref, s128_ref, 64, n_slabs) * scale).astype(o_ref.dtype)

    @pl.when(j >= n64 + n128)
    def _():
        o_ref[...] = (acc * scale).astype(o_ref.dtype)


def _inproj(xn, w_main, w_small, scale_row, tabs, seq):
    m, d = xn.shape
    n = w_main.shape[1]
    bm = _pick(seq, (1024, 512, 256))
    bn = IN_TN
    tpb = seq // bm
    tab_spec = pl.BlockSpec((bm, LANES), lambda i, j: (i % tpb, 0))
    return pl.pallas_call(
        functools.partial(_inproj_kernel, n64=N_ROPE64_BLK * LANES // bn, n128=N_ROPE128_BLK * LANES // bn),
        grid=(m // bm, n // bn),
        in_specs=[pl.BlockSpec((bm, d), lambda i, j: (i, 0)),
                  pl.BlockSpec((d, bn), lambda i, j: (0, j)),
                  pl.BlockSpec((d, LANES), lambda i, j: (0, 0)),
                  pl.BlockSpec((1, bn), lambda i, j: (0, j)),
                  tab_spec, tab_spec, tab_spec, tab_spec],
        out_specs=[pl.BlockSpec((bm, bn), lambda i, j: (i, j)),
                   pl.BlockSpec((bm, LANES), lambda i, j: (i, 0))],
        out_shape=[jax.ShapeDtypeStruct((m, n), BF16), jax.ShapeDtypeStruct((m, LANES), F32)],
        compiler_params=_cparams(("parallel", "arbitrary")),
        name="in_proj_rope",
    )(xn, w_main, w_small, scale_row, *tabs)


def _attend(q, k_ref, v_ref, kc_lo, kc_hi, t_rows, mask_fn, bias_fn=None, tk=ATT_TK):
    rows, dh = q.shape

    def body(kc, carry):
        m, l, acc = carry
        k0 = pl.multiple_of(kc * tk, tk)
        kb = k_ref[pl.ds(k0, tk), :]
        vb = v_ref[pl.ds(k0, tk), :]
        s = lax.dot_general(q, kb, (((1,), (1,)), ((), ())), preferred_element_type=F32)
        if bias_fn is not None:
            s = s + bias_fn(k0)
        kpos = k0 + lax.broadcasted_iota(jnp.int32, (1, tk), 1)
        valid = mask_fn(t_rows, kpos, k0)
        s = jnp.where(valid, s, NEG_INF)
        m_new = jnp.maximum(m, jnp.max(s, axis=1, keepdims=True))
        p = jnp.where(valid, jnp.exp(s - m_new), 0.0)
        alpha = jnp.exp(m - m_new)
        l = alpha * l + jnp.sum(p, axis=1, keepdims=True)
        acc = alpha * acc + jnp.dot(p.astype(BF16), vb, preferred_element_type=F32)
        return m_new, l, acc

    init = (jnp.full((rows, 1), NEG_INF, F32), jnp.zeros((rows, 1), F32), jnp.zeros((rows, dh), F32))
    return lax.fori_loop(kc_lo, kc_hi, body, init)


def _normalize(acc, l):
    return acc / jnp.where(l > 0, l, 1.0)


def _causal_mask(t_rows, kpos, k0):
    return kpos <= t_rows


def _row_positions(q0, tq, reps):
    r = lax.broadcasted_iota(jnp.int32, (reps * tq, 1), 0)
    return q0 + (r & (tq - 1))


def _diff_kernel(lam_ref, q_ref, k_ref, v_ref, g_ref, o_ref, *, out_scale):
    qi = pl.program_id(2)
    tq = q_ref.shape[0]
    q = q_ref[...]
    lane = lax.broadcasted_iota(jnp.int32, q.shape, 1)
    zero = jnp.zeros_like(q)
    q2 = jnp.concatenate([jnp.where(lane < HEAD_DIM // 2, q, zero),
                          jnp.where(lane >= HEAD_DIM // 2, q, zero)], axis=0)
    t_rows = _row_positions(qi * tq, tq, 2)
    _, l, acc = _attend(q2, k_ref, v_ref, 0, qi + 1, t_rows, _causal_mask)
    o = _normalize(acc, l)
    a = o[:tq] - lam_ref[0] * o[tq:]
    y = a * lax.rsqrt(jnp.mean(a * a, axis=-1, keepdims=True) + EPS) * g_ref[...]
    o_ref[...] = (y * out_scale).astype(o_ref.dtype)


def _diff_attention(z3, lam, subln_g, lam_init):
    b, t, _ = z3.shape
    tq = ATT_TQ
    return pl.pallas_call(
        functools.partial(_diff_kernel, out_scale=1.0 - lam_init),
        grid=(b, N_HEADS_DIFF, t // tq),
        in_specs=[pl.BlockSpec(memory_space=pltpu.SMEM),
                  pl.BlockSpec((None, tq, LANES), lambda bi, h, qi: (bi, qi, COL_QA + h)),
                  pl.BlockSpec((None, t, LANES), lambda bi, h, qi: (bi, 0, COL_KA + h)),
                  pl.BlockSpec((None, t, LANES), lambda bi, h, qi: (bi, 0, COL_VA + h)),
                  pl.BlockSpec((1, LANES), lambda bi, h, qi: (0, 0))],
        out_specs=pl.BlockSpec((None, tq, LANES), lambda bi, h, qi: (bi, qi, h)),
        out_shape=jax.ShapeDtypeStruct((b, t, N_HEADS_DIFF * HEAD_DIM), BF16),
        compiler_params=_cparams(("parallel", "parallel", "arbitrary")),
        name="diff_attention",
    )(lam.reshape(1), z3, z3, z3, subln_g.reshape(1, LANES))


def _small_prep_kernel(zs_ref, bf_ref, c_ref, ct_ref, g_ref):
    zs = zs_ref[...]
    t = zs.shape[0]
    c = jax.nn.log_sigmoid(zs + bf_ref[...])
    row = lax.broadcasted_iota(jnp.int32, c.shape, 0)
    shift = 1
    while shift < t:
        c = c + jnp.where(row >= shift, pltpu.roll(c, shift, axis=0), 0.0)
        shift *= 2
    c_ref[...] = c
    ct_ref[...] = jnp.transpose(c)[:8, :]
    g_ref[...] = jax.nn.sigmoid(zs)


def _small_prep(zs3, bf_row):
    b, t, _ = zs3.shape
    return pl.pallas_call(
        _small_prep_kernel,
        grid=(b,),
        in_specs=[pl.BlockSpec((None, t, LANES), lambda bi: (bi, 0, 0)),
                  pl.BlockSpec((1, LANES), lambda bi: (0, 0))],
        out_specs=[pl.BlockSpec((None, t, LANES), lambda bi: (bi, 0, 0)),
                   pl.BlockSpec((None, 8, t), lambda bi: (bi, 0, 0)),
                   pl.BlockSpec((None, t, LANES), lambda bi: (bi, 0, 0))],
        out_shape=[jax.ShapeDtypeStruct((b, t, LANES), F32),
                   jax.ShapeDtypeStruct((b, 8, t), F32),
                   jax.ShapeDtypeStruct((b, t, LANES), F32)],
        compiler_params=_cparams(("parallel",)),
        name="forget_cumsum_gates",
    )(zs3, bf_row)


def _fox_kernel(q_ref, k_ref, v_ref, c_ref, ct_ref, o_ref):
    h = pl.program_id(1)
    qi = pl.program_id(2)
    tq = q_ref.shape[0]
    c = c_ref[...]
    lane = lax.broadcasted_iota(jnp.int32, c.shape, 1)
    cq = jnp.sum(jnp.where(lane == h, c, 0.0), axis=1, keepdims=True)

    def bias_fn(k0):
        return cq - ct_ref[pl.ds(h, 1), pl.ds(k0, ATT_TK)]

    t_rows = _row_positions(qi * tq, tq, 1)
    _, l, acc = _attend(q_ref[...], k_ref, v_ref, 0, qi + 1, t_rows, _causal_mask, bias_fn)
    o_ref[...] = _normalize(acc, l).astype(o_ref.dtype)


def _fox_attention(z3, c_col, c_row):
    b, t, _ = z3.shape
    tq = ATT_TQ
    return pl.pallas_call(
        _fox_kernel,
        grid=(b, N_HEADS_FOX, t // tq),
        in_specs=[pl.BlockSpec((None, tq, LANES), lambda bi, h, qi: (bi, qi, COL_QB + h)),
                  pl.BlockSpec((None, t, LANES), lambda bi, h, qi: (bi, 0, COL_KB + h)),
                  pl.BlockSpec((None, t, LANES), lambda bi, h, qi: (bi, 0, COL_VB + h)),
                  pl.BlockSpec((None, tq, LANES), lambda bi, h, qi: (bi, qi, 0)),
                  pl.BlockSpec((None, 8, t), lambda bi, h, qi: (bi, 0, 0))],
        out_specs=pl.BlockSpec((None, tq, LANES), lambda bi, h, qi: (bi, qi, h)),
        out_shape=jax.ShapeDtypeStruct((b, t, N_HEADS_FOX * HEAD_DIM), BF16),
        compiler_params=_cparams(("parallel", "parallel", "arbitrary")),
        name="forgetting_attention",
    )(z3, z3, z3, c_col, c_row)


def _compress_one(x_ref, xs_ref, pe_ref, w1_ref, w2_ref, o_ref):
    t = x_ref.shape[0]
    nb = t // CMP_STRIDE
    xs_ref[...] = x_ref[...].astype(F32)
    pe = pe_ref[...]
    lo, hi = [], []
    for i in range(CMP_STRIDE):
        xi = xs_ref[pl.ds(i, nb, stride=CMP_STRIDE), :]
        lo.append((xi + pe[i:i + 1, :]).astype(BF16))
        hi.append((xi + pe[CMP_STRIDE + i:CMP_STRIDE + i + 1, :]).astype(BF16))
    half = CMP_STRIDE * HEAD_DIM
    a = jnp.dot(jnp.concatenate(lo, axis=1), w1_ref[:half, :], preferred_element_type=F32)
    bb = jnp.dot(jnp.concatenate(hi, axis=1), w1_ref[half:, :], preferred_element_type=F32)
    y = a + pltpu.roll(bb, nb - 1, axis=0)
    out = jnp.dot(jax.nn.gelu(y).astype(BF16), w2_ref[...], preferred_element_type=F32)
    row = lax.broadcasted_iota(jnp.int32, out.shape, 0)
    o_ref[...] = jnp.where(row < nb - 1, out, 0.0).astype(o_ref.dtype)


def _compress_kernel(k_ref, v_ref, pek_ref, pev_ref, wk1_ref, wk2_ref, wv1_ref, wv2_ref,
                     kc_ref, vc_ref, xs_ref):
    _compress_one(k_ref, xs_ref, pek_ref, wk1_ref, wk2_ref, kc_ref)
    _compress_one(v_ref, xs_ref, pev_ref, wv1_ref, wv2_ref, vc_ref)


def _nsa_compress(z3, pe_k, pe_v, wk1, wk2, wv1, wv2):
    b, t, _ = z3.shape
    nb = t // CMP_STRIDE
    full = lambda a: pl.BlockSpec(a.shape, lambda bi, g: (0,) * a.ndim)
    return pl.pallas_call(
        _compress_kernel,
        grid=(b, N_KV_NSA),
        in_specs=[pl.BlockSpec((None, t, LANES), lambda bi, g: (bi, 0, COL_KCC + g)),
                  pl.BlockSpec((None, t, LANES), lambda bi, g: (bi, 0, COL_VCC + g)),
                  full(pe_k), full(pe_v), full(wk1), full(wk2), full(wv1), full(wv2)],
        out_specs=[pl.BlockSpec((None, None, nb, LANES), lambda bi, g: (bi, g, 0, 0)),
                   pl.BlockSpec((None, None, nb, LANES), lambda bi, g: (bi, g, 0, 0))],
        out_shape=[jax.ShapeDtypeStruct((b, N_KV_NSA, nb, LANES), BF16),
                   jax.ShapeDtypeStruct((b, N_KV_NSA, nb, LANES), BF16)],
        scratch_shapes=[pltpu.VMEM((t, LANES), F32)],
        compiler_params=_cparams(("parallel", "parallel")),
        name="nsa_compress",
    )(z3, z3, pe_k, pe_v, wk1, wk2, wv1, wv2)


def _nsa_kernel(q_ref, kc_ref, vc_ref, ks_ref, vs_ref, kw_ref, vw_ref, gate_ref, o_ref, *, seq):
    g = pl.program_id(1)
    qi = pl.program_id(2)
    tq = q_ref.shape[0]
    reps = N_HEADS_NSA // N_KV_NSA
    q0 = qi * tq
    q4 = jnp.concatenate([q_ref[:, r * LANES:(r + 1) * LANES] for r in range(reps)], axis=0)
    t_rows = _row_positions(q0, tq, reps)

    nb = kc_ref.shape[0]
    n_cmp = (seq - CMP_LEN) // CMP_STRIDE + 1
    s = lax.dot_general(q4, kc_ref[...], (((1,), (1,)), ((), ())), preferred_element_type=F32)
    cidx = lax.broadcasted_iota(jnp.int32, (1, nb), 1)
    valid = (cidx * CMP_STRIDE + (CMP_LEN - 1) <= t_rows) & (cidx < n_cmp)
    s = jnp.where(valid, s, NEG_INF)
    e = jnp.where(valid, jnp.exp(s - jnp.max(s, axis=1, keepdims=True)), 0.0)
    den = jnp.sum(e, axis=1, keepdims=True)
    p_cmp = e / jnp.where(den > 0, den, 1.0)
    o_cmp = jnp.dot(p_cmp.astype(BF16), vc_ref[...], preferred_element_type=F32)

    n_sel = seq // SEL_LEN
    n_sel_pad = LANES
    p_sum = p_cmp[:tq]
    for r in range(1, reps):
        p_sum = p_sum + p_cmp[r * tq:(r + 1) * tq]
    p_hi = p_sum.astype(BF16)
    p_lo = (p_sum - p_hi.astype(F32)).astype(BF16)
    srow = lax.broadcasted_iota(jnp.int32, (n_sel_pad, nb), 0)
    ccol = lax.broadcasted_iota(jnp.int32, (n_sel_pad, nb), 1)
    overlap = ((ccol * CMP_STRIDE < srow * SEL_LEN + SEL_LEN) &
               (ccol * CMP_STRIDE + CMP_LEN > srow * SEL_LEN) & (ccol < n_cmp) & (srow < n_sel))
    ov = jnp.where(overlap, 1.0, 0.0).astype(BF16)
    nt = (((1,), (1,)), ((), ()))
    imp_t = (lax.dot_general(ov, p_hi, nt, preferred_element_type=F32) +
             lax.dot_general(ov, p_lo, nt, preferred_element_type=F32))
    n_rank = ((n_sel + 7) // 8) * 8
    imp_t = imp_t[:n_rank]
    sblk = lax.broadcasted_iota(jnp.int32, (n_rank, tq), 0)
    tpos = q0 + lax.broadcasted_iota(jnp.int32, (n_rank, tq), 1)
    cur = tpos // SEL_LEN
    forced = (sblk == 0) | (sblk == cur) | (sblk == cur - 1)
    val = jnp.where(forced, POS_BIG, jnp.where(sblk * SEL_LEN <= tpos, imp_t, NEG_INF))
    val = jnp.where(sblk < n_sel, val, -2.0 * POS_BIG)
    rank = jnp.zeros((n_rank, tq), F32)
    for s2 in range(n_sel):
        other = val[s2:s2 + 1, :]
        tie = jnp.where(sblk > s2, 1.0, 0.0)
        rank = rank + jnp.where(other > val, 1.0, jnp.where(other == val, tie, 0.0))
    n_top = min(SEL_TOPN, n_sel)
    sel_t = jnp.where((rank < n_top) & (sblk < n_sel), 1.0, 0.0)
    if n_rank < n_sel_pad:
        sel_t = jnp.concatenate([sel_t, jnp.zeros((n_sel_pad - n_rank, tq), F32)], axis=0)
    sel = jnp.transpose(sel_t).astype(BF16)

    def sel_mask(t_r, kpos, k0):
        erow = lax.broadcasted_iota(jnp.int32, (n_sel_pad, ATT_TK), 0)
        ecol = k0 + lax.broadcasted_iota(jnp.int32, (n_sel_pad, ATT_TK), 1)
        expand = jnp.where(ecol // SEL_LEN == erow, 1.0, 0.0).astype(BF16)
        chosen = jnp.dot(sel, expand, preferred_element_type=F32)
        chosen = jnp.concatenate([chosen] * reps, axis=0)
        return (chosen > 0.5) & (kpos <= t_r)

    _, l_s, acc_s = _attend(q4, ks_ref, vs_ref, 0, qi + 1, t_rows, sel_mask)
    o_sel = _normalize(acc_s, l_s)

    def win_mask(t_r, kpos, k0):
        return (kpos <= t_r) & (kpos > t_r - WIN_LEN)

    lo = jnp.maximum(qi - (WIN_LEN + ATT_TK - 1) // ATT_TK, 0)
    _, l_w, acc_w = _attend(q4, kw_ref, vw_ref, lo, qi + 1, t_rows, win_mask)
    o_win = _normalize(acc_w, l_w)

    gates = gate_ref[...]
    outs = []
    for r in range(reps):
        col = 7 + 3 * (g * reps + r)
        lane = lax.broadcasted_iota(jnp.int32, gates.shape, 1)
        rs = slice(r * tq, (r + 1) * tq)
        o_r = jnp.zeros((tq, LANES), F32)
        for br, o_br in enumerate((o_cmp, o_sel, o_win)):
            gcol = jnp.sum(jnp.where(lane == col + br, gates, 0.0), axis=1, keepdims=True)
            o_r = o_r + gcol * o_br[rs]
        outs.append(o_r)
    o_ref[...] = jnp.concatenate(outs, axis=1).astype(o_ref.dtype)


def _nsa_attention(z3, kc, vc, gates):
    b, t, _ = z3.shape
    tq = ATT_TQ
    reps = N_HEADS_NSA // N_KV_NSA
    nb = kc.shape[2]
    kv = lambda col: pl.BlockSpec((None, t, LANES), lambda bi, g, qi: (bi, 0, col + g))
    return pl.pallas_call(
        functools.partial(_nsa_kernel, seq=t),
        grid=(b, N_KV_NSA, t // tq),
        in_specs=[pl.BlockSpec((None, tq, reps * LANES), lambda bi, g, qi: (bi, qi, COL_QC // reps + g)),
                  pl.BlockSpec((None, None, nb, LANES), lambda bi, g, qi: (bi, g, 0, 0)),
                  pl.BlockSpec((None, None, nb, LANES), lambda bi, g, qi: (bi, g, 0, 0)),
                  kv(COL_KSC), kv(COL_VSC), kv(COL_KWC), kv(COL_VWC),
                  pl.BlockSpec((None, tq, LANES), lambda bi, g, qi: (bi, qi, 0))],
        out_specs=pl.BlockSpec((None, tq, reps * LANES), lambda bi, g, qi: (bi, qi, g)),
        out_shape=jax.ShapeDtypeStruct((b, t, N_HEADS_NSA * HEAD_DIM), BF16),
        compiler_params=_cparams(("parallel", "parallel", "arbitrary")),
        name="nsa_attention",
    )(z3, kc, vc, z3, z3, z3, z3, gates)


def _dil_kernel(*refs):
    n_g = len(DIL_PATTERNS)
    q_refs, k_refs, v_refs = refs[:n_g], refs[n_g:2 * n_g], refs[2 * n_g:3 * n_g]
    o_ref = refs[3 * n_g]
    qi = pl.program_id(2)
    tq = o_ref.shape[0]
    t_rows = _row_positions(qi * tq, tq, 1)
    parts = []
    for gi, (w, r) in enumerate(DIL_PATTERNS):
        def mask(t_r, kpos, k0, w=w, r=r):
            dist = t_r - kpos
            return (dist >= 0) & (dist <= w) & ((dist & (r - 1)) == 0)

        lo = jnp.maximum(qi - (w + ATT_TK - 1) // ATT_TK, 0)
        parts.append(_attend(q_refs[gi][...], k_refs[gi], v_refs[gi], lo, qi + 1, t_rows, mask))
    lses = [m + jnp.log(l) for (m, l, _) in parts]
    top = functools.reduce(jnp.maximum, lses)
    ws = [jnp.exp(x - top) for x in lses]
    tot = functools.reduce(lambda a, b2: a + b2, ws)
    out = jnp.zeros((tq, LANES), F32)
    for wgt, (m, l, acc) in zip(ws, parts):
        out = out + (wgt / tot) * (acc / l)
    o_ref[...] = out.astype(o_ref.dtype)


def _dilated_attention(z3):
    b, t, _ = z3.shape
    tq = ATT_TQ
    n_g = len(DIL_PATTERNS)
    hg = N_HEADS_DIL // n_g
    qs = [pl.BlockSpec((None, tq, LANES), lambda bi, j, qi, gi=gi: (bi, qi, COL_QD + gi * hg + j)) for gi in range(n_g)]
    ks = [pl.BlockSpec((None, t, LANES), lambda bi, j, qi, gi=gi: (bi, 0, COL_KD + gi * hg + j)) for gi in range(n_g)]
    vs = [pl.BlockSpec((None, t, LANES), lambda bi, j, qi, gi=gi: (bi, 0, COL_VD + gi * hg + j)) for gi in range(n_g)]
    return pl.pallas_call(
        _dil_kernel,
        grid=(b, hg, t // tq),
        in_specs=qs + ks + vs,
        out_specs=pl.BlockSpec((None, tq, LANES), lambda bi, j, qi: (bi, qi, j)),
        out_shape=jax.ShapeDtypeStruct((b, t, hg * HEAD_DIM), BF16),
        compiler_params=_cparams(("parallel", "parallel", "arbitrary")),
        name="dilated_attention",
    )(*([z3] * (3 * n_g)))


def _in_proj_sizes():
    hd = HEAD_DIM
    kv = N_KV_NSA * hd
    return (N_HEADS_DIFF * hd, N_HEADS_DIFF * hd, N_HEADS_DIFF * hd,
            N_HEADS_FOX * hd, N_HEADS_FOX * hd, N_HEADS_FOX * hd, N_HEADS_FOX,
            N_HEADS_NSA * hd, kv, kv, kv, kv, kv, kv, 3 * N_HEADS_NSA,
            N_HEADS_DIL * hd, N_HEADS_DIL * hd, N_HEADS_DIL * hd)


def _pack_w_in(w):
    offs = np.concatenate([[0], np.cumsum(_in_proj_sizes())])
    names = ("qa", "ka", "va", "qb", "kb", "vb", "fb", "qc", "kcc", "vcc", "ksc", "vsc", "kwc", "vwc", "gc",
             "qd", "kd", "vd")
    sec = {n: w[:, int(offs[i]):int(offs[i + 1])] for i, n in enumerate(names)}
    order = ("qa", "ka", "qc", "kcc", "ksc", "kwc", "qd", "kd", "va", "qb", "kb", "vb", "vcc", "vsc", "vwc", "vd")
    main = jnp.concatenate([sec[n] for n in order], axis=1).astype(BF16)
    n_small = sec["fb"].shape[1] + sec["gc"].shape[1]
    small = jnp.concatenate([sec["fb"], sec["gc"], jnp.zeros((w.shape[0], LANES - n_small), w.dtype)], axis=1)
    scales = {"qa": (HEAD_DIM // 2) ** -0.5, "qb": HEAD_DIM ** -0.5, "qc": HEAD_DIM ** -0.5, "qd": HEAD_DIM ** -0.5}
    scale_row = jnp.concatenate([jnp.full((1, sec[n].shape[1]), scales.get(n, 1.0), F32) for n in order], axis=1)
    return main, small.astype(BF16), scale_row


def _rope_tables(t):
    pos = jnp.arange(t, dtype=F32)[:, None]
    tabs = []
    for d in (HEAD_DIM // 2, HEAD_DIM):
        half = d // 2
        inv = ROPE_THETA ** (-jnp.arange(half, dtype=F32) * 2.0 / d)
        ang = pos * inv[None, :]
        cos, sin = jnp.cos(ang), jnp.sin(ang)
        reps = LANES // d
        tabs.append(jnp.tile(jnp.concatenate([cos, cos], axis=1), (1, reps)))
        tabs.append(jnp.tile(jnp.concatenate([-sin, sin], axis=1), (1, reps)))
    return tabs


def _pad_cols(w, mult):
    pad = (-w.shape[1]) % mult
    return jnp.pad(w, ((0, 0), (0, pad))) if pad else w


def _pad_rows(w, mult):
    pad = (-w.shape[0]) % mult
    return jnp.pad(w, ((0, pad), (0, 0))) if pad else w


def kernel(x, p, ffn1_pre_g, ffn1_w_gate, ffn1_w_up, ffn1_w_down, ffn1_post_g, mix_pre_g, w_in, fox_bf, diff_lam_q1, diff_lam_k1, diff_lam_q2, diff_lam_k2, diff_subln_g, nsa_pe_k, nsa_pe_v, nsa_wk1, nsa_wk2, nsa_wv1, nsa_wv2, w_out, mix_post_g, ffn2_pre_g, ffn2_w_gate, ffn2_w_up, ffn2_w_down, ffn2_post_g, ple_pre_g, ple_w_gate, ple_w_proj, ple_post_g):
    b, t, d = x.shape
    depth = w_in.shape[0]
    m = b * t
    tabs = _rope_tables(t)
    h = x.reshape(m, d)
    xn = _rmsnorm(h, ffn1_pre_g[0])

    def ffn(h, xn, wg, wu, wd, g_post, g_next):
        act = _gateup(xn, _pad_cols(wg, FF_PAD).astype(BF16), _pad_cols(wu, FF_PAD).astype(BF16))
        return _proj_residual(act, _pad_rows(wd, FF_PAD).astype(BF16), h, g_post, g_next, 0.5)

    for li in range(depth):
        h, xn = ffn(h, xn, ffn1_w_gate[li], ffn1_w_up[li], ffn1_w_down[li], ffn1_post_g[li], mix_pre_g[li])

        w_main, w_small, scale_row = _pack_w_in(w_in[li])
        z, zs = _inproj(xn, w_main, w_small, scale_row, tabs, t)
        z3 = z.reshape(b, t, z.shape[1])
        bf_row = jnp.pad(fox_bf[li].astype(F32), (0, LANES - N_HEADS_FOX)).reshape(1, LANES)
        c_col, c_row, gates = _small_prep(zs.reshape(b, t, LANES), bf_row)

        lam_init = 0.8 - 0.6 * math.exp(-0.3 * li)
        lam = (jnp.exp(jnp.sum(diff_lam_q1[li].astype(F32) * diff_lam_k1[li].astype(F32)))
               - jnp.exp(jnp.sum(diff_lam_q2[li].astype(F32) * diff_lam_k2[li].astype(F32))) + lam_init)
        o_a = _diff_attention(z3, lam, diff_subln_g[li].astype(F32), lam_init)
        o_b = _fox_attention(z3, c_col, c_row)
        kc, vc = _nsa_compress(z3, nsa_pe_k[li], nsa_pe_v[li], nsa_wk1[li].astype(BF16), nsa_wk2[li].astype(BF16),
                               nsa_wv1[li].astype(BF16), nsa_wv2[li].astype(BF16))
        o_c = _nsa_attention(z3, kc, vc, gates)
        o_d = _dilated_attention(z3)
        o = jnp.concatenate([o_a, o_b, o_c, o_d], axis=-1).reshape(m, -1)
        h, xn = _proj_residual(_pad_cols(o, FF_PAD), _pad_rows(w_out[li], FF_PAD).astype(BF16), h,
                               mix_post_g[li], ffn2_pre_g[li], 1.0)

        h, xn = ffn(h, xn, ffn2_w_gate[li], ffn2_w_up[li], ffn2_w_down[li], ffn2_post_g[li], ple_pre_g[li])

        g_next = ffn1_pre_g[li + 1] if li + 1 < depth else None
        h, xn = _proj_residual(xn, ple_w_gate[li].astype(BF16), h, ple_post_g[li], g_next, 1.0,
                               p=p[li].reshape(m, -1), wp=ple_w_proj[li].astype(BF16))
    return h.reshape(b, t, d)
```

```python
import functools
import math

import numpy as np
import jax
import jax.numpy as jnp
from jax import lax
from jax.experimental import pallas as pl
from jax.experimental.pallas import tpu as pltpu

F32 = jnp.float32
BF16 = jnp.bfloat16

HEAD_DIM = 128
N_HEADS_DIFF = 8
N_HEADS_FOX = 7
N_HEADS_NSA = 8
N_KV_NSA = 2
N_HEADS_DIL = 9
ROPE_THETA = 10000.0
EPS = 1e-6
CMP_LEN = 32
CMP_STRIDE = 16
SEL_LEN = 64
SEL_TOPN = 16
WIN_LEN = 512
DIL_PATTERNS = ((128, 1), (512, 4), (2048, 16))
NEG_INF = -1e30
POS_BIG = 1e30

V7X_VMEM_LIMIT_BYTES = 56 * 1024 * 1024
LANES = 128
FF_PAD = 512
ATT_TQ = 256
ATT_TK = 256

COL_QA, COL_KA = 0, 8
COL_QC, COL_KCC, COL_KSC, COL_KWC, COL_QD, COL_KD = 16, 24, 26, 28, 30, 39
COL_VA, COL_QB, COL_KB, COL_VB, COL_VCC, COL_VSC, COL_VWC, COL_VD = 48, 56, 63, 70, 77, 79, 81, 83
N_COLBLK = 92
N_ROPE64_BLK = 16
N_ROPE128_BLK = 32
IN_TN = 512


def _cparams(sem):
    return pltpu.CompilerParams(dimension_semantics=sem, vmem_limit_bytes=V7X_VMEM_LIMIT_BYTES)


def _pick(n, prefs):
    for p in prefs:
        if n % p == 0:
            return p
    return n


def _rmsnorm_kernel(x_ref, g_ref, o_ref):
    x = x_ref[...]
    ms = jnp.mean(x * x, axis=-1, keepdims=True)
    o_ref[...] = (x * lax.rsqrt(ms + EPS) * g_ref[...]).astype(o_ref.dtype)


def _rmsnorm(x, g):
    m, d = x.shape
    bm = _pick(m, (256, 128, 64, 32, 16, 8))
    return pl.pallas_call(
        _rmsnorm_kernel,
        grid=(m // bm,),
        in_specs=[pl.BlockSpec((bm, d), lambda i: (i, 0)), pl.BlockSpec((1, d), lambda i: (0, 0))],
        out_specs=pl.BlockSpec((bm, d), lambda i: (i, 0)),
        out_shape=jax.ShapeDtypeStruct((m, d), BF16),
        compiler_params=_cparams(("parallel",)),
        name="rmsnorm",
    )(x, g.reshape(1, d))


def _gateup_kernel(x_ref, wg_ref, wu_ref, o_ref):
    x = x_ref[...]
    g = jnp.dot(x, wg_ref[...], preferred_element_type=F32)
    u = jnp.dot(x, wu_ref[...], preferred_element_type=F32)
    o_ref[...] = (g * jax.nn.sigmoid(g) * u).astype(o_ref.dtype)


def _gateup(xn, wg, wu):
    m, d = xn.shape
    f = wg.shape[1]
    bm = _pick(m, (1024, 512, 256, 128))
    bf = _pick(f, (512, 256, 128))
    return pl.pallas_call(
        _gateup_kernel,
        grid=(m // bm, f // bf),
        in_specs=[pl.BlockSpec((bm, d), lambda i, j: (i, 0)),
                  pl.BlockSpec((d, bf), lambda i, j: (0, j)),
                  pl.BlockSpec((d, bf), lambda i, j: (0, j))],
        out_specs=pl.BlockSpec((bm, bf), lambda i, j: (i, j)),
        out_shape=jax.ShapeDtypeStruct((m, f), BF16),
        compiler_params=_cparams(("parallel", "arbitrary")),
        name="ffn_gateup",
    )(xn, wg, wu)


EPI_ROWS = 32


def _proj_kernel(*refs, ple, coef, nk, emit_next):
    a_ref, w_ref, h_ref, gpost_ref, gnext_ref = refs[:5]
    refs = refs[5:]
    if ple:
        p_ref, wp_ref = refs[:2]
        refs = refs[2:]
    o_ref = refs[0]
    xn_ref = refs[1] if emit_next else None
    k = pl.program_id(1)

    @pl.when(k == 0)
    def _():
        o_ref[...] = jnp.dot(a_ref[...], w_ref[...], preferred_element_type=F32)

    @pl.when(k > 0)
    def _():
        o_ref[...] += jnp.dot(a_ref[...], w_ref[...], preferred_element_type=F32)

    @pl.when(k == nk - 1)
    def _():
        bm = o_ref.shape[0]
        rows = min(EPI_ROWS, bm)

        def body(r, carry):
            r0 = pl.multiple_of(r * rows, rows)
            sl = pl.ds(r0, rows)
            f = o_ref[sl, :]
            if ple:
                pp = jnp.dot(p_ref[sl, :].astype(BF16), wp_ref[...], preferred_element_type=F32)
                f = jax.nn.sigmoid(f) * pp
            y = f * lax.rsqrt(jnp.mean(f * f, axis=-1, keepdims=True) + EPS) * gpost_ref[...]
            hn = h_ref[sl, :] + coef * y
            o_ref[sl, :] = hn
            if emit_next:
                ms = jnp.mean(hn * hn, axis=-1, keepdims=True)
                xn_ref[sl, :] = (hn * lax.rsqrt(ms + EPS) * gnext_ref[...]).astype(xn_ref.dtype)
            return carry

        lax.fori_loop(0, bm // rows, body, 0)


def _proj_residual(a, w, h, g_post, g_next, coef, p=None, wp=None):
    m, kdim = a.shape
    d = w.shape[1]
    ple = p is not None
    emit_next = g_next is not None
    bm = _pick(m, (512, 256, 128))
    bk = _pick(kdim, (512, 256, 128))
    nk = kdim // bk
    in_specs = [pl.BlockSpec((bm, bk), lambda i, k: (i, k)),
                pl.BlockSpec((bk, d), lambda i, k: (k, 0)),
                pl.BlockSpec((bm, d), lambda i, k: (i, 0), pipeline_mode=pl.Buffered(1)),
                pl.BlockSpec((1, d), lambda i, k: (0, 0)),
                pl.BlockSpec((1, d), lambda i, k: (0, 0))]
    args = [a, w, h, g_post.reshape(1, d), (g_next if emit_next else g_post).reshape(1, d)]
    if ple:
        in_specs += [pl.BlockSpec((bm, p.shape[1]), lambda i, k: (i, 0)),
                     pl.BlockSpec(wp.shape, lambda i, k: (0, 0))]
        args += [p, wp]
    out_specs = [pl.BlockSpec((bm, d), lambda i, k: (i, 0))]
    out_shape = [jax.ShapeDtypeStruct((m, d), F32)]
    if emit_next:
        out_specs.append(pl.BlockSpec((bm, d), lambda i, k: (i, 0)))
        out_shape.append(jax.ShapeDtypeStruct((m, d), BF16))
    res = pl.pallas_call(
        functools.partial(_proj_kernel, ple=ple, coef=coef, nk=nk, emit_next=emit_next),
        grid=(m // bm, nk),
        in_specs=in_specs,
        out_specs=out_specs,
        out_shape=out_shape,
        compiler_params=_cparams(("parallel", "arbitrary")),
        name="ple_residual" if ple else "proj_residual",
    )(*args)
    return (res[0], res[1]) if emit_next else (res[0], None)


def _rope_tile(acc, cos_ref, sin_ref, half, n_slabs):
    cos = cos_ref[...]
    sin = sin_ref[...]
    outs = []
    for s in range(n_slabs):
        a = acc[:, s * LANES:(s + 1) * LANES]
        if half == LANES // 2:
            partner = pltpu.roll(a, LANES // 2, axis=1)
        else:
            lane = lax.broadcasted_iota(jnp.int32, a.shape, 1)
            fwd = pltpu.roll(a, LANES - half, axis=1)
            bwd = pltpu.roll(a, half, axis=1)
            partner = jnp.where((lane & (2 * half - 1)) < half, fwd, bwd)
        outs.append(a * cos + partner * sin)
    return jnp.concatenate(outs, axis=1)


def _inproj_kernel(x_ref, w_ref, ws_ref, scale_ref, c64_ref, s64_ref, c128_ref, s128_ref,
                   o_ref, os_ref, *, n64, n128):
    j = pl.program_id(1)
    x = x_ref[...]
    acc = jnp.dot(x, w_ref[...], preferred_element_type=F32)
    n_slabs = acc.shape[1] // LANES
    scale = scale_ref[...]

    @pl.when(j == 0)
    def _():
        os_ref[...] = jnp.dot(x, ws_ref[...], preferred_element_type=F32)

    @pl.when(j < n64)
    def _():
        o_ref[...] = (_rope_tile(acc, c64_ref, s64_ref, 32, n_slabs) * scale).astype(o_ref.dtype)

    @pl.when((j >= n64) & (j < n64 + n128))
    def _():
        o_ref[...] = (_rope_tile(acc, c128_ref, s128_ref, 64, n_slabs) * scale).astype(o_ref.dtype)

    @pl.when(j >= n64 + n128)
    def _():
        o_ref[...] = (acc * scale).astype(o_ref.dtype)


def _inproj(xn, w_main, w_small, scale_row, tabs, seq):
    m, d = xn.shape
    n = w_main.shape[1]
    bm = _pick(seq, (1024, 512, 256))
    bn = IN_TN
    tpb = seq // bm
    tab_spec = pl.BlockSpec((bm, LANES), lambda i, j: (i % tpb, 0))
    return pl.pallas_call(
        functools.partial(_inproj_kernel, n64=N_ROPE64_BLK * LANES // bn, n128=N_ROPE128_BLK * LANES // bn),
        grid=(m // bm, n // bn),
        in_specs=[pl.BlockSpec((bm, d), lambda i, j: (i, 0)),
                  pl.BlockSpec((d, bn), lambda i, j: (0, j)),
                  pl.BlockSpec((d, LANES), lambda i, j: (0, 0)),
                  pl.BlockSpec((1, bn), lambda i, j: (0, j)),
                  tab_spec, tab_spec, tab_spec, tab_spec],
        out_specs=[pl.BlockSpec((bm, bn), lambda i, j: (i, j)),
                   pl.BlockSpec((bm, LANES), lambda i, j: (i, 0))],
        out_shape=[jax.ShapeDtypeStruct((m, n), BF16), jax.ShapeDtypeStruct((m, LANES), F32)],
        compiler_params=_cparams(("parallel", "arbitrary")),
        name="in_proj_rope",
    )(xn, w_main, w_small, scale_row, *tabs)


def _attend(q, k_ref, v_ref, c_lo, c_mask, c_hi, t_rows, mask_fn, bias_fn=None):
    n0, nm, n1 = c_lo * ATT_TK, c_mask * ATT_TK, c_hi * ATT_TK
    s = lax.dot_general(q, k_ref[n0:n1, :], (((1,), (1,)), ((), ())), preferred_element_type=F32)
    if bias_fn is not None:
        s = s + bias_fn(n0, n1)
    kpos = nm + lax.broadcasted_iota(jnp.int32, (1, n1 - nm), 1)
    tail = jnp.where(mask_fn(t_rows, kpos, nm, n1), s[:, nm - n0:], NEG_INF)
    s = jnp.concatenate([s[:, :nm - n0], tail], axis=1) if nm > n0 else tail
    m = jnp.max(s, axis=1, keepdims=True)
    p = jnp.exp(s - m)
    l = jnp.sum(p, axis=1, keepdims=True)
    acc = jnp.dot(p.astype(BF16), v_ref[n0:n1, :], preferred_element_type=F32)
    return m, l, acc


def _causal_mask(t_rows, kpos, n0, n1):
    return kpos <= t_rows


def _row_positions(q0, tq, reps):
    r = lax.broadcasted_iota(jnp.int32, (reps * tq, 1), 0)
    return q0 + (r & (tq - 1))


def _per_query_tile(qi, n_tiles, fn):
    for c in range(n_tiles):
        pl.when(qi == c)(functools.partial(fn, c))


def _first_chunk(c, reach):
    return max(c - (reach + ATT_TK - 1) // ATT_TK, 0)


def _diff_kernel(lam_ref, q_ref, k_ref, v_ref, g_ref, o_ref, *, out_scale):
    tq = q_ref.shape[0]

    def tile(c):
        q = q_ref[...]
        lane = lax.broadcasted_iota(jnp.int32, q.shape, 1)
        zero = jnp.zeros_like(q)
        q2 = jnp.concatenate([jnp.where(lane < HEAD_DIM // 2, q, zero),
                              jnp.where(lane >= HEAD_DIM // 2, q, zero)], axis=0)
        t_rows = _row_positions(c * tq, tq, 2)
        _, l, acc = _attend(q2, k_ref, v_ref, 0, c, c + 1, t_rows, _causal_mask)
        o = acc / l
        a = o[:tq] - lam_ref[0] * o[tq:]
        y = a * lax.rsqrt(jnp.mean(a * a, axis=-1, keepdims=True) + EPS) * g_ref[...]
        o_ref[...] = (y * out_scale).astype(o_ref.dtype)

    _per_query_tile(pl.program_id(2), k_ref.shape[0] // tq, tile)


def _diff_attention(z3, lam, subln_g, lam_init):
    b, t, _ = z3.shape
    tq = ATT_TQ
    return pl.pallas_call(
        functools.partial(_diff_kernel, out_scale=1.0 - lam_init),
        grid=(b, N_HEADS_DIFF, t // tq),
        in_specs=[pl.BlockSpec(memory_space=pltpu.SMEM),
                  pl.BlockSpec((None, tq, LANES), lambda bi, h, qi: (bi, qi, COL_QA + h)),
                  pl.BlockSpec((None, t, LANES), lambda bi, h, qi: (bi, 0, COL_KA + h)),
                  pl.BlockSpec((None, t, LANES), lambda bi, h, qi: (bi, 0, COL_VA + h)),
                  pl.BlockSpec((1, LANES), lambda bi, h, qi: (0, 0))],
        out_specs=pl.BlockSpec((None, tq, LANES), lambda bi, h, qi: (bi, qi, h)),
        out_shape=jax.ShapeDtypeStruct((b, t, N_HEADS_DIFF * HEAD_DIM), BF16),
        compiler_params=_cparams(("parallel", "parallel", "arbitrary")),
        name="diff_attention",
    )(lam.reshape(1), z3, z3, z3, subln_g.reshape(1, LANES))


def _small_prep_kernel(zs_ref, bf_ref, c_ref, ct_ref, g_ref):
    zs = zs_ref[...]
    t = zs.shape[0]
    c = jax.nn.log_sigmoid(zs + bf_ref[...])
    row = lax.broadcasted_iota(jnp.int32, c.shape, 0)
    shift = 1
    while shift < t:
        c = c + jnp.where(row >= shift, pltpu.roll(c, shift, axis=0), 0.0)
        shift *= 2
    c_ref[...] = c
    ct_ref[...] = jnp.transpose(c)[:8, :]
    g_ref[...] = jax.nn.sigmoid(zs)


def _small_prep(zs3, bf_row):
    b, t, _ = zs3.shape
    return pl.pallas_call(
        _small_prep_kernel,
        grid=(b,),
        in_specs=[pl.BlockSpec((None, t, LANES), lambda bi: (bi, 0, 0)),
                  pl.BlockSpec((1, LANES), lambda bi: (0, 0))],
        out_specs=[pl.BlockSpec((None, t, LANES), lambda bi: (bi, 0, 0)),
                   pl.BlockSpec((None, 8, t), lambda bi: (bi, 0, 0)),
                   pl.BlockSpec((None, t, LANES), lambda bi: (bi, 0, 0))],
        out_shape=[jax.ShapeDtypeStruct((b, t, LANES), F32),
                   jax.ShapeDtypeStruct((b, 8, t), F32),
                   jax.ShapeDtypeStruct((b, t, LANES), F32)],
        compiler_params=_cparams(("parallel",)),
        name="forget_cumsum_gates",
    )(zs3, bf_row)


def _fox_kernel(q_ref, k_ref, v_ref, c_ref, ct_ref, o_ref):
    h = pl.program_id(1)
    tq = q_ref.shape[0]

    def tile(c):
        cc = c_ref[...]
        lane = lax.broadcasted_iota(jnp.int32, cc.shape, 1)
        cq = jnp.sum(jnp.where(lane == h, cc, 0.0), axis=1, keepdims=True)

        def bias_fn(n0, n1):
            return cq - ct_ref[pl.ds(h, 1), n0:n1]

        t_rows = _row_positions(c * tq, tq, 1)
        _, l, acc = _attend(q_ref[...], k_ref, v_ref, 0, c, c + 1, t_rows, _causal_mask, bias_fn)
        o_ref[...] = (acc / l).astype(o_ref.dtype)

    _per_query_tile(pl.program_id(2), k_ref.shape[0] // tq, tile)


def _fox_attention(z3, c_col, c_row):
    b, t, _ = z3.shape
    tq = ATT_TQ
    return pl.pallas_call(
        _fox_kernel,
        grid=(b, N_HEADS_FOX, t // tq),
        in_specs=[pl.BlockSpec((None, tq, LANES), lambda bi, h, qi: (bi, qi, COL_QB + h)),
                  pl.BlockSpec((None, t, LANES), lambda bi, h, qi: (bi, 0, COL_KB + h)),
                  pl.BlockSpec((None, t, LANES), lambda bi, h, qi: (bi, 0, COL_VB + h)),
                  pl.BlockSpec((None, tq, LANES), lambda bi, h, qi: (bi, qi, 0)),
                  pl.BlockSpec((None, 8, t), lambda bi, h, qi: (bi, 0, 0))],
        out_specs=pl.BlockSpec((None, tq, LANES), lambda bi, h, qi: (bi, qi, h)),
        out_shape=jax.ShapeDtypeStruct((b, t, N_HEADS_FOX * HEAD_DIM), BF16),
        compiler_params=_cparams(("parallel", "parallel", "arbitrary")),
        name="forgetting_attention",
    )(z3, z3, z3, c_col, c_row)


def _compress_one(x_ref, xs_ref, pe_ref, w1_ref, w2_ref, o_ref):
    t = x_ref.shape[0]
    nb = t // CMP_STRIDE
    xs_ref[...] = x_ref[...].astype(F32)
    pe = pe_ref[...]
    lo, hi = [], []
    for i in range(CMP_STRIDE):
        xi = xs_ref[pl.ds(i, nb, stride=CMP_STRIDE), :]
        lo.append((xi + pe[i:i + 1, :]).astype(BF16))
        hi.append((xi + pe[CMP_STRIDE + i:CMP_STRIDE + i + 1, :]).astype(BF16))
    half = CMP_STRIDE * HEAD_DIM
    a = jnp.dot(jnp.concatenate(lo, axis=1), w1_ref[:half, :], preferred_element_type=F32)
    bb = jnp.dot(jnp.concatenate(hi, axis=1), w1_ref[half:, :], preferred_element_type=F32)
    y = a + pltpu.roll(bb, nb - 1, axis=0)
    out = jnp.dot(jax.nn.gelu(y).astype(BF16), w2_ref[...], preferred_element_type=F32)
    row = lax.broadcasted_iota(jnp.int32, out.shape, 0)
    o_ref[...] = jnp.where(row < nb - 1, out, 0.0).astype(o_ref.dtype)


def _compress_kernel(k_ref, v_ref, pek_ref, pev_ref, wk1_ref, wk2_ref, wv1_ref, wv2_ref,
                     kc_ref, vc_ref, xs_ref):
    _compress_one(k_ref, xs_ref, pek_ref, wk1_ref, wk2_ref, kc_ref)
    _compress_one(v_ref, xs_ref, pev_ref, wv1_ref, wv2_ref, vc_ref)


def _nsa_compress(z3, pe_k, pe_v, wk1, wk2, wv1, wv2):
    b, t, _ = z3.shape
    nb = t // CMP_STRIDE
    full = lambda a: pl.BlockSpec(a.shape, lambda bi, g: (0,) * a.ndim)
    return pl.pallas_call(
        _compress_kernel,
        grid=(b, N_KV_NSA),
        in_specs=[pl.BlockSpec((None, t, LANES), lambda bi, g: (bi, 0, COL_KCC + g)),
                  pl.BlockSpec((None, t, LANES), lambda bi, g: (bi, 0, COL_VCC + g)),
                  full(pe_k), full(pe_v), full(wk1), full(wk2), full(wv1), full(wv2)],
        out_specs=[pl.BlockSpec((None, None, nb, LANES), lambda bi, g: (bi, g, 0, 0)),
                   pl.BlockSpec((None, None, nb, LANES), lambda bi, g: (bi, g, 0, 0))],
        out_shape=[jax.ShapeDtypeStruct((b, N_KV_NSA, nb, LANES), BF16),
                   jax.ShapeDtypeStruct((b, N_KV_NSA, nb, LANES), BF16)],
        scratch_shapes=[pltpu.VMEM((t, LANES), F32)],
        compiler_params=_cparams(("parallel", "parallel")),
        name="nsa_compress",
    )(z3, z3, pe_k, pe_v, wk1, wk2, wv1, wv2)


def _nsa_tile(c, g, q_ref, kc_ref, vc_ref, ks_ref, vs_ref, kw_ref, vw_ref, gate_ref, o_ref, seq):
    tq = q_ref.shape[0]
    reps = N_HEADS_NSA // N_KV_NSA
    q0 = c * tq
    q4 = jnp.concatenate([q_ref[:, r * LANES:(r + 1) * LANES] for r in range(reps)], axis=0)
    t_rows = _row_positions(q0, tq, reps)

    nb = kc_ref.shape[0]
    n_cmp = (seq - CMP_LEN) // CMP_STRIDE + 1
    s = lax.dot_general(q4, kc_ref[...], (((1,), (1,)), ((), ())), preferred_element_type=F32)
    cidx = lax.broadcasted_iota(jnp.int32, (1, nb), 1)
    valid = (cidx * CMP_STRIDE + (CMP_LEN - 1) <= t_rows) & (cidx < n_cmp)
    s = jnp.where(valid, s, NEG_INF)
    e = jnp.where(valid, jnp.exp(s - jnp.max(s, axis=1, keepdims=True)), 0.0)
    den = jnp.sum(e, axis=1, keepdims=True)
    p_cmp = e / jnp.where(den > 0, den, 1.0)
    o_cmp = jnp.dot(p_cmp.astype(BF16), vc_ref[...], preferred_element_type=F32)

    n_sel = seq // SEL_LEN
    n_sel_pad = LANES
    p_sum = p_cmp[:tq]
    for r in range(1, reps):
        p_sum = p_sum + p_cmp[r * tq:(r + 1) * tq]
    p_hi = p_sum.astype(BF16)
    p_lo = (p_sum - p_hi.astype(F32)).astype(BF16)
    srow = lax.broadcasted_iota(jnp.int32, (n_sel_pad, nb), 0)
    ccol = lax.broadcasted_iota(jnp.int32, (n_sel_pad, nb), 1)
    overlap = ((ccol * CMP_STRIDE < srow * SEL_LEN + SEL_LEN) &
               (ccol * CMP_STRIDE + CMP_LEN > srow * SEL_LEN) & (ccol < n_cmp) & (srow < n_sel))
    ov = jnp.where(overlap, 1.0, 0.0).astype(BF16)
    nt = (((1,), (1,)), ((), ()))
    imp_t = (lax.dot_general(ov, p_hi, nt, preferred_element_type=F32) +
             lax.dot_general(ov, p_lo, nt, preferred_element_type=F32))
    n_rank = ((n_sel + 7) // 8) * 8
    imp_t = imp_t[:n_rank]
    sblk = lax.broadcasted_iota(jnp.int32, (n_rank, tq), 0)
    tpos = q0 + lax.broadcasted_iota(jnp.int32, (n_rank, tq), 1)
    cur = tpos // SEL_LEN
    forced = (sblk == 0) | (sblk == cur) | (sblk == cur - 1)
    val = jnp.where(forced, POS_BIG, jnp.where(sblk * SEL_LEN <= tpos, imp_t, NEG_INF))
    val = jnp.where(sblk < n_sel, val, -2.0 * POS_BIG)
    rank = jnp.zeros((n_rank, tq), F32)
    for s2 in range(n_sel):
        other = val[s2:s2 + 1, :]
        tie = jnp.where(sblk > s2, 1.0, 0.0)
        rank = rank + jnp.where(other > val, 1.0, jnp.where(other == val, tie, 0.0))
    n_top = min(SEL_TOPN, n_sel)
    sel_t = jnp.where((rank < n_top) & (sblk < n_sel), 1.0, 0.0)
    if n_rank < n_sel_pad:
        sel_t = jnp.concatenate([sel_t, jnp.zeros((n_sel_pad - n_rank, tq), F32)], axis=0)
    sel = jnp.transpose(sel_t).astype(BF16)

    def sel_mask(t_r, kpos, n0, n1):
        erow = lax.broadcasted_iota(jnp.int32, (n_sel_pad, n1 - n0), 0)
        ecol = n0 + lax.broadcasted_iota(jnp.int32, (n_sel_pad, n1 - n0), 1)
        expand = jnp.where(ecol // SEL_LEN == erow, 1.0, 0.0).astype(BF16)
        chosen = jnp.dot(sel, expand, preferred_element_type=F32)
        chosen = jnp.concatenate([chosen] * reps, axis=0)
        return (chosen > 0.5) & (kpos <= t_r)

    _, l_s, acc_s = _attend(q4, ks_ref, vs_ref, 0, 0, c + 1, t_rows, sel_mask)
    o_sel = acc_s / l_s

    def win_mask(t_r, kpos, n0, n1):
        return (kpos <= t_r) & (kpos > t_r - WIN_LEN)

    lo = _first_chunk(c, WIN_LEN)
    _, l_w, acc_w = _attend(q4, kw_ref, vw_ref, lo, lo, c + 1, t_rows, win_mask)
    o_win = acc_w / l_w

    gates = gate_ref[...]
    lane = lax.broadcasted_iota(jnp.int32, gates.shape, 1)
    n_fb = N_HEADS_FOX
    outs = []
    for r in range(reps):
        col = n_fb + 3 * (g * reps + r)
        rs = slice(r * tq, (r + 1) * tq)
        o_r = jnp.zeros((tq, LANES), F32)
        for br, o_br in enumerate((o_cmp, o_sel, o_win)):
            gcol = jnp.sum(jnp.where(lane == col + br, gates, 0.0), axis=1, keepdims=True)
            o_r = o_r + gcol * o_br[rs]
        outs.append(o_r)
    o_ref[...] = jnp.concatenate(outs, axis=1).astype(o_ref.dtype)


def _nsa_kernel(*refs, seq):
    tq = refs[0].shape[0]
    g = pl.program_id(1)
    _per_query_tile(pl.program_id(2), seq // tq, lambda c: _nsa_tile(c, g, *refs, seq))


def _nsa_attention(z3, kc, vc, gates):
    b, t, _ = z3.shape
    tq = ATT_TQ
    reps = N_HEADS_NSA // N_KV_NSA
    nb = kc.shape[2]
    kv = lambda col: pl.BlockSpec((None, t, LANES), lambda bi, g, qi: (bi, 0, col + g))
    return pl.pallas_call(
        functools.partial(_nsa_kernel, seq=t),
        grid=(b, N_KV_NSA, t // tq),
        in_specs=[pl.BlockSpec((None, tq, reps * LANES), lambda bi, g, qi: (bi, qi, COL_QC // reps + g)),
                  pl.BlockSpec((None, None, nb, LANES), lambda bi, g, qi: (bi, g, 0, 0)),
                  pl.BlockSpec((None, None, nb, LANES), lambda bi, g, qi: (bi, g, 0, 0)),
                  kv(COL_KSC), kv(COL_VSC), kv(COL_KWC), kv(COL_VWC),
                  pl.BlockSpec((None, tq, LANES), lambda bi, g, qi: (bi, qi, 0))],
        out_specs=pl.BlockSpec((None, tq, reps * LANES), lambda bi, g, qi: (bi, qi, g)),
        out_shape=jax.ShapeDtypeStruct((b, t, N_HEADS_NSA * HEAD_DIM), BF16),
        compiler_params=_cparams(("parallel", "parallel", "arbitrary")),
        name="nsa_attention",
    )(z3, kc, vc, z3, z3, z3, z3, gates)


def _dil_kernel(*refs):
    n_g = len(DIL_PATTERNS)
    q_refs, k_refs, v_refs = refs[:n_g], refs[n_g:2 * n_g], refs[2 * n_g:3 * n_g]
    o_ref = refs[3 * n_g]
    tq = o_ref.shape[0]

    def tile(c):
        t_rows = _row_positions(c * tq, tq, 1)
        parts = []
        for gi, (w, r) in enumerate(DIL_PATTERNS):
            def mask(t_r, kpos, n0, n1, w=w, r=r):
                dist = t_r - kpos
                return (dist >= 0) & (dist <= w) & ((dist & (r - 1)) == 0)

            lo = _first_chunk(c, w)
            parts.append(_attend(q_refs[gi][...], k_refs[gi], v_refs[gi], lo, lo, c + 1, t_rows, mask))
        lses = [m + jnp.log(l) for (m, l, _) in parts]
        top = functools.reduce(jnp.maximum, lses)
        ws = [jnp.exp(x - top) for x in lses]
        tot = functools.reduce(lambda a, b2: a + b2, ws)
        out = jnp.zeros((tq, LANES), F32)
        for wgt, (m, l, acc) in zip(ws, parts):
            out = out + (wgt / tot) * (acc / l)
        o_ref[...] = out.astype(o_ref.dtype)

    _per_query_tile(pl.program_id(2), k_refs[0].shape[0] // tq, tile)


def _dilated_attention(z3):
    b, t, _ = z3.shape
    tq = ATT_TQ
    n_g = len(DIL_PATTERNS)
    hg = N_HEADS_DIL // n_g
    qs = [pl.BlockSpec((None, tq, LANES), lambda bi, j, qi, gi=gi: (bi, qi, COL_QD + gi * hg + j)) for gi in range(n_g)]
    ks = [pl.BlockSpec((None, t, LANES), lambda bi, j, qi, gi=gi: (bi, 0, COL_KD + gi * hg + j)) for gi in range(n_g)]
    vs = [pl.BlockSpec((None, t, LANES), lambda bi, j, qi, gi=gi: (bi, 0, COL_VD + gi * hg + j)) for gi in range(n_g)]
    return pl.pallas_call(
        _dil_kernel,
        grid=(b, hg, t // tq),
        in_specs=qs + ks + vs,
        out_specs=pl.BlockSpec((None, tq, LANES), lambda bi, j, qi: (bi, qi, j)),
        out_shape=jax.ShapeDtypeStruct((b, t, hg * HEAD_DIM), BF16),
        compiler_params=_cparams(("parallel", "parallel", "arbitrary")),
        name="dilated_attention",
    )(*([z3] * (3 * n_g)))


def _in_proj_sizes():
    hd = HEAD_DIM
    kv = N_KV_NSA * hd
    return (N_HEADS_DIFF * hd, N_HEADS_DIFF * hd, N_HEADS_DIFF * hd,
            N_HEADS_FOX * hd, N_HEADS_FOX * hd, N_HEADS_FOX * hd, N_HEADS_FOX,
            N_HEADS_NSA * hd, kv, kv, kv, kv, kv, kv, 3 * N_HEADS_NSA,
            N_HEADS_DIL * hd, N_HEADS_DIL * hd, N_HEADS_DIL * hd)


def _pack_w_in(w):
    offs = np.concatenate([[0], np.cumsum(_in_proj_sizes())])
    names = ("qa", "ka", "va", "qb", "kb", "vb", "fb", "qc", "kcc", "vcc", "ksc", "vsc", "kwc", "vwc", "gc",
             "qd", "kd", "vd")
    sec = {n: w[:, int(offs[i]):int(offs[i + 1])] for i, n in enumerate(names)}
    order = ("qa", "ka", "qc", "kcc", "ksc", "kwc", "qd", "kd", "va", "qb", "kb", "vb", "vcc", "vsc", "vwc", "vd")
    main = jnp.concatenate([sec[n] for n in order], axis=1).astype(BF16)
    n_small = sec["fb"].shape[1] + sec["gc"].shape[1]
    small = jnp.concatenate([sec["fb"], sec["gc"], jnp.zeros((w.shape[0], LANES - n_small), w.dtype)], axis=1)
    scales = {"qa": (HEAD_DIM // 2) ** -0.5, "qb": HEAD_DIM ** -0.5, "qc": HEAD_DIM ** -0.5, "qd": HEAD_DIM ** -0.5}
    scale_row = jnp.concatenate([jnp.full((1, sec[n].shape[1]), scales.get(n, 1.0), F32) for n in order], axis=1)
    return main, small.astype(BF16), scale_row


def _rope_tables(t):
    pos = jnp.arange(t, dtype=F32)[:, None]
    tabs = []
    for d in (HEAD_DIM // 2, HEAD_DIM):
        half = d // 2
        inv = ROPE_THETA ** (-jnp.arange(half, dtype=F32) * 2.0 / d)
        ang = pos * inv[None, :]
        cos, sin = jnp.cos(ang), jnp.sin(ang)
        reps = LANES // d
        tabs.append(jnp.tile(jnp.concatenate([cos, cos], axis=1), (1, reps)))
        tabs.append(jnp.tile(jnp.concatenate([-sin, sin], axis=1), (1, reps)))
    return tabs


def _pad_cols(w, mult):
    pad = (-w.shape[1]) % mult
    return jnp.pad(w, ((0, 0), (0, pad))) if pad else w


def _pad_rows(w, mult):
    pad = (-w.shape[0]) % mult
    return jnp.pad(w, ((0, pad), (0, 0))) if pad else w


def kernel(x, p, ffn1_pre_g, ffn1_w_gate, ffn1_w_up, ffn1_w_down, ffn1_post_g, mix_pre_g, w_in, fox_bf, diff_lam_q1, diff_lam_k1, diff_lam_q2, diff_lam_k2, diff_subln_g, nsa_pe_k, nsa_pe_v, nsa_wk1, nsa_wk2, nsa_wv1, nsa_wv2, w_out, mix_post_g, ffn2_pre_g, ffn2_w_gate, ffn2_w_up, ffn2_w_down, ffn2_post_g, ple_pre_g, ple_w_gate, ple_w_proj, ple_post_g):
    b, t, d = x.shape
    depth = w_in.shape[0]
    m = b * t
    tabs = _rope_tables(t)
    h = x.reshape(m, d)
    xn = _rmsnorm(h, ffn1_pre_g[0])

    def ffn(h, xn, wg, wu, wd, g_post, g_next):
        act = _gateup(xn, _pad_cols(wg, FF_PAD).astype(BF16), _pad_cols(wu, FF_PAD).astype(BF16))
        return _proj_residual(act, _pad_rows(wd, FF_PAD).astype(BF16), h, g_post, g_next, 0.5)

    for li in range(depth):
        h, xn = ffn(h, xn, ffn1_w_gate[li], ffn1_w_up[li], ffn1_w_down[li], ffn1_post_g[li], mix_pre_g[li])

        w_main, w_small, scale_row = _pack_w_in(w_in[li])
        z, zs = _inproj(xn, w_main, w_small, scale_row, tabs, t)
        z3 = z.reshape(b, t, z.shape[1])
        bf_row = jnp.pad(fox_bf[li].astype(F32), (0, LANES - N_HEADS_FOX)).reshape(1, LANES)
        c_col, c_row, gates = _small_prep(zs.reshape(b, t, LANES), bf_row)

        lam_init = 0.8 - 0.6 * math.exp(-0.3 * li)
        lam = (jnp.exp(jnp.sum(diff_lam_q1[li].astype(F32) * diff_lam_k1[li].astype(F32)))
               - jnp.exp(jnp.sum(diff_lam_q2[li].astype(F32) * diff_lam_k2[li].astype(F32))) + lam_init)
        o_a = _diff_attention(z3, lam, diff_subln_g[li].astype(F32), lam_init)
        o_b = _fox_attention(z3, c_col, c_row)
        kc, vc = _nsa_compress(z3, nsa_pe_k[li], nsa_pe_v[li], nsa_wk1[li].astype(BF16), nsa_wk2[li].astype(BF16),
                               nsa_wv1[li].astype(BF16), nsa_wv2[li].astype(BF16))
        o_c = _nsa_attention(z3, kc, vc, gates)
        o_d = _dilated_attention(z3)
        o = jnp.concatenate([o_a, o_b, o_c, o_d], axis=-1).reshape(m, -1)
        h, xn = _proj_residual(_pad_cols(o, FF_PAD), _pad_rows(w_out[li], FF_PAD).astype(BF16), h,
                               mix_post_g[li], ffn2_pre_g[li], 1.0)

        h, xn = ffn(h, xn, ffn2_w_gate[li], ffn2_w_up[li], ffn2_w_down[li], ffn2_post_g[li], ple_pre_g[li])

        g_next = ffn1_pre_g[li + 1] if li + 1 < depth else None
        h, xn = _proj_residual(xn, ple_w_gate[li].astype(BF16), h, ple_post_g[li], g_next, 1.0,
                               p=p[li].reshape(m, -1), wp=ple_w_proj[li].astype(BF16))
    return h.reshape(b, t, d)
```

```python
import functools
import math

import numpy as np
import jax
import jax.numpy as jnp
from jax import lax
from jax.experimental import pallas as pl
from jax.experimental.pallas import tpu as pltpu

F32 = jnp.float32
BF16 = jnp.bfloat16

HEAD_DIM = 128
N_HEADS_DIFF = 8
N_HEADS_FOX = 7
N_HEADS_NSA = 8
N_KV_NSA = 2
N_HEADS_DIL = 9
ROPE_THETA = 10000.0
EPS = 1e-6
CMP_LEN = 32
CMP_STRIDE = 16
SEL_LEN = 64
SEL_TOPN = 16
WIN_LEN = 512
DIL_PATTERNS = ((128, 1), (512, 4), (2048, 16))
NEG_INF = -1e30
POS_BIG = 1e30

V7X_VMEM_LIMIT_BYTES = 56 * 1024 * 1024
LANES = 128
FF_PAD = 512
ATT_TQ = 256
ATT_TK = 256

COL_QA, COL_KA = 0, 8
COL_QC, COL_KCC, COL_KSC, COL_KWC, COL_QD, COL_KD = 16, 24, 26, 28, 30, 39
COL_VA, COL_QB, COL_KB, COL_VB, COL_VCC, COL_VSC, COL_VWC, COL_VD = 48, 56, 63, 70, 77, 79, 81, 83
N_COLBLK = 92
N_ROPE64_BLK = 16
N_ROPE128_BLK = 32
IN_TN = 512


def _cparams(sem):
    return pltpu.CompilerParams(dimension_semantics=sem, vmem_limit_bytes=V7X_VMEM_LIMIT_BYTES)


def _pick(n, prefs):
    for p in prefs:
        if n % p == 0:
            return p
    return n


def _rmsnorm_kernel(x_ref, g_ref, o_ref):
    x = x_ref[...]
    ms = jnp.mean(x * x, axis=-1, keepdims=True)
    o_ref[...] = (x * lax.rsqrt(ms + EPS) * g_ref[...]).astype(o_ref.dtype)


def _rmsnorm(x, g):
    m, d = x.shape
    bm = _pick(m, (256, 128, 64, 32, 16, 8))
    return pl.pallas_call(
        _rmsnorm_kernel,
        grid=(m // bm,),
        in_specs=[pl.BlockSpec((bm, d), lambda i: (i, 0)), pl.BlockSpec((1, d), lambda i: (0, 0))],
        out_specs=pl.BlockSpec((bm, d), lambda i: (i, 0)),
        out_shape=jax.ShapeDtypeStruct((m, d), BF16),
        compiler_params=_cparams(("parallel",)),
        name="rmsnorm",
    )(x, g.reshape(1, d))


def _gateup_kernel(x_ref, wg_ref, wu_ref, o_ref):
    x = x_ref[...]
    g = jnp.dot(x, wg_ref[...], preferred_element_type=F32)
    u = jnp.dot(x, wu_ref[...], preferred_element_type=F32)
    o_ref[...] = (g * jax.nn.sigmoid(g) * u).astype(o_ref.dtype)


def _gateup(xn, wg, wu):
    m, d = xn.shape
    f = wg.shape[1]
    bm = _pick(m, (1024, 512, 256, 128))
    bf = _pick(f, (512, 256, 128))
    return pl.pallas_call(
        _gateup_kernel,
        grid=(m // bm, f // bf),
        in_specs=[pl.BlockSpec((bm, d), lambda i, j: (i, 0)),
                  pl.BlockSpec((d, bf), lambda i, j: (0, j)),
                  pl.BlockSpec((d, bf), lambda i, j: (0, j))],
        out_specs=pl.BlockSpec((bm, bf), lambda i, j: (i, j)),
        out_shape=jax.ShapeDtypeStruct((m, f), BF16),
        compiler_params=_cparams(("parallel", "arbitrary")),
        name="ffn_gateup",
    )(xn, wg, wu)


EPI_ROWS = 32


def _residual_epilogue(o_ref, h_ref, gpost_ref, gnext_ref, xn_ref):
    bm = o_ref.shape[0]
    rows = min(EPI_ROWS, bm)

    def body(r, carry):
        sl = pl.ds(pl.multiple_of(r * rows, rows), rows)
        f = o_ref[sl, :]
        hn = h_ref[sl, :] + f * lax.rsqrt(jnp.mean(f * f, axis=-1, keepdims=True) + EPS) * gpost_ref[...]
        o_ref[sl, :] = hn
        if xn_ref is not None:
            ms = jnp.mean(hn * hn, axis=-1, keepdims=True)
            xn_ref[sl, :] = (hn * lax.rsqrt(ms + EPS) * gnext_ref[...]).astype(xn_ref.dtype)
        return carry

    lax.fori_loop(0, bm // rows, body, 0)


def _proj_k_kernel(a_ref, w_ref, h_ref, gpost_ref, gnext_ref, o_ref, *rest, nk):
    k = pl.program_id(1)

    @pl.when(k == 0)
    def _():
        o_ref[...] = jnp.dot(a_ref[...], w_ref[...], preferred_element_type=F32)

    @pl.when(k > 0)
    def _():
        o_ref[...] += jnp.dot(a_ref[...], w_ref[...], preferred_element_type=F32)

    @pl.when(k == nk - 1)
    def _():
        _residual_epilogue(o_ref, h_ref, gpost_ref, gnext_ref, rest[0] if rest else None)


def _proj_n_kernel(*refs, ple, nj, emit_next):
    a_ref, w_ref, h_ref, gpost_ref, gnext_ref = refs[:5]
    refs = refs[5:]
    if ple:
        p_ref, wp_ref = refs[:2]
        refs = refs[2:]
    o_ref = refs[0]
    xn_ref = refs[1] if emit_next else None
    j = pl.program_id(1)
    bn = w_ref.shape[1]
    f = jnp.dot(a_ref[...], w_ref[...], preferred_element_type=F32)
    if ple:
        f = jax.nn.sigmoid(f) * jnp.dot(p_ref[...].astype(BF16), wp_ref[...], preferred_element_type=F32)
    o_ref[:, pl.ds(pl.multiple_of(j * bn, bn), bn)] = f

    @pl.when(j == nj - 1)
    def _():
        _residual_epilogue(o_ref, h_ref, gpost_ref, gnext_ref, xn_ref)


PROJ_FULL_K_MAX = 4096


def _proj_residual(a, w, h, g_post, g_next, coef, p=None, wp=None):
    m, kdim = a.shape
    d = w.shape[1]
    ple = p is not None
    emit_next = g_next is not None
    bm = _pick(m, (512, 256, 128))
    row_specs = [pl.BlockSpec((bm, d), lambda i, k: (i, 0), pipeline_mode=pl.Buffered(1)),
                 pl.BlockSpec((1, d), lambda i, k: (0, 0)),
                 pl.BlockSpec((1, d), lambda i, k: (0, 0))]
    row_args = [h, (coef * g_post).reshape(1, d), (g_next if emit_next else g_post).reshape(1, d)]
    out_specs = [pl.BlockSpec((bm, d), lambda i, k: (i, 0))]
    out_shape = [jax.ShapeDtypeStruct((m, d), F32)]
    if emit_next:
        out_specs.append(pl.BlockSpec((bm, d), lambda i, k: (i, 0)))
        out_shape.append(jax.ShapeDtypeStruct((m, d), BF16))
    if kdim <= PROJ_FULL_K_MAX:
        bn = _pick(d, (512, 256, 128))
        nj = d // bn
        in_specs = [pl.BlockSpec((bm, kdim), lambda i, j: (i, 0)),
                    pl.BlockSpec((kdim, bn), lambda i, j: (0, j))] + row_specs
        args = [a, w] + row_args
        if ple:
            in_specs += [pl.BlockSpec((bm, p.shape[1]), lambda i, j: (i, 0)),
                         pl.BlockSpec((wp.shape[0], bn), lambda i, j: (0, j))]
            args += [p, wp]
        body = functools.partial(_proj_n_kernel, ple=ple, nj=nj, emit_next=emit_next)
        grid = (m // bm, nj)
    else:
        assert not ple
        bk = _pick(kdim, (1024, 512, 256, 128))
        nk = kdim // bk
        in_specs = [pl.BlockSpec((bm, bk), lambda i, k: (i, k)),
                    pl.BlockSpec((bk, d), lambda i, k: (k, 0))] + row_specs
        args = [a, w] + row_args
        body = functools.partial(_proj_k_kernel, nk=nk)
        grid = (m // bm, nk)
    res = pl.pallas_call(
        body,
        grid=grid,
        in_specs=in_specs,
        out_specs=out_specs,
        out_shape=out_shape,
        compiler_params=_cparams(("parallel", "arbitrary")),
        name="ple_residual" if ple else "proj_residual",
    )(*args)
    return (res[0], res[1]) if emit_next else (res[0], None)


def _cast_pad_kernel(x_ref, o_ref, *, rows, cols, masked):
    x = x_ref[...]
    if masked:
        br, bc = x.shape
        r = pl.program_id(0) * br + lax.broadcasted_iota(jnp.int32, x.shape, 0)
        c = pl.program_id(1) * bc + lax.broadcasted_iota(jnp.int32, x.shape, 1)
        x = jnp.where((r < rows) & (c < cols), x, 0.0)
    o_ref[...] = x.astype(o_ref.dtype)


def _cast_pad(w_stack, li, row_mult=1, col_mult=1):
    _, rows, cols = w_stack.shape
    rows_p = -(-rows // row_mult) * row_mult
    cols_p = -(-cols // col_mult) * col_mult
    br = _pick(rows_p, (512, 256, 128))
    bc = _pick(cols_p, (1024, 512, 256, 128))
    last_i, last_j = (rows - 1) // br, (cols - 1) // bc
    return pl.pallas_call(
        functools.partial(_cast_pad_kernel, rows=rows, cols=cols, masked=(rows_p, cols_p) != (rows, cols)),
        grid=(rows_p // br, cols_p // bc),
        in_specs=[pl.BlockSpec((None, br, bc), lambda i, j: (li, jnp.minimum(i, last_i), jnp.minimum(j, last_j)))],
        out_specs=pl.BlockSpec((br, bc), lambda i, j: (i, j)),
        out_shape=jax.ShapeDtypeStruct((rows_p, cols_p), BF16),
        compiler_params=_cparams(("parallel", "parallel")),
        name="weight_cast_pad",
    )(w_stack)


def _rope_tile(acc, cos_ref, sin_ref, half, n_slabs):
    cos = cos_ref[...]
    sin = sin_ref[...]
    outs = []
    for s in range(n_slabs):
        a = acc[:, s * LANES:(s + 1) * LANES]
        if half == LANES // 2:
            partner = pltpu.roll(a, LANES // 2, axis=1)
        else:
            lane = lax.broadcasted_iota(jnp.int32, a.shape, 1)
            fwd = pltpu.roll(a, LANES - half, axis=1)
            bwd = pltpu.roll(a, half, axis=1)
            partner = jnp.where((lane & (2 * half - 1)) < half, fwd, bwd)
        outs.append(a * cos + partner * sin)
    return jnp.concatenate(outs, axis=1)


def _inproj_kernel(x_ref, w_ref, ws_ref, scale_ref, c64_ref, s64_ref, c128_ref, s128_ref,
                   o_ref, os_ref, *, n64, n128):
    j = pl.program_id(1)
    n_slabs = w_ref.shape[1] // LANES

    def product():
        return jnp.dot(x_ref[...], w_ref[...], preferred_element_type=F32)

    @pl.when(j == 0)
    def _():
        os_ref[...] = jnp.dot(x_ref[...], ws_ref[...], preferred_element_type=F32)

    @pl.when(j < n64)
    def _():
        o_ref[...] = (_rope_tile(product(), c64_ref, s64_ref, 32, n_slabs) * scale_ref[...]).astype(o_ref.dtype)

    @pl.when((j >= n64) & (j < n64 + n128))
    def _():
        o_ref[...] = (_rope_tile(product(), c128_ref, s128_ref, 64, n_slabs) * scale_ref[...]).astype(o_ref.dtype)

    @pl.when(j >= n64 + n128)
    def _():
        o_ref[...] = (product() * scale_ref[...]).astype(o_ref.dtype)


def _inproj(xn, w_main, w_small, scale_row, tabs, seq):
    m, d = xn.shape
    n = w_main.shape[1]
    bm = _pick(seq, (1024, 512, 256))
    bn = IN_TN
    tpb = seq // bm
    tab_spec = pl.BlockSpec((bm, LANES), lambda i, j: (i % tpb, 0))
    return pl.pallas_call(
        functools.partial(_inproj_kernel, n64=N_ROPE64_BLK * LANES // bn, n128=N_ROPE128_BLK * LANES // bn),
        grid=(m // bm, n // bn),
        in_specs=[pl.BlockSpec((bm, d), lambda i, j: (i, 0)),
                  pl.BlockSpec((d, bn), lambda i, j: (0, j)),
                  pl.BlockSpec((d, LANES), lambda i, j: (0, 0)),
                  pl.BlockSpec((1, bn), lambda i, j: (0, j)),
                  tab_spec, tab_spec, tab_spec, tab_spec],
        out_specs=[pl.BlockSpec((bm, bn), lambda i, j: (i, j)),
                   pl.BlockSpec((bm, LANES), lambda i, j: (i, 0))],
        out_shape=[jax.ShapeDtypeStruct((m, n), BF16), jax.ShapeDtypeStruct((m, LANES), F32)],
        compiler_params=_cparams(("parallel", "arbitrary")),
        name="in_proj_rope",
    )(xn, w_main, w_small, scale_row, *tabs)


def _attend(q, k_ref, v_ref, c_lo, c_mask, c_hi, t_rows, mask_fn, bias_fn=None):
    n0, nm, n1 = c_lo * ATT_TK, c_mask * ATT_TK, c_hi * ATT_TK
    s = lax.dot_general(q, k_ref[n0:n1, :], (((1,), (1,)), ((), ())), preferred_element_type=F32)
    if bias_fn is not None:
        s = s + bias_fn(n0, n1)
    kpos = nm + lax.broadcasted_iota(jnp.int32, (1, n1 - nm), 1)
    tail = jnp.where(mask_fn(t_rows, kpos, nm, n1), s[:, nm - n0:], NEG_INF)
    s = jnp.concatenate([s[:, :nm - n0], tail], axis=1) if nm > n0 else tail
    m = jnp.max(s, axis=1, keepdims=True)
    p = jnp.exp(s - m)
    l = jnp.sum(p, axis=1, keepdims=True)
    acc = jnp.dot(p.astype(BF16), v_ref[n0:n1, :], preferred_element_type=F32)
    return m, l, acc


def _causal_mask(t_rows, kpos, n0, n1):
    return kpos <= t_rows


def _row_positions(q0, tq, reps):
    r = lax.broadcasted_iota(jnp.int32, (reps * tq, 1), 0)
    return q0 + (r & (tq - 1))


def _per_query_tile(qi, n_tiles, fn):
    for c in range(n_tiles):
        pl.when(qi == c)(functools.partial(fn, c))


def _first_chunk(c, reach):
    return max(c - (reach + ATT_TK - 1) // ATT_TK, 0)


def _diff_kernel(lam_ref, q_ref, k_ref, v_ref, g_ref, o_ref, *, out_scale):
    tq = q_ref.shape[0]

    def tile(c):
        q = q_ref[...]
        lane = lax.broadcasted_iota(jnp.int32, q.shape, 1)
        zero = jnp.zeros_like(q)
        q2 = jnp.concatenate([jnp.where(lane < HEAD_DIM // 2, q, zero),
                              jnp.where(lane >= HEAD_DIM // 2, q, zero)], axis=0)
        t_rows = _row_positions(c * tq, tq, 2)
        _, l, acc = _attend(q2, k_ref, v_ref, 0, c, c + 1, t_rows, _causal_mask)
        o = acc / l
        a = o[:tq] - lam_ref[0] * o[tq:]
        y = a * lax.rsqrt(jnp.mean(a * a, axis=-1, keepdims=True) + EPS) * g_ref[...]
        o_ref[...] = (y * out_scale).astype(o_ref.dtype)

    _per_query_tile(pl.program_id(2), k_ref.shape[0] // tq, tile)


def _diff_attention(z3, lam, subln_g, lam_init):
    b, t, _ = z3.shape
    tq = ATT_TQ
    return pl.pallas_call(
        functools.partial(_diff_kernel, out_scale=1.0 - lam_init),
        grid=(b, N_HEADS_DIFF, t // tq),
        in_specs=[pl.BlockSpec(memory_space=pltpu.SMEM),
                  pl.BlockSpec((None, tq, LANES), lambda bi, h, qi: (bi, qi, COL_QA + h)),
                  pl.BlockSpec((None, t, LANES), lambda bi, h, qi: (bi, 0, COL_KA + h)),
                  pl.BlockSpec((None, t, LANES), lambda bi, h, qi: (bi, 0, COL_VA + h)),
                  pl.BlockSpec((1, LANES), lambda bi, h, qi: (0, 0))],
        out_specs=pl.BlockSpec((None, tq, LANES), lambda bi, h, qi: (bi, qi, h)),
        out_shape=jax.ShapeDtypeStruct((b, t, N_HEADS_DIFF * HEAD_DIM), BF16),
        compiler_params=_cparams(("parallel", "parallel", "arbitrary")),
        name="diff_attention",
    )(lam.reshape(1), z3, z3, z3, subln_g.reshape(1, LANES))


def _small_prep_kernel(zs_ref, bf_ref, c_ref, ct_ref, g_ref):
    zs = zs_ref[...]
    t = zs.shape[0]
    c = jax.nn.log_sigmoid(zs + bf_ref[...])
    row = lax.broadcasted_iota(jnp.int32, c.shape, 0)
    shift = 1
    while shift < t:
        c = c + jnp.where(row >= shift, pltpu.roll(c, shift, axis=0), 0.0)
        shift *= 2
    c_ref[...] = c
    ct_ref[...] = jnp.transpose(c)[:8, :]
    g_ref[...] = jax.nn.sigmoid(zs)


def _small_prep(zs3, bf_row):
    b, t, _ = zs3.shape
    return pl.pallas_call(
        _small_prep_kernel,
        grid=(b,),
        in_specs=[pl.BlockSpec((None, t, LANES), lambda bi: (bi, 0, 0)),
                  pl.BlockSpec((1, LANES), lambda bi: (0, 0))],
        out_specs=[pl.BlockSpec((None, t, LANES), lambda bi: (bi, 0, 0)),
                   pl.BlockSpec((None, 8, t), lambda bi: (bi, 0, 0)),
                   pl.BlockSpec((None, t, LANES), lambda bi: (bi, 0, 0))],
        out_shape=[jax.ShapeDtypeStruct((b, t, LANES), F32),
                   jax.ShapeDtypeStruct((b, 8, t), F32),
                   jax.ShapeDtypeStruct((b, t, LANES), F32)],
        compiler_params=_cparams(("parallel",)),
        name="forget_cumsum_gates",
    )(zs3, bf_row)


def _fox_kernel(q_ref, k_ref, v_ref, c_ref, ct_ref, o_ref):
    h = pl.program_id(1)
    tq = q_ref.shape[0]

    def tile(c):
        cc = c_ref[...]
        lane = lax.broadcasted_iota(jnp.int32, cc.shape, 1)
        cq = jnp.sum(jnp.where(lane == h, cc, 0.0), axis=1, keepdims=True)

        def bias_fn(n0, n1):
            return cq - ct_ref[pl.ds(h, 1), n0:n1]

        t_rows = _row_positions(c * tq, tq, 1)
        _, l, acc = _attend(q_ref[...], k_ref, v_ref, 0, c, c + 1, t_rows, _causal_mask, bias_fn)
        o_ref[...] = (acc / l).astype(o_ref.dtype)

    _per_query_tile(pl.program_id(2), k_ref.shape[0] // tq, tile)


def _fox_attention(z3, c_col, c_row):
    b, t, _ = z3.shape
    tq = ATT_TQ
    return pl.pallas_call(
        _fox_kernel,
        grid=(b, N_HEADS_FOX, t // tq),
        in_specs=[pl.BlockSpec((None, tq, LANES), lambda bi, h, qi: (bi, qi, COL_QB + h)),
                  pl.BlockSpec((None, t, LANES), lambda bi, h, qi: (bi, 0, COL_KB + h)),
                  pl.BlockSpec((None, t, LANES), lambda bi, h, qi: (bi, 0, COL_VB + h)),
                  pl.BlockSpec((None, tq, LANES), lambda bi, h, qi: (bi, qi, 0)),
                  pl.BlockSpec((None, 8, t), lambda bi, h, qi: (bi, 0, 0))],
        out_specs=pl.BlockSpec((None, tq, LANES), lambda bi, h, qi: (bi, qi, h)),
        out_shape=jax.ShapeDtypeStruct((b, t, N_HEADS_FOX * HEAD_DIM), BF16),
        compiler_params=_cparams(("parallel", "parallel", "arbitrary")),
        name="forgetting_attention",
    )(z3, z3, z3, c_col, c_row)


def _compress_one(x_ref, xs_ref, pe_ref, w1_ref, w2_ref, o_ref):
    t = x_ref.shape[0]
    nb = t // CMP_STRIDE
    xs_ref[...] = x_ref[...].astype(F32)
    pe = pe_ref[...]
    lo, hi = [], []
    for i in range(CMP_STRIDE):
        xi = xs_ref[pl.ds(i, nb, stride=CMP_STRIDE), :]
        lo.append((xi + pe[i:i + 1, :]).astype(BF16))
        hi.append((xi + pe[CMP_STRIDE + i:CMP_STRIDE + i + 1, :]).astype(BF16))
    half = CMP_STRIDE * HEAD_DIM
    a = jnp.dot(jnp.concatenate(lo, axis=1), w1_ref[:half, :], preferred_element_type=F32)
    bb = jnp.dot(jnp.concatenate(hi, axis=1), w1_ref[half:, :], preferred_element_type=F32)
    y = a + pltpu.roll(bb, nb - 1, axis=0)
    out = jnp.dot(jax.nn.gelu(y).astype(BF16), w2_ref[...], preferred_element_type=F32)
    row = lax.broadcasted_iota(jnp.int32, out.shape, 0)
    o_ref[...] = jnp.where(row < nb - 1, out, 0.0).astype(o_ref.dtype)


def _compress_kernel(k_ref, v_ref, pek_ref, pev_ref, wk1_ref, wk2_ref, wv1_ref, wv2_ref,
                     kc_ref, vc_ref, xs_ref):
    _compress_one(k_ref, xs_ref, pek_ref, wk1_ref, wk2_ref, kc_ref)
    _compress_one(v_ref, xs_ref, pev_ref, wv1_ref, wv2_ref, vc_ref)


def _nsa_compress(z3, pe_k, pe_v, wk1, wk2, wv1, wv2):
    b, t, _ = z3.shape
    nb = t // CMP_STRIDE
    full = lambda a: pl.BlockSpec(a.shape, lambda bi, g: (0,) * a.ndim)
    return pl.pallas_call(
        _compress_kernel,
        grid=(b, N_KV_NSA),
        in_specs=[pl.BlockSpec((None, t, LANES), lambda bi, g: (bi, 0, COL_KCC + g)),
                  pl.BlockSpec((None, t, LANES), lambda bi, g: (bi, 0, COL_VCC + g)),
                  full(pe_k), full(pe_v), full(wk1), full(wk2), full(wv1), full(wv2)],
        out_specs=[pl.BlockSpec((None, None, nb, LANES), lambda bi, g: (bi, g, 0, 0)),
                   pl.BlockSpec((None, None, nb, LANES), lambda bi, g: (bi, g, 0, 0))],
        out_shape=[jax.ShapeDtypeStruct((b, N_KV_NSA, nb, LANES), BF16),
                   jax.ShapeDtypeStruct((b, N_KV_NSA, nb, LANES), BF16)],
        scratch_shapes=[pltpu.VMEM((t, LANES), F32)],
        compiler_params=_cparams(("parallel", "parallel")),
        name="nsa_compress",
    )(z3, z3, pe_k, pe_v, wk1, wk2, wv1, wv2)


def _nsa_tile(c, g, q_ref, kc_ref, vc_ref, ks_ref, vs_ref, kw_ref, vw_ref, gate_ref, o_ref, seq):
    tq = q_ref.shape[0]
    reps = N_HEADS_NSA // N_KV_NSA
    q0 = c * tq
    q4 = jnp.concatenate([q_ref[:, r * LANES:(r + 1) * LANES] for r in range(reps)], axis=0)
    t_rows = _row_positions(q0, tq, reps)

    nb = kc_ref.shape[0]
    n_cmp = (seq - CMP_LEN) // CMP_STRIDE + 1
    s = lax.dot_general(q4, kc_ref[...], (((1,), (1,)), ((), ())), preferred_element_type=F32)
    cidx = lax.broadcasted_iota(jnp.int32, (1, nb), 1)
    valid = (cidx * CMP_STRIDE + (CMP_LEN - 1) <= t_rows) & (cidx < n_cmp)
    s = jnp.where(valid, s, NEG_INF)
    e = jnp.where(valid, jnp.exp(s - jnp.max(s, axis=1, keepdims=True)), 0.0)
    den = jnp.sum(e, axis=1, keepdims=True)
    p_cmp = e / jnp.where(den > 0, den, 1.0)
    o_cmp = jnp.dot(p_cmp.astype(BF16), vc_ref[...], preferred_element_type=F32)

    n_sel = seq // SEL_LEN
    n_sel_pad = LANES
    p_sum = p_cmp[:tq]
    for r in range(1, reps):
        p_sum = p_sum + p_cmp[r * tq:(r + 1) * tq]
    p_hi = p_sum.astype(BF16)
    p_lo = (p_sum - p_hi.astype(F32)).astype(BF16)
    srow = lax.broadcasted_iota(jnp.int32, (n_sel_pad, nb), 0)
    ccol = lax.broadcasted_iota(jnp.int32, (n_sel_pad, nb), 1)
    overlap = ((ccol * CMP_STRIDE < srow * SEL_LEN + SEL_LEN) &
               (ccol * CMP_STRIDE + CMP_LEN > srow * SEL_LEN) & (ccol < n_cmp) & (srow < n_sel))
    ov = jnp.where(overlap, 1.0, 0.0).astype(BF16)
    nt = (((1,), (1,)), ((), ()))
    imp_t = (lax.dot_general(ov, p_hi, nt, preferred_element_type=F32) +
             lax.dot_general(ov, p_lo, nt, preferred_element_type=F32))
    n_rank = ((n_sel + 7) // 8) * 8
    imp_t = imp_t[:n_rank]
    sblk = lax.broadcasted_iota(jnp.int32, (n_rank, tq), 0)
    tpos = q0 + lax.broadcasted_iota(jnp.int32, (n_rank, tq), 1)
    cur = tpos // SEL_LEN
    forced = (sblk == 0) | (sblk == cur) | (sblk == cur - 1)
    val = jnp.where(forced, POS_BIG, jnp.where(sblk * SEL_LEN <= tpos, imp_t, NEG_INF))
    val = jnp.where(sblk < n_sel, val, -2.0 * POS_BIG)
    rank = jnp.zeros((n_rank, tq), F32)
    for s2 in range(n_sel):
        other = val[s2:s2 + 1, :]
        tie = jnp.where(sblk > s2, 1.0, 0.0)
        rank = rank + jnp.where(other > val, 1.0, jnp.where(other == val, tie, 0.0))
    n_top = min(SEL_TOPN, n_sel)
    sel_t = jnp.where((rank < n_top) & (sblk < n_sel), 1.0, 0.0)
    if n_rank < n_sel_pad:
        sel_t = jnp.concatenate([sel_t, jnp.zeros((n_sel_pad - n_rank, tq), F32)], axis=0)
    sel = jnp.transpose(sel_t).astype(BF16)

    def sel_mask(t_r, kpos, n0, n1):
        erow = lax.broadcasted_iota(jnp.int32, (n_sel_pad, n1 - n0), 0)
        ecol = n0 + lax.broadcasted_iota(jnp.int32, (n_sel_pad, n1 - n0), 1)
        expand = jnp.where(ecol // SEL_LEN == erow, 1.0, 0.0).astype(BF16)
        chosen = jnp.dot(sel, expand, preferred_element_type=F32)
        chosen = jnp.concatenate([chosen] * reps, axis=0)
        return (chosen > 0.5) & (kpos <= t_r)

    _, l_s, acc_s = _attend(q4, ks_ref, vs_ref, 0, 0, c + 1, t_rows, sel_mask)
    o_sel = acc_s / l_s

    def win_mask(t_r, kpos, n0, n1):
        return (kpos <= t_r) & (kpos > t_r - WIN_LEN)

    lo = _first_chunk(c, WIN_LEN)
    _, l_w, acc_w = _attend(q4, kw_ref, vw_ref, lo, lo, c + 1, t_rows, win_mask)
    o_win = acc_w / l_w

    gates = gate_ref[...]
    lane = lax.broadcasted_iota(jnp.int32, gates.shape, 1)
    n_fb = N_HEADS_FOX
    outs = []
    for r in range(reps):
        col = n_fb + 3 * (g * reps + r)
        rs = slice(r * tq, (r + 1) * tq)
        o_r = jnp.zeros((tq, LANES), F32)
        for br, o_br in enumerate((o_cmp, o_sel, o_win)):
            gcol = jnp.sum(jnp.where(lane == col + br, gates, 0.0), axis=1, keepdims=True)
            o_r = o_r + gcol * o_br[rs]
        outs.append(o_r)
    o_ref[...] = jnp.concatenate(outs, axis=1).astype(o_ref.dtype)


def _nsa_kernel(*refs, seq):
    tq = refs[0].shape[0]
    g = pl.program_id(1)
    _per_query_tile(pl.program_id(2), seq // tq, lambda c: _nsa_tile(c, g, *refs, seq))


def _nsa_attention(z3, kc, vc, gates):
    b, t, _ = z3.shape
    tq = ATT_TQ
    reps = N_HEADS_NSA // N_KV_NSA
    nb = kc.shape[2]
    kv = lambda col: pl.BlockSpec((None, t, LANES), lambda bi, g, qi: (bi, 0, col + g))
    return pl.pallas_call(
        functools.partial(_nsa_kernel, seq=t),
        grid=(b, N_KV_NSA, t // tq),
        in_specs=[pl.BlockSpec((None, tq, reps * LANES), lambda bi, g, qi: (bi, qi, COL_QC // reps + g)),
                  pl.BlockSpec((None, None, nb, LANES), lambda bi, g, qi: (bi, g, 0, 0)),
                  pl.BlockSpec((None, None, nb, LANES), lambda bi, g, qi: (bi, g, 0, 0)),
                  kv(COL_KSC), kv(COL_VSC), kv(COL_KWC), kv(COL_VWC),
                  pl.BlockSpec((None, tq, LANES), lambda bi, g, qi: (bi, qi, 0))],
        out_specs=pl.BlockSpec((None, tq, reps * LANES), lambda bi, g, qi: (bi, qi, g)),
        out_shape=jax.ShapeDtypeStruct((b, t, N_HEADS_NSA * HEAD_DIM), BF16),
        compiler_params=_cparams(("parallel", "parallel", "arbitrary")),
        name="nsa_attention",
    )(z3, kc, vc, z3, z3, z3, z3, gates)


def _dil_kernel(*refs):
    n_g = len(DIL_PATTERNS)
    q_refs, k_refs, v_refs = refs[:n_g], refs[n_g:2 * n_g], refs[2 * n_g:3 * n_g]
    o_ref = refs[3 * n_g]
    tq = o_ref.shape[0]

    def tile(c):
        t_rows = _row_positions(c * tq, tq, 1)
        parts = []
        for gi, (w, r) in enumerate(DIL_PATTERNS):
            def mask(t_r, kpos, n0, n1, w=w, r=r):
                dist = t_r - kpos
                return (dist >= 0) & (dist <= w) & ((dist & (r - 1)) == 0)

            lo = _first_chunk(c, w)
            parts.append(_attend(q_refs[gi][...], k_refs[gi], v_refs[gi], lo, lo, c + 1, t_rows, mask))
        lses = [m + jnp.log(l) for (m, l, _) in parts]
        top = functools.reduce(jnp.maximum, lses)
        ws = [jnp.exp(x - top) for x in lses]
        tot = functools.reduce(lambda a, b2: a + b2, ws)
        out = jnp.zeros((tq, LANES), F32)
        for wgt, (m, l, acc) in zip(ws, parts):
            out = out + (wgt / tot) * (acc / l)
        o_ref[...] = out.astype(o_ref.dtype)

    _per_query_tile(pl.program_id(2), k_refs[0].shape[0] // tq, tile)


def _dilated_attention(z3):
    b, t, _ = z3.shape
    tq = ATT_TQ
    n_g = len(DIL_PATTERNS)
    hg = N_HEADS_DIL // n_g
    qs = [pl.BlockSpec((None, tq, LANES), lambda bi, j, qi, gi=gi: (bi, qi, COL_QD + gi * hg + j)) for gi in range(n_g)]
    ks = [pl.BlockSpec((None, t, LANES), lambda bi, j, qi, gi=gi: (bi, 0, COL_KD + gi * hg + j)) for gi in range(n_g)]
    vs = [pl.BlockSpec((None, t, LANES), lambda bi, j, qi, gi=gi: (bi, 0, COL_VD + gi * hg + j)) for gi in range(n_g)]
    return pl.pallas_call(
        _dil_kernel,
        grid=(b, hg, t // tq),
        in_specs=qs + ks + vs,
        out_specs=pl.BlockSpec((None, tq, LANES), lambda bi, j, qi: (bi, qi, j)),
        out_shape=jax.ShapeDtypeStruct((b, t, hg * HEAD_DIM), BF16),
        compiler_params=_cparams(("parallel", "parallel", "arbitrary")),
        name="dilated_attention",
    )(*([z3] * (3 * n_g)))


def _in_proj_sizes():
    hd = HEAD_DIM
    kv = N_KV_NSA * hd
    return (N_HEADS_DIFF * hd, N_HEADS_DIFF * hd, N_HEADS_DIFF * hd,
            N_HEADS_FOX * hd, N_HEADS_FOX * hd, N_HEADS_FOX * hd, N_HEADS_FOX,
            N_HEADS_NSA * hd, kv, kv, kv, kv, kv, kv, 3 * N_HEADS_NSA,
            N_HEADS_DIL * hd, N_HEADS_DIL * hd, N_HEADS_DIL * hd)


def _pack_w_in(w):
    offs = np.concatenate([[0], np.cumsum(_in_proj_sizes())])
    names = ("qa", "ka", "va", "qb", "kb", "vb", "fb", "qc", "kcc", "vcc", "ksc", "vsc", "kwc", "vwc", "gc",
             "qd", "kd", "vd")
    sec = {n: w[:, int(offs[i]):int(offs[i + 1])] for i, n in enumerate(names)}
    order = ("qa", "ka", "qc", "kcc", "ksc", "kwc", "qd", "kd", "va", "qb", "kb", "vb", "vcc", "vsc", "vwc", "vd")
    main = jnp.concatenate([sec[n] for n in order], axis=1).astype(BF16)
    n_small = sec["fb"].shape[1] + sec["gc"].shape[1]
    small = jnp.concatenate([sec["fb"], sec["gc"], jnp.zeros((w.shape[0], LANES - n_small), w.dtype)], axis=1)
    scales = {"qa": (HEAD_DIM // 2) ** -0.5, "qb": HEAD_DIM ** -0.5, "qc": HEAD_DIM ** -0.5, "qd": HEAD_DIM ** -0.5}
    scale_row = jnp.concatenate([jnp.full((1, sec[n].shape[1]), scales.get(n, 1.0), F32) for n in order], axis=1)
    return main, small.astype(BF16), scale_row


def _rope_tables(t):
    pos = jnp.arange(t, dtype=F32)[:, None]
    tabs = []
    for d in (HEAD_DIM // 2, HEAD_DIM):
        half = d // 2
        inv = ROPE_THETA ** (-jnp.arange(half, dtype=F32) * 2.0 / d)
        ang = pos * inv[None, :]
        cos, sin = jnp.cos(ang), jnp.sin(ang)
        reps = LANES // d
        tabs.append(jnp.tile(jnp.concatenate([cos, cos], axis=1), (1, reps)))
        tabs.append(jnp.tile(jnp.concatenate([-sin, sin], axis=1), (1, reps)))
    return tabs


def kernel(x, p, ffn1_pre_g, ffn1_w_gate, ffn1_w_up, ffn1_w_down, ffn1_post_g, mix_pre_g, w_in, fox_bf, diff_lam_q1, diff_lam_k1, diff_lam_q2, diff_lam_k2, diff_subln_g, nsa_pe_k, nsa_pe_v, nsa_wk1, nsa_wk2, nsa_wv1, nsa_wv2, w_out, mix_post_g, ffn2_pre_g, ffn2_w_gate, ffn2_w_up, ffn2_w_down, ffn2_post_g, ple_pre_g, ple_w_gate, ple_w_proj, ple_post_g):
    b, t, d = x.shape
    depth = w_in.shape[0]
    m = b * t
    tabs = _rope_tables(t)
    h = x.reshape(m, d)
    xn = _rmsnorm(h, ffn1_pre_g[0])

    def ffn(h, xn, li, wg, wu, wd, g_post, g_next):
        act = _gateup(xn, _cast_pad(wg, li, col_mult=FF_PAD), _cast_pad(wu, li, col_mult=FF_PAD))
        return _proj_residual(act, _cast_pad(wd, li, row_mult=FF_PAD), h, g_post, g_next, 0.5)

    for li in range(depth):
        h, xn = ffn(h, xn, li, ffn1_w_gate, ffn1_w_up, ffn1_w_down, ffn1_post_g[li], mix_pre_g[li])

        w_main, w_small, scale_row = _pack_w_in(w_in[li])
        z, zs = _inproj(xn, w_main, w_small, scale_row, tabs, t)
        z3 = z.reshape(b, t, z.shape[1])
        bf_row = jnp.pad(fox_bf[li].astype(F32), (0, LANES - N_HEADS_FOX)).reshape(1, LANES)
        c_col, c_row, gates = _small_prep(zs.reshape(b, t, LANES), bf_row)

        lam_init = 0.8 - 0.6 * math.exp(-0.3 * li)
        lam = (jnp.exp(jnp.sum(diff_lam_q1[li].astype(F32) * diff_lam_k1[li].astype(F32)))
               - jnp.exp(jnp.sum(diff_lam_q2[li].astype(F32) * diff_lam_k2[li].astype(F32))) + lam_init)
        o_a = _diff_attention(z3, lam, diff_subln_g[li].astype(F32), lam_init)
        o_b = _fox_attention(z3, c_col, c_row)
        kc, vc = _nsa_compress(z3, nsa_pe_k[li], nsa_pe_v[li], nsa_wk1[li].astype(BF16), nsa_wk2[li].astype(BF16),
                               nsa_wv1[li].astype(BF16), nsa_wv2[li].astype(BF16))
        o_c = _nsa_attention(z3, kc, vc, gates)
        o_d = _dilated_attention(z3)
        o = jnp.concatenate([o_a, o_b, o_c, o_d], axis=-1).reshape(m, -1)
        h, xn = _proj_residual(o, _cast_pad(w_out, li), h, mix_post_g[li], ffn2_pre_g[li], 1.0)

        h, xn = ffn(h, xn, li, ffn2_w_gate, ffn2_w_up, ffn2_w_down, ffn2_post_g[li], ple_pre_g[li])

        g_next = ffn1_pre_g[li + 1] if li + 1 < depth else None
        h, xn = _proj_residual(xn, _cast_pad(ple_w_gate, li), h, ple_post_g[li], g_next, 1.0,
                               p=p[li].reshape(m, -1), wp=ple_w_proj[li].astype(BF16))
    return h.reshape(b, t, d)
```

```python
import functools
import math

import numpy as np
import jax
import jax.numpy as jnp
from jax import lax
from jax.experimental import pallas as pl
from jax.experimental.pallas import tpu as pltpu

F32 = jnp.float32
BF16 = jnp.bfloat16

HEAD_DIM = 128
N_HEADS_DIFF = 8
N_HEADS_FOX = 7
N_HEADS_NSA = 8
N_KV_NSA = 2
N_HEADS_DIL = 9
ROPE_THETA = 10000.0
EPS = 1e-6
CMP_LEN = 32
CMP_STRIDE = 16
SEL_LEN = 64
SEL_TOPN = 16
WIN_LEN = 512
DIL_PATTERNS = ((128, 1), (512, 4), (2048, 16))
NEG_INF = -1e30
POS_BIG = 1e30

V7X_VMEM_LIMIT_BYTES = 60 * 1024 * 1024
LANES = 128
FF_PAD = 512
ATT_TQ = 256
ATT_TK = 256

COL_QA, COL_KA = 0, 8
COL_QC, COL_KCC, COL_KSC, COL_KWC, COL_QD, COL_KD = 16, 24, 26, 28, 30, 39
COL_VA, COL_QB, COL_KB, COL_VB, COL_VCC, COL_VSC, COL_VWC, COL_VD = 48, 56, 63, 70, 77, 79, 81, 83
N_COLBLK = 92
N_ROPE64_BLK = 16
N_ROPE128_BLK = 32
IN_TN = 512


def _cparams(sem):
    return pltpu.CompilerParams(dimension_semantics=sem, vmem_limit_bytes=V7X_VMEM_LIMIT_BYTES)


def _pick(n, prefs):
    for p in prefs:
        if n % p == 0:
            return p
    return n


def _rmsnorm_kernel(x_ref, g_ref, o_ref):
    x = x_ref[...]
    ms = jnp.mean(x * x, axis=-1, keepdims=True)
    o_ref[...] = (x * lax.rsqrt(ms + EPS) * g_ref[...]).astype(o_ref.dtype)


def _rmsnorm(x, g):
    m, d = x.shape
    bm = _pick(m, (256, 128, 64, 32, 16, 8))
    return pl.pallas_call(
        _rmsnorm_kernel,
        grid=(m // bm,),
        in_specs=[pl.BlockSpec((bm, d), lambda i: (i, 0)), pl.BlockSpec((1, d), lambda i: (0, 0))],
        out_specs=pl.BlockSpec((bm, d), lambda i: (i, 0)),
        out_shape=jax.ShapeDtypeStruct((m, d), BF16),
        compiler_params=_cparams(("parallel",)),
        name="rmsnorm",
    )(x, g.reshape(1, d))


def _gateup_kernel(x_ref, wg_ref, wu_ref, o_ref):
    x = x_ref[...]
    g = jnp.dot(x, wg_ref[...], preferred_element_type=F32)
    u = jnp.dot(x, wu_ref[...], preferred_element_type=F32)
    o_ref[...] = (g * jax.nn.sigmoid(g) * u).astype(o_ref.dtype)


def _gateup(xn, wg, wu):
    m, d = xn.shape
    f = wg.shape[1]
    bm = _pick(m, (1024, 512, 256, 128))
    bf = _pick(f, (512, 256, 128))
    return pl.pallas_call(
        _gateup_kernel,
        grid=(m // bm, f // bf),
        in_specs=[pl.BlockSpec((bm, d), lambda i, j: (i, 0)),
                  pl.BlockSpec((d, bf), lambda i, j: (0, j)),
                  pl.BlockSpec((d, bf), lambda i, j: (0, j))],
        out_specs=pl.BlockSpec((bm, bf), lambda i, j: (i, j)),
        out_shape=jax.ShapeDtypeStruct((m, f), BF16),
        compiler_params=_cparams(("parallel", "arbitrary")),
        name="ffn_gateup",
    )(xn, wg, wu)


EPI_ROWS = 32


def _residual_epilogue(o_ref, h_ref, gpost_ref, gnext_ref, xn_ref):
    bm = o_ref.shape[0]
    rows = min(EPI_ROWS, bm)

    def body(r, carry):
        sl = pl.ds(pl.multiple_of(r * rows, rows), rows)
        f = o_ref[sl, :]
        hn = h_ref[sl, :] + f * lax.rsqrt(jnp.mean(f * f, axis=-1, keepdims=True) + EPS) * gpost_ref[...]
        o_ref[sl, :] = hn
        if xn_ref is not None:
            ms = jnp.mean(hn * hn, axis=-1, keepdims=True)
            xn_ref[sl, :] = (hn * lax.rsqrt(ms + EPS) * gnext_ref[...]).astype(xn_ref.dtype)
        return carry

    lax.fori_loop(0, bm // rows, body, 0, unroll=2)


def _proj_k_kernel(a_ref, w_ref, h_ref, gpost_ref, gnext_ref, o_ref, *rest, nk):
    k = pl.program_id(1)

    @pl.when(k == 0)
    def _():
        o_ref[...] = jnp.dot(a_ref[...], w_ref[...], preferred_element_type=F32)

    @pl.when(k > 0)
    def _():
        o_ref[...] += jnp.dot(a_ref[...], w_ref[...], preferred_element_type=F32)

    @pl.when(k == nk - 1)
    def _():
        _residual_epilogue(o_ref, h_ref, gpost_ref, gnext_ref, rest[0] if rest else None)


def _proj_n_kernel(*refs, ple, nj, emit_next):
    a_ref, w_ref, h_ref, gpost_ref, gnext_ref = refs[:5]
    refs = refs[5:]
    if ple:
        p_ref, wp_ref = refs[:2]
        refs = refs[2:]
    o_ref = refs[0]
    xn_ref = refs[1] if emit_next else None
    j = pl.program_id(1)
    bn = w_ref.shape[1]
    f = jnp.dot(a_ref[...], w_ref[...], preferred_element_type=F32)
    if ple:
        f = jax.nn.sigmoid(f) * jnp.dot(p_ref[...].astype(BF16), wp_ref[...], preferred_element_type=F32)
    o_ref[:, pl.ds(pl.multiple_of(j * bn, bn), bn)] = f

    @pl.when(j == nj - 1)
    def _():
        _residual_epilogue(o_ref, h_ref, gpost_ref, gnext_ref, xn_ref)


PROJ_FULL_K_MAX = 4096


def _proj_residual(a, w, h, g_post, g_next, coef, p=None, wp=None):
    m, kdim = a.shape
    d = w.shape[1]
    ple = p is not None
    emit_next = g_next is not None
    bm = _pick(m, (512, 256, 128))
    row_specs = [pl.BlockSpec((bm, d), lambda i, k: (i, 0)),
                 pl.BlockSpec((1, d), lambda i, k: (0, 0)),
                 pl.BlockSpec((1, d), lambda i, k: (0, 0))]
    row_args = [h, (coef * g_post).reshape(1, d), (g_next if emit_next else g_post).reshape(1, d)]
    out_specs = [pl.BlockSpec((bm, d), lambda i, k: (i, 0))]
    out_shape = [jax.ShapeDtypeStruct((m, d), F32)]
    if emit_next:
        out_specs.append(pl.BlockSpec((bm, d), lambda i, k: (i, 0), pipeline_mode=pl.Buffered(1)))
        out_shape.append(jax.ShapeDtypeStruct((m, d), BF16))
    if kdim <= PROJ_FULL_K_MAX:
        bn = _pick(d, (512, 256, 128))
        nj = d // bn
        in_specs = [pl.BlockSpec((bm, kdim), lambda i, j: (i, 0)),
                    pl.BlockSpec((kdim, bn), lambda i, j: (0, j))] + row_specs
        args = [a, w] + row_args
        if ple:
            in_specs += [pl.BlockSpec((bm, p.shape[1]), lambda i, j: (i, 0)),
                         pl.BlockSpec((wp.shape[0], bn), lambda i, j: (0, j))]
            args += [p, wp]
        body = functools.partial(_proj_n_kernel, ple=ple, nj=nj, emit_next=emit_next)
        grid = (m // bm, nj)
    else:
        assert not ple
        bk = _pick(kdim, (1024, 512, 256, 128))
        nk = kdim // bk
        in_specs = [pl.BlockSpec((bm, bk), lambda i, k: (i, k)),
                    pl.BlockSpec((bk, d), lambda i, k: (k, 0))] + row_specs
        args = [a, w] + row_args
        body = functools.partial(_proj_k_kernel, nk=nk)
        grid = (m // bm, nk)
    res = pl.pallas_call(
        body,
        grid=grid,
        in_specs=in_specs,
        out_specs=out_specs,
        out_shape=out_shape,
        compiler_params=_cparams(("parallel", "arbitrary")),
        name="ple_residual" if ple else "proj_residual",
    )(*args)
    return (res[0], res[1]) if emit_next else (res[0], None)


def _cast_pad_kernel(x_ref, o_ref, *, rows, cols, masked):
    x = x_ref[...]
    if masked:
        br, bc = x.shape
        r = pl.program_id(0) * br + lax.broadcasted_iota(jnp.int32, x.shape, 0)
        c = pl.program_id(1) * bc + lax.broadcasted_iota(jnp.int32, x.shape, 1)
        x = jnp.where((r < rows) & (c < cols), x, 0.0)
    o_ref[...] = x.astype(o_ref.dtype)


def _cast_pad(w_stack, li, row_mult=1, col_mult=1):
    _, rows, cols = w_stack.shape
    rows_p = -(-rows // row_mult) * row_mult
    cols_p = -(-cols // col_mult) * col_mult
    br = _pick(rows_p, (512, 256, 128))
    bc = _pick(cols_p, (1024, 512, 256, 128))
    last_i, last_j = (rows - 1) // br, (cols - 1) // bc
    return pl.pallas_call(
        functools.partial(_cast_pad_kernel, rows=rows, cols=cols, masked=(rows_p, cols_p) != (rows, cols)),
        grid=(rows_p // br, cols_p // bc),
        in_specs=[pl.BlockSpec((None, br, bc), lambda i, j: (li, jnp.minimum(i, last_i), jnp.minimum(j, last_j)))],
        out_specs=pl.BlockSpec((br, bc), lambda i, j: (i, j)),
        out_shape=jax.ShapeDtypeStruct((rows_p, cols_p), BF16),
        compiler_params=_cparams(("parallel", "parallel")),
        name="weight_cast_pad",
    )(w_stack)


def _rope_tile(acc, cos_ref, sin_ref, half, n_slabs):
    cos = cos_ref[...]
    sin = sin_ref[...]
    outs = []
    for s in range(n_slabs):
        a = acc[:, s * LANES:(s + 1) * LANES]
        if half == LANES // 2:
            partner = pltpu.roll(a, LANES // 2, axis=1)
        else:
            lane = lax.broadcasted_iota(jnp.int32, a.shape, 1)
            fwd = pltpu.roll(a, LANES - half, axis=1)
            bwd = pltpu.roll(a, half, axis=1)
            partner = jnp.where((lane & (2 * half - 1)) < half, fwd, bwd)
        outs.append(a * cos + partner * sin)
    return jnp.concatenate(outs, axis=1)


def _inproj_kernel(x_ref, w_ref, ws_ref, scale_ref, c64_ref, s64_ref, c128_ref, s128_ref,
                   o_ref, os_ref, *, n64, n128):
    j = pl.program_id(1)
    n_slabs = w_ref.shape[1] // LANES

    def product():
        return jnp.dot(x_ref[...], w_ref[...], preferred_element_type=F32)

    @pl.when(j == 0)
    def _():
        os_ref[...] = jnp.dot(x_ref[...], ws_ref[...], preferred_element_type=F32)

    @pl.when(j < n64)
    def _():
        o_ref[...] = (_rope_tile(product(), c64_ref, s64_ref, 32, n_slabs) * scale_ref[...]).astype(o_ref.dtype)

    @pl.when((j >= n64) & (j < n64 + n128))
    def _():
        o_ref[...] = (_rope_tile(product(), c128_ref, s128_ref, 64, n_slabs) * scale_ref[...]).astype(o_ref.dtype)

    @pl.when(j >= n64 + n128)
    def _():
        o_ref[...] = (product() * scale_ref[...]).astype(o_ref.dtype)


def _inproj(xn, w_main, w_small, scale_row, tabs, seq):
    m, d = xn.shape
    n = w_main.shape[1]
    bm = _pick(seq, (1024, 512, 256))
    bn = IN_TN
    tpb = seq // bm
    tab_spec = pl.BlockSpec((bm, LANES), lambda i, j: (i % tpb, 0))
    return pl.pallas_call(
        functools.partial(_inproj_kernel, n64=N_ROPE64_BLK * LANES // bn, n128=N_ROPE128_BLK * LANES // bn),
        grid=(m // bm, n // bn),
        in_specs=[pl.BlockSpec((bm, d), lambda i, j: (i, 0)),
                  pl.BlockSpec((d, bn), lambda i, j: (0, j)),
                  pl.BlockSpec((d, LANES), lambda i, j: (0, 0)),
                  pl.BlockSpec((1, bn), lambda i, j: (0, j)),
                  tab_spec, tab_spec, tab_spec, tab_spec],
        out_specs=[pl.BlockSpec((bm, bn), lambda i, j: (i, j)),
                   pl.BlockSpec((bm, LANES), lambda i, j: (i, 0))],
        out_shape=[jax.ShapeDtypeStruct((m, n), BF16), jax.ShapeDtypeStruct((m, LANES), F32)],
        compiler_params=_cparams(("parallel", "arbitrary")),
        name="in_proj_rope",
    )(xn, w_main, w_small, scale_row, *tabs)


def _attend(q, k_ref, v_ref, c_lo, c_mask, c_hi, t_rows, mask_fn, bias_fn=None):
    n0, nm, n1 = c_lo * ATT_TK, c_mask * ATT_TK, c_hi * ATT_TK
    s = lax.dot_general(q, k_ref[n0:n1, :], (((1,), (1,)), ((), ())), preferred_element_type=F32)
    if bias_fn is not None:
        s = s + bias_fn(n0, n1)
    kpos = nm + lax.broadcasted_iota(jnp.int32, (1, n1 - nm), 1)
    tail = jnp.where(mask_fn(t_rows, kpos, nm, n1), s[:, nm - n0:], NEG_INF)
    s = jnp.concatenate([s[:, :nm - n0], tail], axis=1) if nm > n0 else tail
    m = jnp.max(s, axis=1, keepdims=True)
    p = jnp.exp(s - m)
    l = jnp.sum(p, axis=1, keepdims=True)
    acc = jnp.dot(p.astype(BF16), v_ref[n0:n1, :], preferred_element_type=F32)
    return m, l, acc


def _causal_mask(t_rows, kpos, n0, n1):
    return kpos <= t_rows


def _row_positions(q0, tq, reps):
    r = lax.broadcasted_iota(jnp.int32, (reps * tq, 1), 0)
    return q0 + (r & (tq - 1))


def _per_query_tile(qi, n_tiles, fn):
    for c in range(n_tiles):
        pl.when(qi == c)(functools.partial(fn, c))


def _first_chunk(c, reach):
    return max(c - (reach + ATT_TK - 1) // ATT_TK, 0)


def _diff_kernel(lam_ref, q_ref, k_ref, v_ref, g_ref, o_ref, *, out_scale):
    tq = q_ref.shape[0]

    def tile(c):
        q = q_ref[...]
        lane = lax.broadcasted_iota(jnp.int32, q.shape, 1)
        zero = jnp.zeros_like(q)
        q2 = jnp.concatenate([jnp.where(lane < HEAD_DIM // 2, q, zero),
                              jnp.where(lane >= HEAD_DIM // 2, q, zero)], axis=0)
        t_rows = _row_positions(c * tq, tq, 2)
        _, l, acc = _attend(q2, k_ref, v_ref, 0, c, c + 1, t_rows, _causal_mask)
        o = acc / l
        a = o[:tq] - lam_ref[0] * o[tq:]
        y = a * lax.rsqrt(jnp.mean(a * a, axis=-1, keepdims=True) + EPS) * g_ref[...]
        o_ref[...] = (y * out_scale).astype(o_ref.dtype)

    _per_query_tile(pl.program_id(2), k_ref.shape[0] // tq, tile)


def _diff_attention(z3, lam, subln_g, lam_init):
    b, t, _ = z3.shape
    tq = ATT_TQ
    return pl.pallas_call(
        functools.partial(_diff_kernel, out_scale=1.0 - lam_init),
        grid=(b, N_HEADS_DIFF, t // tq),
        in_specs=[pl.BlockSpec(memory_space=pltpu.SMEM),
                  pl.BlockSpec((None, tq, LANES), lambda bi, h, qi: (bi, qi, COL_QA + h)),
                  pl.BlockSpec((None, t, LANES), lambda bi, h, qi: (bi, 0, COL_KA + h)),
                  pl.BlockSpec((None, t, LANES), lambda bi, h, qi: (bi, 0, COL_VA + h)),
                  pl.BlockSpec((1, LANES), lambda bi, h, qi: (0, 0))],
        out_specs=pl.BlockSpec((None, tq, LANES), lambda bi, h, qi: (bi, qi, h)),
        out_shape=jax.ShapeDtypeStruct((b, t, N_HEADS_DIFF * HEAD_DIM), BF16),
        compiler_params=_cparams(("parallel", "parallel", "arbitrary")),
        name="diff_attention",
    )(lam.reshape(1), z3, z3, z3, subln_g.reshape(1, LANES))


def _small_prep_kernel(zs_ref, bf_ref, c_ref, ct_ref, g_ref):
    zs = zs_ref[...]
    t = zs.shape[0]
    c = jax.nn.log_sigmoid(zs + bf_ref[...])
    row = lax.broadcasted_iota(jnp.int32, c.shape, 0)
    shift = 1
    while shift < t:
        c = c + jnp.where(row >= shift, pltpu.roll(c, shift, axis=0), 0.0)
        shift *= 2
    c_ref[...] = c
    ct_ref[...] = jnp.transpose(c)[:8, :]
    g_ref[...] = jax.nn.sigmoid(zs)


def _small_prep(zs3, bf_row):
    b, t, _ = zs3.shape
    return pl.pallas_call(
        _small_prep_kernel,
        grid=(b,),
        in_specs=[pl.BlockSpec((None, t, LANES), lambda bi: (bi, 0, 0)),
                  pl.BlockSpec((1, LANES), lambda bi: (0, 0))],
        out_specs=[pl.BlockSpec((None, t, LANES), lambda bi: (bi, 0, 0)),
                   pl.BlockSpec((None, 8, t), lambda bi: (bi, 0, 0)),
                   pl.BlockSpec((None, t, LANES), lambda bi: (bi, 0, 0))],
        out_shape=[jax.ShapeDtypeStruct((b, t, LANES), F32),
                   jax.ShapeDtypeStruct((b, 8, t), F32),
                   jax.ShapeDtypeStruct((b, t, LANES), F32)],
        compiler_params=_cparams(("parallel",)),
        name="forget_cumsum_gates",
    )(zs3, bf_row)


def _fox_kernel(q_ref, k_ref, v_ref, c_ref, ct_ref, o_ref):
    h = pl.program_id(1)
    tq = q_ref.shape[0]

    def tile(c):
        cc = c_ref[...]
        lane = lax.broadcasted_iota(jnp.int32, cc.shape, 1)
        cq = jnp.sum(jnp.where(lane == h, cc, 0.0), axis=1, keepdims=True)

        def bias_fn(n0, n1):
            return cq - ct_ref[pl.ds(h, 1), n0:n1]

        t_rows = _row_positions(c * tq, tq, 1)
        _, l, acc = _attend(q_ref[...], k_ref, v_ref, 0, c, c + 1, t_rows, _causal_mask, bias_fn)
        o_ref[...] = (acc / l).astype(o_ref.dtype)

    _per_query_tile(pl.program_id(2), k_ref.shape[0] // tq, tile)


def _fox_attention(z3, c_col, c_row):
    b, t, _ = z3.shape
    tq = ATT_TQ
    return pl.pallas_call(
        _fox_kernel,
        grid=(b, N_HEADS_FOX, t // tq),
        in_specs=[pl.BlockSpec((None, tq, LANES), lambda bi, h, qi: (bi, qi, COL_QB + h)),
                  pl.BlockSpec((None, t, LANES), lambda bi, h, qi: (bi, 0, COL_KB + h)),
                  pl.BlockSpec((None, t, LANES), lambda bi, h, qi: (bi, 0, COL_VB + h)),
                  pl.BlockSpec((None, tq, LANES), lambda bi, h, qi: (bi, qi, 0)),
                  pl.BlockSpec((None, 8, t), lambda bi, h, qi: (bi, 0, 0))],
        out_specs=pl.BlockSpec((None, tq, LANES), lambda bi, h, qi: (bi, qi, h)),
        out_shape=jax.ShapeDtypeStruct((b, t, N_HEADS_FOX * HEAD_DIM), BF16),
        compiler_params=_cparams(("parallel", "parallel", "arbitrary")),
        name="forgetting_attention",
    )(z3, z3, z3, c_col, c_row)


def _compress_one(x_ref, xs_ref, pe_ref, w1_ref, w2_ref, o_ref):
    t = x_ref.shape[0]
    nb = t // CMP_STRIDE
    xs_ref[...] = x_ref[...].astype(F32)
    pe = pe_ref[...]
    lo, hi = [], []
    for i in range(CMP_STRIDE):
        xi = xs_ref[pl.ds(i, nb, stride=CMP_STRIDE), :]
        lo.append((xi + pe[i:i + 1, :]).astype(BF16))
        hi.append((xi + pe[CMP_STRIDE + i:CMP_STRIDE + i + 1, :]).astype(BF16))
    half = CMP_STRIDE * HEAD_DIM
    a = jnp.dot(jnp.concatenate(lo, axis=1), w1_ref[:half, :], preferred_element_type=F32)
    bb = jnp.dot(jnp.concatenate(hi, axis=1), w1_ref[half:, :], preferred_element_type=F32)
    y = a + pltpu.roll(bb, nb - 1, axis=0)
    out = jnp.dot(jax.nn.gelu(y).astype(BF16), w2_ref[...], preferred_element_type=F32)
    row = lax.broadcasted_iota(jnp.int32, out.shape, 0)
    o_ref[...] = jnp.where(row < nb - 1, out, 0.0).astype(o_ref.dtype)


def _compress_kernel(k_ref, v_ref, pek_ref, pev_ref, wk1_ref, wk2_ref, wv1_ref, wv2_ref,
                     kc_ref, vc_ref, xs_ref):
    _compress_one(k_ref, xs_ref, pek_ref, wk1_ref, wk2_ref, kc_ref)
    _compress_one(v_ref, xs_ref, pev_ref, wv1_ref, wv2_ref, vc_ref)


def _nsa_compress(z3, pe_k, pe_v, wk1, wk2, wv1, wv2):
    b, t, _ = z3.shape
    nb = t // CMP_STRIDE
    full = lambda a: pl.BlockSpec(a.shape, lambda bi, g: (0,) * a.ndim)
    return pl.pallas_call(
        _compress_kernel,
        grid=(b, N_KV_NSA),
        in_specs=[pl.BlockSpec((None, t, LANES), lambda bi, g: (bi, 0, COL_KCC + g)),
                  pl.BlockSpec((None, t, LANES), lambda bi, g: (bi, 0, COL_VCC + g)),
                  full(pe_k), full(pe_v), full(wk1), full(wk2), full(wv1), full(wv2)],
        out_specs=[pl.BlockSpec((None, None, nb, LANES), lambda bi, g: (bi, g, 0, 0)),
                   pl.BlockSpec((None, None, nb, LANES), lambda bi, g: (bi, g, 0, 0))],
        out_shape=[jax.ShapeDtypeStruct((b, N_KV_NSA, nb, LANES), BF16),
                   jax.ShapeDtypeStruct((b, N_KV_NSA, nb, LANES), BF16)],
        scratch_shapes=[pltpu.VMEM((t, LANES), F32)],
        compiler_params=_cparams(("parallel", "parallel")),
        name="nsa_compress",
    )(z3, z3, pe_k, pe_v, wk1, wk2, wv1, wv2)


def _nsa_tile(c, g, q_ref, kc_ref, vc_ref, ks_ref, vs_ref, kw_ref, vw_ref, gate_ref, o_ref, seq):
    tq = q_ref.shape[0]
    reps = N_HEADS_NSA // N_KV_NSA
    q0 = c * tq
    q4 = jnp.concatenate([q_ref[:, r * LANES:(r + 1) * LANES] for r in range(reps)], axis=0)
    t_rows = _row_positions(q0, tq, reps)

    nb = kc_ref.shape[0]
    n_cmp = (seq - CMP_LEN) // CMP_STRIDE + 1
    s = lax.dot_general(q4, kc_ref[...], (((1,), (1,)), ((), ())), preferred_element_type=F32)
    cidx = lax.broadcasted_iota(jnp.int32, (1, nb), 1)
    valid = (cidx * CMP_STRIDE + (CMP_LEN - 1) <= t_rows) & (cidx < n_cmp)
    s = jnp.where(valid, s, NEG_INF)
    e = jnp.where(valid, jnp.exp(s - jnp.max(s, axis=1, keepdims=True)), 0.0)
    den = jnp.sum(e, axis=1, keepdims=True)
    p_cmp = e / jnp.where(den > 0, den, 1.0)
    o_cmp = jnp.dot(p_cmp.astype(BF16), vc_ref[...], preferred_element_type=F32)

    n_sel = seq // SEL_LEN
    n_sel_pad = LANES
    p_sum = p_cmp[:tq]
    for r in range(1, reps):
        p_sum = p_sum + p_cmp[r * tq:(r + 1) * tq]
    p_hi = p_sum.astype(BF16)
    p_lo = (p_sum - p_hi.astype(F32)).astype(BF16)
    srow = lax.broadcasted_iota(jnp.int32, (n_sel_pad, nb), 0)
    ccol = lax.broadcasted_iota(jnp.int32, (n_sel_pad, nb), 1)
    overlap = ((ccol * CMP_STRIDE < srow * SEL_LEN + SEL_LEN) &
               (ccol * CMP_STRIDE + CMP_LEN > srow * SEL_LEN) & (ccol < n_cmp) & (srow < n_sel))
    ov = jnp.where(overlap, 1.0, 0.0).astype(BF16)
    nt = (((1,), (1,)), ((), ()))
    imp_t = (lax.dot_general(ov, p_hi, nt, preferred_element_type=F32) +
             lax.dot_general(ov, p_lo, nt, preferred_element_type=F32))
    n_rank = ((n_sel + 7) // 8) * 8
    imp_t = imp_t[:n_rank]
    sblk = lax.broadcasted_iota(jnp.int32, (n_rank, tq), 0)
    tpos = q0 + lax.broadcasted_iota(jnp.int32, (n_rank, tq), 1)
    cur = tpos // SEL_LEN
    forced = (sblk == 0) | (sblk == cur) | (sblk == cur - 1)
    val = jnp.where(forced, POS_BIG, jnp.where(sblk * SEL_LEN <= tpos, imp_t, NEG_INF))
    val = jnp.where(sblk < n_sel, val, -2.0 * POS_BIG)
    rank = jnp.zeros((n_rank, tq), F32)
    for s2 in range(n_sel):
        other = val[s2:s2 + 1, :]
        tie = jnp.where(sblk > s2, 1.0, 0.0)
        rank = rank + jnp.where(other > val, 1.0, jnp.where(other == val, tie, 0.0))
    n_top = min(SEL_TOPN, n_sel)
    sel_t = jnp.where((rank < n_top) & (sblk < n_sel), 1.0, 0.0)
    if n_rank < n_sel_pad:
        sel_t = jnp.concatenate([sel_t, jnp.zeros((n_sel_pad - n_rank, tq), F32)], axis=0)
    sel = jnp.transpose(sel_t).astype(BF16)

    def sel_mask(t_r, kpos, n0, n1):
        erow = lax.broadcasted_iota(jnp.int32, (n_sel_pad, n1 - n0), 0)
        ecol = n0 + lax.broadcasted_iota(jnp.int32, (n_sel_pad, n1 - n0), 1)
        expand = jnp.where(ecol // SEL_LEN == erow, 1.0, 0.0).astype(BF16)
        chosen = jnp.dot(sel, expand, preferred_element_type=F32)
        chosen = jnp.concatenate([chosen] * reps, axis=0)
        return (chosen > 0.5) & (kpos <= t_r)

    _, l_s, acc_s = _attend(q4, ks_ref, vs_ref, 0, 0, c + 1, t_rows, sel_mask)
    o_sel = acc_s / l_s

    def win_mask(t_r, kpos, n0, n1):
        return (kpos <= t_r) & (kpos > t_r - WIN_LEN)

    lo = _first_chunk(c, WIN_LEN)
    _, l_w, acc_w = _attend(q4, kw_ref, vw_ref, lo, lo, c + 1, t_rows, win_mask)
    o_win = acc_w / l_w

    gates = gate_ref[...]
    lane = lax.broadcasted_iota(jnp.int32, gates.shape, 1)
    n_fb = N_HEADS_FOX
    outs = []
    for r in range(reps):
        col = n_fb + 3 * (g * reps + r)
        rs = slice(r * tq, (r + 1) * tq)
        o_r = jnp.zeros((tq, LANES), F32)
        for br, o_br in enumerate((o_cmp, o_sel, o_win)):
            gcol = jnp.sum(jnp.where(lane == col + br, gates, 0.0), axis=1, keepdims=True)
            o_r = o_r + gcol * o_br[rs]
        outs.append(o_r)
    o_ref[...] = jnp.concatenate(outs, axis=1).astype(o_ref.dtype)


def _nsa_kernel(*refs, seq):
    tq = refs[0].shape[0]
    g = pl.program_id(1)
    _per_query_tile(pl.program_id(2), seq // tq, lambda c: _nsa_tile(c, g, *refs, seq))


def _nsa_attention(z3, kc, vc, gates):
    b, t, _ = z3.shape
    tq = ATT_TQ
    reps = N_HEADS_NSA // N_KV_NSA
    nb = kc.shape[2]
    kv = lambda col: pl.BlockSpec((None, t, LANES), lambda bi, g, qi: (bi, 0, col + g))
    return pl.pallas_call(
        functools.partial(_nsa_kernel, seq=t),
        grid=(b, N_KV_NSA, t // tq),
        in_specs=[pl.BlockSpec((None, tq, reps * LANES), lambda bi, g, qi: (bi, qi, COL_QC // reps + g)),
                  pl.BlockSpec((None, None, nb, LANES), lambda bi, g, qi: (bi, g, 0, 0)),
                  pl.BlockSpec((None, None, nb, LANES), lambda bi, g, qi: (bi, g, 0, 0)),
                  kv(COL_KSC), kv(COL_VSC), kv(COL_KWC), kv(COL_VWC),
                  pl.BlockSpec((None, tq, LANES), lambda bi, g, qi: (bi, qi, 0))],
        out_specs=pl.BlockSpec((None, tq, reps * LANES), lambda bi, g, qi: (bi, qi, g)),
        out_shape=jax.ShapeDtypeStruct((b, t, N_HEADS_NSA * HEAD_DIM), BF16),
        compiler_params=_cparams(("parallel", "parallel", "arbitrary")),
        name="nsa_attention",
    )(z3, kc, vc, z3, z3, z3, z3, gates)


def _dil_kernel(*refs):
    n_g = len(DIL_PATTERNS)
    q_refs, k_refs, v_refs = refs[:n_g], refs[n_g:2 * n_g], refs[2 * n_g:3 * n_g]
    o_ref = refs[3 * n_g]
    tq = o_ref.shape[0]

    def tile(c):
        t_rows = _row_positions(c * tq, tq, 1)
        parts = []
        for gi, (w, r) in enumerate(DIL_PATTERNS):
            def mask(t_r, kpos, n0, n1, w=w, r=r):
                dist = t_r - kpos
                return (dist >= 0) & (dist <= w) & ((dist & (r - 1)) == 0)

            lo = _first_chunk(c, w)
            parts.append(_attend(q_refs[gi][...], k_refs[gi], v_refs[gi], lo, lo, c + 1, t_rows, mask))
        lses = [m + jnp.log(l) for (m, l, _) in parts]
        top = functools.reduce(jnp.maximum, lses)
        ws = [jnp.exp(x - top) for x in lses]
        tot = functools.reduce(lambda a, b2: a + b2, ws)
        out = jnp.zeros((tq, LANES), F32)
        for wgt, (m, l, acc) in zip(ws, parts):
            out = out + (wgt / tot) * (acc / l)
        o_ref[...] = out.astype(o_ref.dtype)

    _per_query_tile(pl.program_id(2), k_refs[0].shape[0] // tq, tile)


def _dilated_attention(z3):
    b, t, _ = z3.shape
    tq = ATT_TQ
    n_g = len(DIL_PATTERNS)
    hg = N_HEADS_DIL // n_g
    qs = [pl.BlockSpec((None, tq, LANES), lambda bi, j, qi, gi=gi: (bi, qi, COL_QD + gi * hg + j)) for gi in range(n_g)]
    ks = [pl.BlockSpec((None, t, LANES), lambda bi, j, qi, gi=gi: (bi, 0, COL_KD + gi * hg + j)) for gi in range(n_g)]
    vs = [pl.BlockSpec((None, t, LANES), lambda bi, j, qi, gi=gi: (bi, 0, COL_VD + gi * hg + j)) for gi in range(n_g)]
    return pl.pallas_call(
        _dil_kernel,
        grid=(b, hg, t // tq),
        in_specs=qs + ks + vs,
        out_specs=pl.BlockSpec((None, tq, LANES), lambda bi, j, qi: (bi, qi, j)),
        out_shape=jax.ShapeDtypeStruct((b, t, hg * HEAD_DIM), BF16),
        compiler_params=_cparams(("parallel", "parallel", "arbitrary")),
        name="dilated_attention",
    )(*([z3] * (3 * n_g)))


def _in_proj_sizes():
    hd = HEAD_DIM
    kv = N_KV_NSA * hd
    return (N_HEADS_DIFF * hd, N_HEADS_DIFF * hd, N_HEADS_DIFF * hd,
            N_HEADS_FOX * hd, N_HEADS_FOX * hd, N_HEADS_FOX * hd, N_HEADS_FOX,
            N_HEADS_NSA * hd, kv, kv, kv, kv, kv, kv, 3 * N_HEADS_NSA,
            N_HEADS_DIL * hd, N_HEADS_DIL * hd, N_HEADS_DIL * hd)


_SEC_NAMES = ("qa", "ka", "va", "qb", "kb", "vb", "fb", "qc", "kcc", "vcc", "ksc", "vsc", "kwc", "vwc", "gc",
              "qd", "kd", "vd")
_SEC_ORDER = ("qa", "ka", "qc", "kcc", "ksc", "kwc", "qd", "kd", "va", "qb", "kb", "vb", "vcc", "vsc", "vwc", "vd")


def _pack_cols_kernel(blk_ref, cls_ref, a_ref, b_ref, o_ref, *, shifts):
    j = pl.program_id(0)
    for ci, s in enumerate(shifts):
        @pl.when(cls_ref[j] == ci)
        def _(s=s):
            a = a_ref[...]
            if s:
                lane = lax.broadcasted_iota(jnp.int32, a.shape, 1)
                a = jnp.where(lane < LANES - s, pltpu.roll(a, LANES - s, axis=1),
                              pltpu.roll(b_ref[...], LANES - s, axis=1))
            o_ref[...] = a.astype(o_ref.dtype)


def _pack_w_in(w_stack, li):
    _, rows, cols = w_stack.shape
    sizes = _in_proj_sizes()
    offs = np.concatenate([[0], np.cumsum(sizes)])
    start = {n: int(offs[i]) for i, n in enumerate(_SEC_NAMES)}
    size = {n: int(sizes[i]) for i, n in enumerate(_SEC_NAMES)}
    src = np.concatenate([start[n] + np.arange(0, size[n], LANES) for n in _SEC_ORDER])
    shifts = tuple(sorted(set(int(s) for s in src % LANES)))
    blk = jnp.asarray(src // LANES, jnp.int32)
    cls = jnp.asarray([shifts.index(int(s)) for s in src % LANES], jnp.int32)
    last = (cols - 1) // LANES
    main = pl.pallas_call(
        functools.partial(_pack_cols_kernel, shifts=shifts),
        grid_spec=pltpu.PrefetchScalarGridSpec(
            num_scalar_prefetch=2,
            grid=(len(src),),
            in_specs=[pl.BlockSpec((None, rows, LANES), lambda j, blk, cls: (li, 0, blk[j])),
                      pl.BlockSpec((None, rows, LANES), lambda j, blk, cls: (li, 0, jnp.minimum(blk[j] + 1, last)))],
            out_specs=pl.BlockSpec((rows, LANES), lambda j, blk, cls: (0, j))),
        out_shape=jax.ShapeDtypeStruct((rows, len(src) * LANES), BF16),
        compiler_params=_cparams(("parallel",)),
        name="w_in_pack",
    )(blk, cls, w_stack, w_stack)
    w = w_stack[li]
    n_small = size["fb"] + size["gc"]
    small = jnp.concatenate([w[:, start["fb"]:start["fb"] + size["fb"]], w[:, start["gc"]:start["gc"] + size["gc"]],
                             jnp.zeros((rows, LANES - n_small), w.dtype)], axis=1)
    scales = {"qa": (HEAD_DIM // 2) ** -0.5, "qb": HEAD_DIM ** -0.5, "qc": HEAD_DIM ** -0.5, "qd": HEAD_DIM ** -0.5}
    scale_row = jnp.concatenate([jnp.full((1, size[n]), scales.get(n, 1.0), F32) for n in _SEC_ORDER], axis=1)
    return main, small.astype(BF16), scale_row


def _rope_tables(t):
    pos = jnp.arange(t, dtype=F32)[:, None]
    tabs = []
    for d in (HEAD_DIM // 2, HEAD_DIM):
        half = d // 2
        inv = ROPE_THETA ** (-jnp.arange(half, dtype=F32) * 2.0 / d)
        ang = pos * inv[None, :]
        cos, sin = jnp.cos(ang), jnp.sin(ang)
        reps = LANES // d
        tabs.append(jnp.tile(jnp.concatenate([cos, cos], axis=1), (1, reps)))
        tabs.append(jnp.tile(jnp.concatenate([-sin, sin], axis=1), (1, reps)))
    return tabs


def kernel(x, p, ffn1_pre_g, ffn1_w_gate, ffn1_w_up, ffn1_w_down, ffn1_post_g, mix_pre_g, w_in, fox_bf, diff_lam_q1, diff_lam_k1, diff_lam_q2, diff_lam_k2, diff_subln_g, nsa_pe_k, nsa_pe_v, nsa_wk1, nsa_wk2, nsa_wv1, nsa_wv2, w_out, mix_post_g, ffn2_pre_g, ffn2_w_gate, ffn2_w_up, ffn2_w_down, ffn2_post_g, ple_pre_g, ple_w_gate, ple_w_proj, ple_post_g):
    b, t, d = x.shape
    depth = w_in.shape[0]
    m = b * t
    tabs = _rope_tables(t)
    h = x.reshape(m, d)
    xn = _rmsnorm(h, ffn1_pre_g[0])

    def ffn(h, xn, li, wg, wu, wd, g_post, g_next):
        act = _gateup(xn, _cast_pad(wg, li, col_mult=FF_PAD), _cast_pad(wu, li, col_mult=FF_PAD))
        return _proj_residual(act, _cast_pad(wd, li, row_mult=FF_PAD), h, g_post, g_next, 0.5)

    for li in range(depth):
        h, xn = ffn(h, xn, li, ffn1_w_gate, ffn1_w_up, ffn1_w_down, ffn1_post_g[li], mix_pre_g[li])

        w_main, w_small, scale_row = _pack_w_in(w_in, li)
        z, zs = _inproj(xn, w_main, w_small, scale_row, tabs, t)
        z3 = z.reshape(b, t, z.shape[1])
        bf_row = jnp.pad(fox_bf[li].astype(F32), (0, LANES - N_HEADS_FOX)).reshape(1, LANES)
        c_col, c_row, gates = _small_prep(zs.reshape(b, t, LANES), bf_row)

        lam_init = 0.8 - 0.6 * math.exp(-0.3 * li)
        lam = (jnp.exp(jnp.sum(diff_lam_q1[li].astype(F32) * diff_lam_k1[li].astype(F32)))
               - jnp.exp(jnp.sum(diff_lam_q2[li].astype(F32) * diff_lam_k2[li].astype(F32))) + lam_init)
        o_a = _diff_attention(z3, lam, diff_subln_g[li].astype(F32), lam_init)
        o_b = _fox_attention(z3, c_col, c_row)
        kc, vc = _nsa_compress(z3, nsa_pe_k[li], nsa_pe_v[li], nsa_wk1[li].astype(BF16), nsa_wk2[li].astype(BF16),
                               nsa_wv1[li].astype(BF16), nsa_wv2[li].astype(BF16))
        o_c = _nsa_attention(z3, kc, vc, gates)
        o_d = _dilated_attention(z3)
        o = jnp.concatenate([o_a, o_b, o_c, o_d], axis=-1).reshape(m, -1)
        h, xn = _proj_residual(o, _cast_pad(w_out, li), h, mix_post_g[li], ffn2_pre_g[li], 1.0)

        h, xn = ffn(h, xn, li, ffn2_w_gate, ffn2_w_up, ffn2_w_down, ffn2_post_g[li], ple_pre_g[li])

        g_next = ffn1_pre_g[li + 1] if li + 1 < depth else None
        h, xn = _proj_residual(xn, _cast_pad(ple_w_gate, li), h, ple_post_g[li], g_next, 1.0,
                               p=p[li].reshape(m, -1), wp=ple_w_proj[li].astype(BF16))
    return h.reshape(b, t, d)
```

```python
import functools
import math

import numpy as np
import jax
import jax.numpy as jnp
from jax import lax
from jax.experimental import pallas as pl
from jax.experimental.pallas import tpu as pltpu

F32 = jnp.float32
BF16 = jnp.bfloat16

HEAD_DIM = 128
N_HEADS_DIFF = 8
N_HEADS_FOX = 7
N_HEADS_NSA = 8
N_KV_NSA = 2
N_HEADS_DIL = 9
ROPE_THETA = 10000.0
EPS = 1e-6
CMP_LEN = 32
CMP_STRIDE = 16
SEL_LEN = 64
SEL_TOPN = 16
WIN_LEN = 512
DIL_PATTERNS = ((128, 1), (512, 4), (2048, 16))
NEG_INF = -1e30
POS_BIG = 1e30

V7X_VMEM_LIMIT_BYTES = 60 * 1024 * 1024
LANES = 128
FF_PAD = 512
ATT_TQ = 256
ATT_TK = 256

COL_QA, COL_KA = 0, 8
COL_QC, COL_KCC, COL_KSC, COL_KWC, COL_QD, COL_KD = 16, 24, 26, 28, 30, 39
COL_VA, COL_QB, COL_KB, COL_VB, COL_VCC, COL_VSC, COL_VWC, COL_VD = 48, 56, 63, 70, 77, 79, 81, 83
N_COLBLK = 92
N_ROPE64_BLK = 16
N_ROPE128_BLK = 32
IN_TN = 512


def _cparams(sem):
    return pltpu.CompilerParams(dimension_semantics=sem, vmem_limit_bytes=V7X_VMEM_LIMIT_BYTES)


def _pick(n, prefs):
    for p in prefs:
        if n % p == 0:
            return p
    return n


def _rmsnorm_kernel(x_ref, g_ref, o_ref):
    x = x_ref[...]
    ms = jnp.mean(x * x, axis=-1, keepdims=True)
    o_ref[...] = (x * lax.rsqrt(ms + EPS) * g_ref[...]).astype(o_ref.dtype)


def _rmsnorm(x, g):
    m, d = x.shape
    bm = _pick(m, (256, 128, 64, 32, 16, 8))
    return pl.pallas_call(
        _rmsnorm_kernel,
        grid=(m // bm,),
        in_specs=[pl.BlockSpec((bm, d), lambda i: (i, 0)), pl.BlockSpec((1, d), lambda i: (0, 0))],
        out_specs=pl.BlockSpec((bm, d), lambda i: (i, 0)),
        out_shape=jax.ShapeDtypeStruct((m, d), BF16),
        compiler_params=_cparams(("parallel",)),
        name="rmsnorm",
    )(x, g.reshape(1, d))


def _gateup_kernel(x_ref, wg_ref, wu_ref, o_ref, *, f_valid):
    x = x_ref[...]
    g = jnp.dot(x, wg_ref[...].astype(BF16), preferred_element_type=F32)
    u = jnp.dot(x, wu_ref[...].astype(BF16), preferred_element_type=F32)
    y = g * jax.nn.sigmoid(g) * u
    bf = y.shape[1]
    if f_valid % bf:
        col = pl.program_id(1) * bf + lax.broadcasted_iota(jnp.int32, (1, bf), 1)
        y = jnp.where(col < f_valid, y, 0.0)
    o_ref[...] = y.astype(o_ref.dtype)


def _gateup(xn, wg_stack, wu_stack, li):
    m, d = xn.shape
    f = wg_stack.shape[2]
    f_pad = -(-f // FF_PAD) * FF_PAD
    bm = _pick(m, (1024, 512, 256, 128))
    bf = _pick(f_pad, (512, 256, 128))
    w_spec = pl.BlockSpec((None, d, bf), lambda i, j: (li, 0, j))
    return pl.pallas_call(
        functools.partial(_gateup_kernel, f_valid=f),
        grid=(m // bm, f_pad // bf),
        in_specs=[pl.BlockSpec((bm, d), lambda i, j: (i, 0), pipeline_mode=pl.Buffered(1)), w_spec, w_spec],
        out_specs=pl.BlockSpec((bm, bf), lambda i, j: (i, j)),
        out_shape=jax.ShapeDtypeStruct((m, f_pad), BF16),
        compiler_params=_cparams(("parallel", "arbitrary")),
        name="ffn_gateup",
    )(xn, wg_stack, wu_stack)


EPI_ROWS = 32


def _residual_epilogue(o_ref, h_ref, gpost_ref, gnext_ref, xn_ref):
    bm = o_ref.shape[0]
    rows = min(EPI_ROWS, bm)

    def body(r, carry):
        sl = pl.ds(pl.multiple_of(r * rows, rows), rows)
        f = o_ref[sl, :]
        hn = h_ref[sl, :] + f * lax.rsqrt(jnp.mean(f * f, axis=-1, keepdims=True) + EPS) * gpost_ref[...]
        o_ref[sl, :] = hn
        if xn_ref is not None:
            ms = jnp.mean(hn * hn, axis=-1, keepdims=True)
            xn_ref[sl, :] = (hn * lax.rsqrt(ms + EPS) * gnext_ref[...]).astype(xn_ref.dtype)
        return carry

    lax.fori_loop(0, bm // rows, body, 0, unroll=2)


def _proj_k_kernel(a_ref, w_ref, h_ref, gpost_ref, gnext_ref, o_ref, *rest, nk):
    k = pl.program_id(1)

    @pl.when(k == 0)
    def _():
        o_ref[...] = jnp.dot(a_ref[...], w_ref[...], preferred_element_type=F32)

    @pl.when(k > 0)
    def _():
        o_ref[...] += jnp.dot(a_ref[...], w_ref[...], preferred_element_type=F32)

    @pl.when(k == nk - 1)
    def _():
        _residual_epilogue(o_ref, h_ref, gpost_ref, gnext_ref, rest[0] if rest else None)


def _proj_n_kernel(*refs, ple, nj, emit_next):
    a_ref, w_ref, h_ref, gpost_ref, gnext_ref = refs[:5]
    refs = refs[5:]
    if ple:
        p_ref, wp_ref = refs[:2]
        refs = refs[2:]
    o_ref = refs[0]
    xn_ref = refs[1] if emit_next else None
    j = pl.program_id(1)
    bn = w_ref.shape[1]
    f = jnp.dot(a_ref[...], w_ref[...], preferred_element_type=F32)
    if ple:
        f = jax.nn.sigmoid(f) * jnp.dot(p_ref[...].astype(BF16), wp_ref[...], preferred_element_type=F32)
    o_ref[:, pl.ds(pl.multiple_of(j * bn, bn), bn)] = f

    @pl.when(j == nj - 1)
    def _():
        _residual_epilogue(o_ref, h_ref, gpost_ref, gnext_ref, xn_ref)


PROJ_FULL_K_MAX = 4096


def _proj_residual(a, w, h, g_post, g_next, coef, p=None, wp=None):
    m, kdim = a.shape
    d = w.shape[1]
    ple = p is not None
    emit_next = g_next is not None
    bm = _pick(m, (512, 256, 128))
    row_specs = [pl.BlockSpec((bm, d), lambda i, k: (i, 0)),
                 pl.BlockSpec((1, d), lambda i, k: (0, 0)),
                 pl.BlockSpec((1, d), lambda i, k: (0, 0))]
    row_args = [h, (coef * g_post).reshape(1, d), (g_next if emit_next else g_post).reshape(1, d)]
    out_specs = [pl.BlockSpec((bm, d), lambda i, k: (i, 0))]
    out_shape = [jax.ShapeDtypeStruct((m, d), F32)]
    if emit_next:
        out_specs.append(pl.BlockSpec((bm, d), lambda i, k: (i, 0), pipeline_mode=pl.Buffered(1)))
        out_shape.append(jax.ShapeDtypeStruct((m, d), BF16))
    if kdim <= PROJ_FULL_K_MAX:
        bn = _pick(d, (512, 256, 128))
        nj = d // bn
        in_specs = [pl.BlockSpec((bm, kdim), lambda i, j: (i, 0)),
                    pl.BlockSpec((kdim, bn), lambda i, j: (0, j))] + row_specs
        args = [a, w] + row_args
        if ple:
            in_specs += [pl.BlockSpec((bm, p.shape[1]), lambda i, j: (i, 0)),
                         pl.BlockSpec((wp.shape[0], bn), lambda i, j: (0, j))]
            args += [p, wp]
        body = functools.partial(_proj_n_kernel, ple=ple, nj=nj, emit_next=emit_next)
        grid = (m // bm, nj)
    else:
        assert not ple
        bk = _pick(kdim, (1024, 512, 256, 128))
        nk = kdim // bk
        in_specs = [pl.BlockSpec((bm, bk), lambda i, k: (i, k)),
                    pl.BlockSpec((bk, d), lambda i, k: (k, 0))] + row_specs
        args = [a, w] + row_args
        body = functools.partial(_proj_k_kernel, nk=nk)
        grid = (m // bm, nk)
    res = pl.pallas_call(
        body,
        grid=grid,
        in_specs=in_specs,
        out_specs=out_specs,
        out_shape=out_shape,
        compiler_params=_cparams(("parallel", "arbitrary")),
        name="ple_residual" if ple else "proj_residual",
    )(*args)
    return (res[0], res[1]) if emit_next else (res[0], None)


def _cast_pad_kernel(x_ref, o_ref, *, rows, cols, masked):
    x = x_ref[...]
    if masked:
        br, bc = x.shape
        r = pl.program_id(0) * br + lax.broadcasted_iota(jnp.int32, x.shape, 0)
        c = pl.program_id(1) * bc + lax.broadcasted_iota(jnp.int32, x.shape, 1)
        x = jnp.where((r < rows) & (c < cols), x, 0.0)
    o_ref[...] = x.astype(o_ref.dtype)


def _cast_pad(w_stack, li, row_mult=1, col_mult=1):
    _, rows, cols = w_stack.shape
    rows_p = -(-rows // row_mult) * row_mult
    cols_p = -(-cols // col_mult) * col_mult
    br = _pick(rows_p, (512, 256, 128))
    bc = _pick(cols_p, (1024, 512, 256, 128))
    last_i, last_j = (rows - 1) // br, (cols - 1) // bc
    return pl.pallas_call(
        functools.partial(_cast_pad_kernel, rows=rows, cols=cols, masked=(rows_p, cols_p) != (rows, cols)),
        grid=(rows_p // br, cols_p // bc),
        in_specs=[pl.BlockSpec((None, br, bc), lambda i, j: (li, jnp.minimum(i, last_i), jnp.minimum(j, last_j)))],
        out_specs=pl.BlockSpec((br, bc), lambda i, j: (i, j)),
        out_shape=jax.ShapeDtypeStruct((rows_p, cols_p), BF16),
        compiler_params=_cparams(("parallel", "parallel")),
        name="weight_cast_pad",
    )(w_stack)


def _rope_tile(acc, cos_ref, sin_ref, half, n_slabs):
    cos = cos_ref[...]
    sin = sin_ref[...]
    outs = []
    for s in range(n_slabs):
        a = acc[:, s * LANES:(s + 1) * LANES]
        if half == LANES // 2:
            partner = pltpu.roll(a, LANES // 2, axis=1)
        else:
            lane = lax.broadcasted_iota(jnp.int32, a.shape, 1)
            fwd = pltpu.roll(a, LANES - half, axis=1)
            bwd = pltpu.roll(a, half, axis=1)
            partner = jnp.where((lane & (2 * half - 1)) < half, fwd, bwd)
        outs.append(a * cos + partner * sin)
    return jnp.concatenate(outs, axis=1)


def _inproj_kernel(x_ref, w_ref, ws_ref, scale_ref, c64_ref, s64_ref, c128_ref, s128_ref,
                   o_ref, os_ref, *, n64, n128):
    j = pl.program_id(1)
    n_slabs = w_ref.shape[1] // LANES

    def product():
        return jnp.dot(x_ref[...], w_ref[...], preferred_element_type=F32)

    @pl.when(j == 0)
    def _():
        os_ref[...] = jnp.dot(x_ref[...], ws_ref[...], preferred_element_type=F32)

    @pl.when(j < n64)
    def _():
        o_ref[...] = (_rope_tile(product(), c64_ref, s64_ref, 32, n_slabs) * scale_ref[...]).astype(o_ref.dtype)

    @pl.when((j >= n64) & (j < n64 + n128))
    def _():
        o_ref[...] = (_rope_tile(product(), c128_ref, s128_ref, 64, n_slabs) * scale_ref[...]).astype(o_ref.dtype)

    @pl.when(j >= n64 + n128)
    def _():
        o_ref[...] = (product() * scale_ref[...]).astype(o_ref.dtype)


def _inproj(xn, w_main, w_small, scale_row, tabs, seq):
    m, d = xn.shape
    n = w_main.shape[1]
    bm = _pick(seq, (1024, 512, 256))
    bn = IN_TN
    tpb = seq // bm
    tab_spec = pl.BlockSpec((bm, LANES), lambda i, j: (i % tpb, 0))
    return pl.pallas_call(
        functools.partial(_inproj_kernel, n64=N_ROPE64_BLK * LANES // bn, n128=N_ROPE128_BLK * LANES // bn),
        grid=(m // bm, n // bn),
        in_specs=[pl.BlockSpec((bm, d), lambda i, j: (i, 0)),
                  pl.BlockSpec((d, bn), lambda i, j: (0, j)),
                  pl.BlockSpec((d, LANES), lambda i, j: (0, 0)),
                  pl.BlockSpec((1, bn), lambda i, j: (0, j)),
                  tab_spec, tab_spec, tab_spec, tab_spec],
        out_specs=[pl.BlockSpec((bm, bn), lambda i, j: (i, j)),
                   pl.BlockSpec((bm, LANES), lambda i, j: (i, 0))],
        out_shape=[jax.ShapeDtypeStruct((m, n), BF16), jax.ShapeDtypeStruct((m, LANES), F32)],
        compiler_params=_cparams(("parallel", "arbitrary")),
        name="in_proj_rope",
    )(xn, w_main, w_small, scale_row, *tabs)


def _softmax_terms(q, k_ref, c_lo, c_mask, c_hi, t_rows, mask_fn, bias_fn=None):
    n0, nm, n1 = c_lo * ATT_TK, c_mask * ATT_TK, c_hi * ATT_TK
    s = lax.dot_general(q, k_ref[n0:n1, :], (((1,), (1,)), ((), ())), preferred_element_type=F32)
    if bias_fn is not None:
        s = s + bias_fn(n0, n1)
    kpos = nm + lax.broadcasted_iota(jnp.int32, (1, n1 - nm), 1)
    tail = jnp.where(mask_fn(t_rows, kpos, nm, n1), s[:, nm - n0:], NEG_INF)
    s = jnp.concatenate([s[:, :nm - n0], tail], axis=1) if nm > n0 else tail
    m = jnp.max(s, axis=1, keepdims=True)
    p = jnp.exp(s - m)
    return p, m, jnp.sum(p, axis=1, keepdims=True)


def _attend(q, k_ref, v_ref, c_lo, c_mask, c_hi, t_rows, mask_fn, bias_fn=None):
    p, m, l = _softmax_terms(q, k_ref, c_lo, c_mask, c_hi, t_rows, mask_fn, bias_fn)
    acc = jnp.dot(p.astype(BF16), v_ref[c_lo * ATT_TK:c_hi * ATT_TK, :], preferred_element_type=F32)
    return m, l, acc


def _causal_mask(t_rows, kpos, n0, n1):
    return kpos <= t_rows


def _row_positions(q0, tq, reps):
    r = lax.broadcasted_iota(jnp.int32, (reps * tq, 1), 0)
    return q0 + (r & (tq - 1))


def _per_query_tile(qi, n_tiles, fn):
    for c in range(n_tiles):
        pl.when(qi == c)(functools.partial(fn, c))


def _first_chunk(c, reach):
    return max(c - (reach + ATT_TK - 1) // ATT_TK, 0)


def _diff_kernel(lam_ref, q_ref, k_ref, v_ref, g_ref, o_ref, *, out_scale):
    tq = q_ref.shape[0]

    def tile(c):
        q = q_ref[...]
        lane = lax.broadcasted_iota(jnp.int32, q.shape, 1)
        zero = jnp.zeros_like(q)
        q2 = jnp.concatenate([jnp.where(lane < HEAD_DIM // 2, q, zero),
                              jnp.where(lane >= HEAD_DIM // 2, q, zero)], axis=0)
        t_rows = _row_positions(c * tq, tq, 2)
        p, _, l = _softmax_terms(q2, k_ref, 0, c, c + 1, t_rows, _causal_mask)
        r = 1.0 / l
        diff_map = p[:tq] * r[:tq] - p[tq:] * (lam_ref[0] * r[tq:])
        a = jnp.dot(diff_map.astype(BF16), v_ref[:(c + 1) * ATT_TK, :], preferred_element_type=F32)
        y = a * lax.rsqrt(jnp.mean(a * a, axis=-1, keepdims=True) + EPS) * g_ref[...]
        o_ref[...] = (y * out_scale).astype(o_ref.dtype)

    _per_query_tile(pl.program_id(2), k_ref.shape[0] // tq, tile)


def _diff_attention(z3, lam, subln_g, lam_init):
    b, t, _ = z3.shape
    tq = ATT_TQ
    return pl.pallas_call(
        functools.partial(_diff_kernel, out_scale=1.0 - lam_init),
        grid=(b, N_HEADS_DIFF, t // tq),
        in_specs=[pl.BlockSpec(memory_space=pltpu.SMEM),
                  pl.BlockSpec((None, tq, LANES), lambda bi, h, qi: (bi, qi, COL_QA + h)),
                  pl.BlockSpec((None, t, LANES), lambda bi, h, qi: (bi, 0, COL_KA + h)),
                  pl.BlockSpec((None, t, LANES), lambda bi, h, qi: (bi, 0, COL_VA + h)),
                  pl.BlockSpec((1, LANES), lambda bi, h, qi: (0, 0))],
        out_specs=pl.BlockSpec((None, tq, LANES), lambda bi, h, qi: (bi, qi, h)),
        out_shape=jax.ShapeDtypeStruct((b, t, N_HEADS_DIFF * HEAD_DIM), BF16),
        compiler_params=_cparams(("parallel", "parallel", "arbitrary")),
        name="diff_attention",
    )(lam.reshape(1), z3, z3, z3, subln_g.reshape(1, LANES))


def _small_prep_kernel(zs_ref, bf_ref, c_ref, ct_ref, g_ref):
    zs = zs_ref[...]
    t = zs.shape[0]
    c = jax.nn.log_sigmoid(zs + bf_ref[...])
    row = lax.broadcasted_iota(jnp.int32, c.shape, 0)
    shift = 1
    while shift < t:
        c = c + jnp.where(row >= shift, pltpu.roll(c, shift, axis=0), 0.0)
        shift *= 2
    c_ref[...] = c
    ct_ref[...] = jnp.transpose(c)[:8, :]
    g_ref[...] = jax.nn.sigmoid(zs)


def _small_prep(zs3, bf_row):
    b, t, _ = zs3.shape
    return pl.pallas_call(
        _small_prep_kernel,
        grid=(b,),
        in_specs=[pl.BlockSpec((None, t, LANES), lambda bi: (bi, 0, 0)),
                  pl.BlockSpec((1, LANES), lambda bi: (0, 0))],
        out_specs=[pl.BlockSpec((None, t, LANES), lambda bi: (bi, 0, 0)),
                   pl.BlockSpec((None, 8, t), lambda bi: (bi, 0, 0)),
                   pl.BlockSpec((None, t, LANES), lambda bi: (bi, 0, 0))],
        out_shape=[jax.ShapeDtypeStruct((b, t, LANES), F32),
                   jax.ShapeDtypeStruct((b, 8, t), F32),
                   jax.ShapeDtypeStruct((b, t, LANES), F32)],
        compiler_params=_cparams(("parallel",)),
        name="forget_cumsum_gates",
    )(zs3, bf_row)


def _fox_kernel(q_ref, k_ref, v_ref, c_ref, ct_ref, o_ref):
    h = pl.program_id(1)
    tq = q_ref.shape[0]

    def tile(c):
        cc = c_ref[...]
        lane = lax.broadcasted_iota(jnp.int32, cc.shape, 1)
        cq = jnp.sum(jnp.where(lane == h, cc, 0.0), axis=1, keepdims=True)

        def bias_fn(n0, n1):
            return cq - ct_ref[pl.ds(h, 1), n0:n1]

        t_rows = _row_positions(c * tq, tq, 1)
        _, l, acc = _attend(q_ref[...], k_ref, v_ref, 0, c, c + 1, t_rows, _causal_mask, bias_fn)
        o_ref[...] = (acc / l).astype(o_ref.dtype)

    _per_query_tile(pl.program_id(2), k_ref.shape[0] // tq, tile)


def _fox_attention(z3, c_col, c_row):
    b, t, _ = z3.shape
    tq = ATT_TQ
    return pl.pallas_call(
        _fox_kernel,
        grid=(b, N_HEADS_FOX, t // tq),
        in_specs=[pl.BlockSpec((None, tq, LANES), lambda bi, h, qi: (bi, qi, COL_QB + h)),
                  pl.BlockSpec((None, t, LANES), lambda bi, h, qi: (bi, 0, COL_KB + h)),
                  pl.BlockSpec((None, t, LANES), lambda bi, h, qi: (bi, 0, COL_VB + h)),
                  pl.BlockSpec((None, tq, LANES), lambda bi, h, qi: (bi, qi, 0)),
                  pl.BlockSpec((None, 8, t), lambda bi, h, qi: (bi, 0, 0))],
        out_specs=pl.BlockSpec((None, tq, LANES), lambda bi, h, qi: (bi, qi, h)),
        out_shape=jax.ShapeDtypeStruct((b, t, N_HEADS_FOX * HEAD_DIM), BF16),
        compiler_params=_cparams(("parallel", "parallel", "arbitrary")),
        name="forgetting_attention",
    )(z3, z3, z3, c_col, c_row)


def _compress_one(x_ref, xs_ref, pe_ref, w1_ref, w2_ref, o_ref):
    t = x_ref.shape[0]
    nb = t // CMP_STRIDE
    xs_ref[...] = x_ref[...].astype(F32)
    pe = pe_ref[...]
    lo, hi = [], []
    for i in range(CMP_STRIDE):
        xi = xs_ref[pl.ds(i, nb, stride=CMP_STRIDE), :]
        lo.append((xi + pe[i:i + 1, :]).astype(BF16))
        hi.append((xi + pe[CMP_STRIDE + i:CMP_STRIDE + i + 1, :]).astype(BF16))
    half = CMP_STRIDE * HEAD_DIM
    a = jnp.dot(jnp.concatenate(lo, axis=1), w1_ref[:half, :], preferred_element_type=F32)
    bb = jnp.dot(jnp.concatenate(hi, axis=1), w1_ref[half:, :], preferred_element_type=F32)
    y = a + pltpu.roll(bb, nb - 1, axis=0)
    out = jnp.dot(jax.nn.gelu(y).astype(BF16), w2_ref[...], preferred_element_type=F32)
    row = lax.broadcasted_iota(jnp.int32, out.shape, 0)
    o_ref[...] = jnp.where(row < nb - 1, out, 0.0).astype(o_ref.dtype)


def _compress_kernel(k_ref, v_ref, pek_ref, pev_ref, wk1_ref, wk2_ref, wv1_ref, wv2_ref,
                     kc_ref, vc_ref, xs_ref):
    _compress_one(k_ref, xs_ref, pek_ref, wk1_ref, wk2_ref, kc_ref)
    _compress_one(v_ref, xs_ref, pev_ref, wv1_ref, wv2_ref, vc_ref)


def _nsa_compress(z3, pe_k, pe_v, wk1, wk2, wv1, wv2):
    b, t, _ = z3.shape
    nb = t // CMP_STRIDE
    full = lambda a: pl.BlockSpec(a.shape, lambda bi, g: (0,) * a.ndim)
    return pl.pallas_call(
        _compress_kernel,
        grid=(b, N_KV_NSA),
        in_specs=[pl.BlockSpec((None, t, LANES), lambda bi, g: (bi, 0, COL_KCC + g)),
                  pl.BlockSpec((None, t, LANES), lambda bi, g: (bi, 0, COL_VCC + g)),
                  full(pe_k), full(pe_v), full(wk1), full(wk2), full(wv1), full(wv2)],
        out_specs=[pl.BlockSpec((None, None, nb, LANES), lambda bi, g: (bi, g, 0, 0)),
                   pl.BlockSpec((None, None, nb, LANES), lambda bi, g: (bi, g, 0, 0))],
        out_shape=[jax.ShapeDtypeStruct((b, N_KV_NSA, nb, LANES), BF16),
                   jax.ShapeDtypeStruct((b, N_KV_NSA, nb, LANES), BF16)],
        scratch_shapes=[pltpu.VMEM((t, LANES), F32)],
        compiler_params=_cparams(("parallel", "parallel")),
        name="nsa_compress",
    )(z3, z3, pe_k, pe_v, wk1, wk2, wv1, wv2)


def _nsa_tile(c, g, q_ref, kc_ref, vc_ref, ks_ref, vs_ref, kw_ref, vw_ref, gate_ref, o_ref, seq):
    tq = q_ref.shape[0]
    reps = N_HEADS_NSA // N_KV_NSA
    q0 = c * tq
    q4 = jnp.concatenate([q_ref[:, r * LANES:(r + 1) * LANES] for r in range(reps)], axis=0)
    t_rows = _row_positions(q0, tq, reps)

    nb = kc_ref.shape[0]
    n_cmp = (seq - CMP_LEN) // CMP_STRIDE + 1
    s = lax.dot_general(q4, kc_ref[...], (((1,), (1,)), ((), ())), preferred_element_type=F32)
    cidx = lax.broadcasted_iota(jnp.int32, (1, nb), 1)
    valid = (cidx * CMP_STRIDE + (CMP_LEN - 1) <= t_rows) & (cidx < n_cmp)
    s = jnp.where(valid, s, NEG_INF)
    e = jnp.where(valid, jnp.exp(s - jnp.max(s, axis=1, keepdims=True)), 0.0)
    den = jnp.sum(e, axis=1, keepdims=True)
    p_cmp = e / jnp.where(den > 0, den, 1.0)
    o_cmp = jnp.dot(p_cmp.astype(BF16), vc_ref[...], preferred_element_type=F32)

    n_sel = seq // SEL_LEN
    n_sel_pad = LANES
    p_sum = p_cmp[:tq]
    for r in range(1, reps):
        p_sum = p_sum + p_cmp[r * tq:(r + 1) * tq]
    p_hi = p_sum.astype(BF16)
    p_lo = (p_sum - p_hi.astype(F32)).astype(BF16)
    srow = lax.broadcasted_iota(jnp.int32, (n_sel_pad, nb), 0)
    ccol = lax.broadcasted_iota(jnp.int32, (n_sel_pad, nb), 1)
    overlap = ((ccol * CMP_STRIDE < srow * SEL_LEN + SEL_LEN) &
               (ccol * CMP_STRIDE + CMP_LEN > srow * SEL_LEN) & (ccol < n_cmp) & (srow < n_sel))
    ov = jnp.where(overlap, 1.0, 0.0).astype(BF16)
    nt = (((1,), (1,)), ((), ()))
    imp_t = (lax.dot_general(ov, p_hi, nt, preferred_element_type=F32) +
             lax.dot_general(ov, p_lo, nt, preferred_element_type=F32))
    n_rank = ((n_sel + 7) // 8) * 8
    imp_t = imp_t[:n_rank]
    sblk = lax.broadcasted_iota(jnp.int32, (n_rank, tq), 0)
    tpos = q0 + lax.broadcasted_iota(jnp.int32, (n_rank, tq), 1)
    cur = tpos // SEL_LEN
    forced = (sblk == 0) | (sblk == cur) | (sblk == cur - 1)
    val = jnp.where(forced, POS_BIG, jnp.where(sblk * SEL_LEN <= tpos, imp_t, NEG_INF))
    val = jnp.where(sblk < n_sel, val, -2.0 * POS_BIG)
    rank = jnp.zeros((n_rank, tq), F32)
    for s2 in range(n_sel):
        other = val[s2:s2 + 1, :]
        tie = jnp.where(sblk > s2, 1.0, 0.0)
        rank = rank + jnp.where(other > val, 1.0, jnp.where(other == val, tie, 0.0))
    n_top = min(SEL_TOPN, n_sel)
    sel_t = jnp.where((rank < n_top) & (sblk < n_sel), 1.0, 0.0)
    if n_rank < n_sel_pad:
        sel_t = jnp.concatenate([sel_t, jnp.zeros((n_sel_pad - n_rank, tq), F32)], axis=0)
    sel = jnp.transpose(sel_t).astype(BF16)

    def sel_mask(t_r, kpos, n0, n1):
        erow = lax.broadcasted_iota(jnp.int32, (n_sel_pad, n1 - n0), 0)
        ecol = n0 + lax.broadcasted_iota(jnp.int32, (n_sel_pad, n1 - n0), 1)
        expand = jnp.where(ecol // SEL_LEN == erow, 1.0, 0.0).astype(BF16)
        chosen = jnp.dot(sel, expand, preferred_element_type=F32)
        chosen = jnp.concatenate([chosen] * reps, axis=0)
        return (chosen > 0.5) & (kpos <= t_r)

    _, l_s, acc_s = _attend(q4, ks_ref, vs_ref, 0, 0, c + 1, t_rows, sel_mask)
    o_sel = acc_s / l_s

    def win_mask(t_r, kpos, n0, n1):
        return (kpos <= t_r) & (kpos > t_r - WIN_LEN)

    lo = _first_chunk(c, WIN_LEN)
    _, l_w, acc_w = _attend(q4, kw_ref, vw_ref, lo, lo, c + 1, t_rows, win_mask)
    o_win = acc_w / l_w

    gates = gate_ref[...]
    lane = lax.broadcasted_iota(jnp.int32, gates.shape, 1)
    n_fb = N_HEADS_FOX
    outs = []
    for r in range(reps):
        col = n_fb + 3 * (g * reps + r)
        rs = slice(r * tq, (r + 1) * tq)
        o_r = jnp.zeros((tq, LANES), F32)
        for br, o_br in enumerate((o_cmp, o_sel, o_win)):
            gcol = jnp.sum(jnp.where(lane == col + br, gates, 0.0), axis=1, keepdims=True)
            o_r = o_r + gcol * o_br[rs]
        outs.append(o_r)
    o_ref[...] = jnp.concatenate(outs, axis=1).astype(o_ref.dtype)


def _nsa_kernel(*refs, seq):
    tq = refs[0].shape[0]
    g = pl.program_id(1)
    _per_query_tile(pl.program_id(2), seq // tq, lambda c: _nsa_tile(c, g, *refs, seq))


def _nsa_attention(z3, kc, vc, gates):
    b, t, _ = z3.shape
    tq = ATT_TQ
    reps = N_HEADS_NSA // N_KV_NSA
    nb = kc.shape[2]
    kv = lambda col: pl.BlockSpec((None, t, LANES), lambda bi, g, qi: (bi, 0, col + g))
    return pl.pallas_call(
        functools.partial(_nsa_kernel, seq=t),
        grid=(b, N_KV_NSA, t // tq),
        in_specs=[pl.BlockSpec((None, tq, reps * LANES), lambda bi, g, qi: (bi, qi, COL_QC // reps + g)),
                  pl.BlockSpec((None, None, nb, LANES), lambda bi, g, qi: (bi, g, 0, 0)),
                  pl.BlockSpec((None, None, nb, LANES), lambda bi, g, qi: (bi, g, 0, 0)),
                  kv(COL_KSC), kv(COL_VSC), kv(COL_KWC), kv(COL_VWC),
                  pl.BlockSpec((None, tq, LANES), lambda bi, g, qi: (bi, qi, 0))],
        out_specs=pl.BlockSpec((None, tq, reps * LANES), lambda bi, g, qi: (bi, qi, g)),
        out_shape=jax.ShapeDtypeStruct((b, t, N_HEADS_NSA * HEAD_DIM), BF16),
        compiler_params=_cparams(("parallel", "parallel", "arbitrary")),
        name="nsa_attention",
    )(z3, kc, vc, z3, z3, z3, z3, gates)


def _dil_kernel(*refs):
    n_g = len(DIL_PATTERNS)
    q_refs, k_refs, v_refs = refs[:n_g], refs[n_g:2 * n_g], refs[2 * n_g:3 * n_g]
    o_ref = refs[3 * n_g]
    tq = o_ref.shape[0]

    def tile(c):
        t_rows = _row_positions(c * tq, tq, 1)
        parts = []
        for gi, (w, r) in enumerate(DIL_PATTERNS):
            def mask(t_r, kpos, n0, n1, w=w, r=r):
                dist = t_r - kpos
                return (dist >= 0) & (dist <= w) & ((dist & (r - 1)) == 0)

            lo = _first_chunk(c, w)
            parts.append(_attend(q_refs[gi][...], k_refs[gi], v_refs[gi], lo, lo, c + 1, t_rows, mask))
        lses = [m + jnp.log(l) for (m, l, _) in parts]
        top = functools.reduce(jnp.maximum, lses)
        ws = [jnp.exp(x - top) for x in lses]
        tot = functools.reduce(lambda a, b2: a + b2, ws)
        out = jnp.zeros((tq, LANES), F32)
        for wgt, (m, l, acc) in zip(ws, parts):
            out = out + (wgt / tot) * (acc / l)
        o_ref[...] = out.astype(o_ref.dtype)

    _per_query_tile(pl.program_id(2), k_refs[0].shape[0] // tq, tile)


def _dilated_attention(z3):
    b, t, _ = z3.shape
    tq = ATT_TQ
    n_g = len(DIL_PATTERNS)
    hg = N_HEADS_DIL // n_g
    qs = [pl.BlockSpec((None, tq, LANES), lambda bi, j, qi, gi=gi: (bi, qi, COL_QD + gi * hg + j)) for gi in range(n_g)]
    ks = [pl.BlockSpec((None, t, LANES), lambda bi, j, qi, gi=gi: (bi, 0, COL_KD + gi * hg + j)) for gi in range(n_g)]
    vs = [pl.BlockSpec((None, t, LANES), lambda bi, j, qi, gi=gi: (bi, 0, COL_VD + gi * hg + j)) for gi in range(n_g)]
    return pl.pallas_call(
        _dil_kernel,
        grid=(b, hg, t // tq),
        in_specs=qs + ks + vs,
        out_specs=pl.BlockSpec((None, tq, LANES), lambda bi, j, qi: (bi, qi, j)),
        out_shape=jax.ShapeDtypeStruct((b, t, hg * HEAD_DIM), BF16),
        compiler_params=_cparams(("parallel", "parallel", "arbitrary")),
        name="dilated_attention",
    )(*([z3] * (3 * n_g)))


def _in_proj_sizes():
    hd = HEAD_DIM
    kv = N_KV_NSA * hd
    return (N_HEADS_DIFF * hd, N_HEADS_DIFF * hd, N_HEADS_DIFF * hd,
            N_HEADS_FOX * hd, N_HEADS_FOX * hd, N_HEADS_FOX * hd, N_HEADS_FOX,
            N_HEADS_NSA * hd, kv, kv, kv, kv, kv, kv, 3 * N_HEADS_NSA,
            N_HEADS_DIL * hd, N_HEADS_DIL * hd, N_HEADS_DIL * hd)


_SEC_NAMES = ("qa", "ka", "va", "qb", "kb", "vb", "fb", "qc", "kcc", "vcc", "ksc", "vsc", "kwc", "vwc", "gc",
              "qd", "kd", "vd")
_SEC_ORDER = ("qa", "ka", "qc", "kcc", "ksc", "kwc", "qd", "kd", "va", "qb", "kb", "vb", "vcc", "vsc", "vwc", "vd")


def _pack_cols_kernel(blk_ref, cls_ref, a_ref, b_ref, o_ref, *, shifts):
    j = pl.program_id(0)
    for ci, s in enumerate(shifts):
        @pl.when(cls_ref[j] == ci)
        def _(s=s):
            a = a_ref[...]
            if s:
                lane = lax.broadcasted_iota(jnp.int32, a.shape, 1)
                a = jnp.where(lane < LANES - s, pltpu.roll(a, LANES - s, axis=1),
                              pltpu.roll(b_ref[...], LANES - s, axis=1))
            o_ref[...] = a.astype(o_ref.dtype)


def _pack_small_kernel(*refs, parts):
    o_ref = refs[-1]
    lane = lax.broadcasted_iota(jnp.int32, o_ref.shape, 1)
    out = jnp.zeros(o_ref.shape, F32)
    for ref, (_, src_lane, dst_lane, width) in zip(refs[:-1], parts):
        x = ref[...]
        if (dst_lane - src_lane) % LANES:
            x = pltpu.roll(x, (dst_lane - src_lane) % LANES, axis=1)
        out = jnp.where((lane >= dst_lane) & (lane < dst_lane + width), x, out)
    o_ref[...] = out.astype(o_ref.dtype)


def _pack_w_in(w_stack, li):
    _, rows, cols = w_stack.shape
    sizes = _in_proj_sizes()
    offs = np.concatenate([[0], np.cumsum(sizes)])
    start = {n: int(offs[i]) for i, n in enumerate(_SEC_NAMES)}
    size = {n: int(sizes[i]) for i, n in enumerate(_SEC_NAMES)}
    src = np.concatenate([start[n] + np.arange(0, size[n], LANES) for n in _SEC_ORDER])
    shifts = tuple(sorted(set(int(s) for s in src % LANES)))
    blk = jnp.asarray(src // LANES, jnp.int32)
    cls = jnp.asarray([shifts.index(int(s)) for s in src % LANES], jnp.int32)
    last = (cols - 1) // LANES
    main = pl.pallas_call(
        functools.partial(_pack_cols_kernel, shifts=shifts),
        grid_spec=pltpu.PrefetchScalarGridSpec(
            num_scalar_prefetch=2,
            grid=(len(src),),
            in_specs=[pl.BlockSpec((None, rows, LANES), lambda j, blk, cls: (li, 0, blk[j])),
                      pl.BlockSpec((None, rows, LANES), lambda j, blk, cls: (li, 0, jnp.minimum(blk[j] + 1, last)))],
            out_specs=pl.BlockSpec((rows, LANES), lambda j, blk, cls: (0, j))),
        out_shape=jax.ShapeDtypeStruct((rows, len(src) * LANES), BF16),
        compiler_params=_cparams(("parallel",)),
        name="w_in_pack",
    )(blk, cls, w_stack, w_stack)
    parts, dest = [], 0
    for n in ("fb", "gc"):
        assert start[n] // LANES == (start[n] + size[n] - 1) // LANES
        parts.append((start[n] // LANES, start[n] % LANES, dest, size[n]))
        dest += size[n]
    small = pl.pallas_call(
        functools.partial(_pack_small_kernel, parts=tuple(parts)),
        grid=(1,),
        in_specs=[pl.BlockSpec((None, rows, LANES), lambda i, b=b: (li, 0, b)) for b, _, _, _ in parts],
        out_specs=pl.BlockSpec((rows, LANES), lambda i: (0, 0)),
        out_shape=jax.ShapeDtypeStruct((rows, LANES), BF16),
        compiler_params=_cparams(("arbitrary",)),
        name="w_in_pack_narrow",
    )(*([w_stack] * len(parts)))
    scales = {"qa": (HEAD_DIM // 2) ** -0.5, "qb": HEAD_DIM ** -0.5, "qc": HEAD_DIM ** -0.5, "qd": HEAD_DIM ** -0.5}
    scale_row = jnp.concatenate([jnp.full((1, size[n]), scales.get(n, 1.0), F32) for n in _SEC_ORDER], axis=1)
    return main, small, scale_row


def _rope_tables(t):
    pos = jnp.arange(t, dtype=F32)[:, None]
    tabs = []
    for d in (HEAD_DIM // 2, HEAD_DIM):
        half = d // 2
        inv = ROPE_THETA ** (-jnp.arange(half, dtype=F32) * 2.0 / d)
        ang = pos * inv[None, :]
        cos, sin = jnp.cos(ang), jnp.sin(ang)
        reps = LANES // d
        tabs.append(jnp.tile(jnp.concatenate([cos, cos], axis=1), (1, reps)))
        tabs.append(jnp.tile(jnp.concatenate([-sin, sin], axis=1), (1, reps)))
    return tabs


def kernel(x, p, ffn1_pre_g, ffn1_w_gate, ffn1_w_up, ffn1_w_down, ffn1_post_g, mix_pre_g, w_in, fox_bf, diff_lam_q1, diff_lam_k1, diff_lam_q2, diff_lam_k2, diff_subln_g, nsa_pe_k, nsa_pe_v, nsa_wk1, nsa_wk2, nsa_wv1, nsa_wv2, w_out, mix_post_g, ffn2_pre_g, ffn2_w_gate, ffn2_w_up, ffn2_w_down, ffn2_post_g, ple_pre_g, ple_w_gate, ple_w_proj, ple_post_g):
    b, t, d = x.shape
    depth = w_in.shape[0]
    m = b * t
    tabs = _rope_tables(t)
    h = x.reshape(m, d)
    xn = _rmsnorm(h, ffn1_pre_g[0])

    def ffn(h, xn, li, wg, wu, wd, g_post, g_next):
        act = _gateup(xn, wg, wu, li)
        return _proj_residual(act, _cast_pad(wd, li, row_mult=FF_PAD), h, g_post, g_next, 0.5)

    for li in range(depth):
        h, xn = ffn(h, xn, li, ffn1_w_gate, ffn1_w_up, ffn1_w_down, ffn1_post_g[li], mix_pre_g[li])

        w_main, w_small, scale_row = _pack_w_in(w_in, li)
        z, zs = _inproj(xn, w_main, w_small, scale_row, tabs, t)
        z3 = z.reshape(b, t, z.shape[1])
        bf_row = jnp.pad(fox_bf[li].astype(F32), (0, LANES - N_HEADS_FOX)).reshape(1, LANES)
        c_col, c_row, gates = _small_prep(zs.reshape(b, t, LANES), bf_row)

        lam_init = 0.8 - 0.6 * math.exp(-0.3 * li)
        lam = (jnp.exp(jnp.sum(diff_lam_q1[li].astype(F32) * diff_lam_k1[li].astype(F32)))
               - jnp.exp(jnp.sum(diff_lam_q2[li].astype(F32) * diff_lam_k2[li].astype(F32))) + lam_init)
        o_a = _diff_attention(z3, lam, diff_subln_g[li].astype(F32), lam_init)
        o_b = _fox_attention(z3, c_col, c_row)
        kc, vc = _nsa_compress(z3, nsa_pe_k[li], nsa_pe_v[li], nsa_wk1[li].astype(BF16), nsa_wk2[li].astype(BF16),
                               nsa_wv1[li].astype(BF16), nsa_wv2[li].astype(BF16))
        o_c = _nsa_attention(z3, kc, vc, gates)
        o_d = _dilated_attention(z3)
        o = jnp.concatenate([o_a, o_b, o_c, o_d], axis=-1).reshape(m, -1)
        h, xn = _proj_residual(o, _cast_pad(w_out, li), h, mix_post_g[li], ffn2_pre_g[li], 1.0)

        h, xn = ffn(h, xn, li, ffn2_w_gate, ffn2_w_up, ffn2_w_down, ffn2_post_g[li], ple_pre_g[li])

        g_next = ffn1_pre_g[li + 1] if li + 1 < depth else None
        h, xn = _proj_residual(xn, _cast_pad(ple_w_gate, li), h, ple_post_g[li], g_next, 1.0,
                               p=p[li].reshape(m, -1), wp=ple_w_proj[li].astype(BF16))
    return h.reshape(b, t, d)
```

```python
import functools
import math

import numpy as np
import jax
import jax.numpy as jnp
from jax import lax
from jax.experimental import pallas as pl
from jax.experimental.pallas import tpu as pltpu

F32 = jnp.float32
BF16 = jnp.bfloat16

HEAD_DIM = 128
N_HEADS_DIFF = 8
N_HEADS_FOX = 7
N_HEADS_NSA = 8
N_KV_NSA = 2
N_HEADS_DIL = 9
ROPE_THETA = 10000.0
EPS = 1e-6
CMP_LEN = 32
CMP_STRIDE = 16
SEL_LEN = 64
SEL_TOPN = 16
WIN_LEN = 512
DIL_PATTERNS = ((128, 1), (512, 4), (2048, 16))
NEG_INF = -1e30
POS_BIG = 1e30

V7X_VMEM_LIMIT_BYTES = 60 * 1024 * 1024
LANES = 128
FF_PAD = 512
ATT_TQ = 256
ATT_TK = 256

COL_QA, COL_KA = 0, 8
COL_QC, COL_KCC, COL_KSC, COL_KWC, COL_QD, COL_KD = 16, 24, 26, 28, 30, 39
COL_VA, COL_QB, COL_KB, COL_VB, COL_VCC, COL_VSC, COL_VWC, COL_VD = 48, 56, 63, 70, 77, 79, 81, 83
N_COLBLK = 92
N_ROPE64_BLK = 16
N_ROPE128_BLK = 32
IN_TN = 512


def _cparams(sem):
    return pltpu.CompilerParams(dimension_semantics=sem, vmem_limit_bytes=V7X_VMEM_LIMIT_BYTES)


def _pick(n, prefs):
    for p in prefs:
        if n % p == 0:
            return p
    return n


def _rmsnorm_kernel(x_ref, g_ref, o_ref):
    x = x_ref[...]
    ms = jnp.mean(x * x, axis=-1, keepdims=True)
    o_ref[...] = (x * lax.rsqrt(ms + EPS) * g_ref[...]).astype(o_ref.dtype)


def _rmsnorm(x, g):
    m, d = x.shape
    bm = _pick(m, (256, 128, 64, 32, 16, 8))
    return pl.pallas_call(
        _rmsnorm_kernel,
        grid=(m // bm,),
        in_specs=[pl.BlockSpec((bm, d), lambda i: (i, 0)), pl.BlockSpec((1, d), lambda i: (0, 0))],
        out_specs=pl.BlockSpec((bm, d), lambda i: (i, 0)),
        out_shape=jax.ShapeDtypeStruct((m, d), BF16),
        compiler_params=_cparams(("parallel",)),
        name="rmsnorm",
    )(x, g.reshape(1, d))


def _gateup_kernel(x_ref, wg_ref, wu_ref, o_ref, *, f_valid):
    x = x_ref[...]
    g = jnp.dot(x, wg_ref[...].astype(BF16), preferred_element_type=F32)
    u = jnp.dot(x, wu_ref[...].astype(BF16), preferred_element_type=F32)
    y = g * jax.nn.sigmoid(g) * u
    bf = y.shape[1]
    if f_valid % bf:
        col = pl.program_id(1) * bf + lax.broadcasted_iota(jnp.int32, (1, bf), 1)
        y = jnp.where(col < f_valid, y, 0.0)
    o_ref[...] = y.astype(o_ref.dtype)


def _gateup(xn, wg_stack, wu_stack, li):
    m, d = xn.shape
    f = wg_stack.shape[2]
    f_pad = -(-f // FF_PAD) * FF_PAD
    bm = _pick(m, (1024, 512, 256, 128))
    bf = _pick(f_pad, (512, 256, 128))
    w_spec = pl.BlockSpec((None, d, bf), lambda i, j: (li, 0, j))
    return pl.pallas_call(
        functools.partial(_gateup_kernel, f_valid=f),
        grid=(m // bm, f_pad // bf),
        in_specs=[pl.BlockSpec((bm, d), lambda i, j: (i, 0), pipeline_mode=pl.Buffered(1)), w_spec, w_spec],
        out_specs=pl.BlockSpec((bm, bf), lambda i, j: (i, j)),
        out_shape=jax.ShapeDtypeStruct((m, f_pad), BF16),
        compiler_params=_cparams(("parallel", "arbitrary")),
        name="ffn_gateup",
    )(xn, wg_stack, wu_stack)


EPI_ROWS = 32


def _residual_epilogue(o_ref, h_ref, gpost_ref, gnext_ref, xn_ref):
    bm = o_ref.shape[0]
    rows = min(EPI_ROWS, bm)

    def body(r, carry):
        sl = pl.ds(pl.multiple_of(r * rows, rows), rows)
        f = o_ref[sl, :]
        hn = h_ref[sl, :] + f * lax.rsqrt(jnp.mean(f * f, axis=-1, keepdims=True) + EPS) * gpost_ref[...]
        o_ref[sl, :] = hn
        if xn_ref is not None:
            ms = jnp.mean(hn * hn, axis=-1, keepdims=True)
            xn_ref[sl, :] = (hn * lax.rsqrt(ms + EPS) * gnext_ref[...]).astype(xn_ref.dtype)
        return carry

    lax.fori_loop(0, bm // rows, body, 0, unroll=2)


def _proj_k_kernel(a_ref, w_ref, h_ref, gpost_ref, gnext_ref, o_ref, *rest, nk):
    k = pl.program_id(1)

    @pl.when(k == 0)
    def _():
        o_ref[...] = jnp.dot(a_ref[...], w_ref[...], preferred_element_type=F32)

    @pl.when(k > 0)
    def _():
        o_ref[...] += jnp.dot(a_ref[...], w_ref[...], preferred_element_type=F32)

    @pl.when(k == nk - 1)
    def _():
        _residual_epilogue(o_ref, h_ref, gpost_ref, gnext_ref, rest[0] if rest else None)


def _proj_n_kernel(*refs, ple, nj, emit_next):
    a_ref, w_ref, h_ref, gpost_ref, gnext_ref = refs[:5]
    refs = refs[5:]
    if ple:
        p_ref, wp_ref = refs[:2]
        refs = refs[2:]
    o_ref = refs[0]
    xn_ref = refs[1] if emit_next else None
    j = pl.program_id(1)
    bn = w_ref.shape[1]
    f = jnp.dot(a_ref[...], w_ref[...], preferred_element_type=F32)
    if ple:
        f = jax.nn.sigmoid(f) * jnp.dot(p_ref[...].astype(BF16), wp_ref[...], preferred_element_type=F32)
    o_ref[:, pl.ds(pl.multiple_of(j * bn, bn), bn)] = f

    @pl.when(j == nj - 1)
    def _():
        _residual_epilogue(o_ref, h_ref, gpost_ref, gnext_ref, xn_ref)


PROJ_FULL_K_MAX = 4096


def _proj_residual(a, w, h, g_post, g_next, coef, p=None, wp=None):
    m, kdim = a.shape
    d = w.shape[1]
    ple = p is not None
    emit_next = g_next is not None
    bm = _pick(m, (512, 256, 128))
    row_specs = [pl.BlockSpec((bm, d), lambda i, k: (i, 0)),
                 pl.BlockSpec((1, d), lambda i, k: (0, 0)),
                 pl.BlockSpec((1, d), lambda i, k: (0, 0))]
    row_args = [h, (coef * g_post).reshape(1, d), (g_next if emit_next else g_post).reshape(1, d)]
    out_specs = [pl.BlockSpec((bm, d), lambda i, k: (i, 0))]
    out_shape = [jax.ShapeDtypeStruct((m, d), F32)]
    if emit_next:
        out_specs.append(pl.BlockSpec((bm, d), lambda i, k: (i, 0), pipeline_mode=pl.Buffered(1)))
        out_shape.append(jax.ShapeDtypeStruct((m, d), BF16))
    if kdim <= PROJ_FULL_K_MAX:
        bn = _pick(d, (512, 256, 128))
        nj = d // bn
        in_specs = [pl.BlockSpec((bm, kdim), lambda i, j: (i, 0)),
                    pl.BlockSpec((kdim, bn), lambda i, j: (0, j))] + row_specs
        args = [a, w] + row_args
        if ple:
            in_specs += [pl.BlockSpec((bm, p.shape[1]), lambda i, j: (i, 0)),
                         pl.BlockSpec((wp.shape[0], bn), lambda i, j: (0, j))]
            args += [p, wp]
        body = functools.partial(_proj_n_kernel, ple=ple, nj=nj, emit_next=emit_next)
        grid = (m // bm, nj)
    else:
        assert not ple
        bk = _pick(kdim, (1024, 512, 256, 128))
        nk = kdim // bk
        in_specs = [pl.BlockSpec((bm, bk), lambda i, k: (i, k)),
                    pl.BlockSpec((bk, d), lambda i, k: (k, 0))] + row_specs
        args = [a, w] + row_args
        body = functools.partial(_proj_k_kernel, nk=nk)
        grid = (m // bm, nk)
    res = pl.pallas_call(
        body,
        grid=grid,
        in_specs=in_specs,
        out_specs=out_specs,
        out_shape=out_shape,
        compiler_params=_cparams(("parallel", "arbitrary")),
        name="ple_residual" if ple else "proj_residual",
    )(*args)
    return (res[0], res[1]) if emit_next else (res[0], None)


def _cast_pad_kernel(x_ref, o_ref, *, rows, cols, masked):
    x = x_ref[...]
    if masked:
        br, bc = x.shape
        r = pl.program_id(0) * br + lax.broadcasted_iota(jnp.int32, x.shape, 0)
        c = pl.program_id(1) * bc + lax.broadcasted_iota(jnp.int32, x.shape, 1)
        x = jnp.where((r < rows) & (c < cols), x, 0.0)
    o_ref[...] = x.astype(o_ref.dtype)


def _cast_pad(w_stack, li, row_mult=1, col_mult=1):
    _, rows, cols = w_stack.shape
    rows_p = -(-rows // row_mult) * row_mult
    cols_p = -(-cols // col_mult) * col_mult
    br = _pick(rows_p, (512, 256, 128))
    bc = _pick(cols_p, (1024, 512, 256, 128))
    last_i, last_j = (rows - 1) // br, (cols - 1) // bc
    return pl.pallas_call(
        functools.partial(_cast_pad_kernel, rows=rows, cols=cols, masked=(rows_p, cols_p) != (rows, cols)),
        grid=(rows_p // br, cols_p // bc),
        in_specs=[pl.BlockSpec((None, br, bc), lambda i, j: (li, jnp.minimum(i, last_i), jnp.minimum(j, last_j)))],
        out_specs=pl.BlockSpec((br, bc), lambda i, j: (i, j)),
        out_shape=jax.ShapeDtypeStruct((rows_p, cols_p), BF16),
        compiler_params=_cparams(("parallel", "parallel")),
        name="weight_cast_pad",
    )(w_stack)


def _rope_tile(acc, cos_ref, sin_ref, half, n_slabs):
    cos = cos_ref[...]
    sin = sin_ref[...]
    outs = []
    for s in range(n_slabs):
        a = acc[:, s * LANES:(s + 1) * LANES]
        if half == LANES // 2:
            partner = pltpu.roll(a, LANES // 2, axis=1)
        else:
            lane = lax.broadcasted_iota(jnp.int32, a.shape, 1)
            fwd = pltpu.roll(a, LANES - half, axis=1)
            bwd = pltpu.roll(a, half, axis=1)
            partner = jnp.where((lane & (2 * half - 1)) < half, fwd, bwd)
        outs.append(a * cos + partner * sin)
    return jnp.concatenate(outs, axis=1)


def _inproj_kernel(x_ref, w_ref, ws_ref, scale_ref, c64_ref, s64_ref, c128_ref, s128_ref,
                   o_ref, os_ref, *, n64, n128):
    j = pl.program_id(1)
    n_slabs = w_ref.shape[1] // LANES

    def product():
        return jnp.dot(x_ref[...], w_ref[...], preferred_element_type=F32)

    @pl.when(j == 0)
    def _():
        os_ref[...] = jnp.dot(x_ref[...], ws_ref[...], preferred_element_type=F32)

    @pl.when(j < n64)
    def _():
        o_ref[...] = (_rope_tile(product(), c64_ref, s64_ref, 32, n_slabs) * scale_ref[...]).astype(o_ref.dtype)

    @pl.when((j >= n64) & (j < n64 + n128))
    def _():
        o_ref[...] = (_rope_tile(product(), c128_ref, s128_ref, 64, n_slabs) * scale_ref[...]).astype(o_ref.dtype)

    @pl.when(j >= n64 + n128)
    def _():
        o_ref[...] = (product() * scale_ref[...]).astype(o_ref.dtype)


def _inproj(xn, w_main, w_small, scale_row, tabs, seq):
    m, d = xn.shape
    n = w_main.shape[1]
    bm = _pick(seq, (1024, 512, 256))
    bn = IN_TN
    tpb = seq // bm
    tab_spec = pl.BlockSpec((bm, LANES), lambda i, j: (i % tpb, 0))
    return pl.pallas_call(
        functools.partial(_inproj_kernel, n64=N_ROPE64_BLK * LANES // bn, n128=N_ROPE128_BLK * LANES // bn),
        grid=(m // bm, n // bn),
        in_specs=[pl.BlockSpec((bm, d), lambda i, j: (i, 0)),
                  pl.BlockSpec((d, bn), lambda i, j: (0, j)),
                  pl.BlockSpec((d, LANES), lambda i, j: (0, 0)),
                  pl.BlockSpec((1, bn), lambda i, j: (0, j)),
                  tab_spec, tab_spec, tab_spec, tab_spec],
        out_specs=[pl.BlockSpec((bm, bn), lambda i, j: (i, j)),
                   pl.BlockSpec((bm, LANES), lambda i, j: (i, 0))],
        out_shape=[jax.ShapeDtypeStruct((m, n), BF16), jax.ShapeDtypeStruct((m, LANES), F32)],
        compiler_params=_cparams(("parallel", "arbitrary")),
        name="in_proj_rope",
    )(xn, w_main, w_small, scale_row, *tabs)


def _softmax_terms(q, k_ref, c_lo, c_mask, c_hi, t_rows, mask_fn, bias_fn=None):
    n0, nm, n1 = c_lo * ATT_TK, c_mask * ATT_TK, c_hi * ATT_TK
    s = lax.dot_general(q, k_ref[n0:n1, :], (((1,), (1,)), ((), ())), preferred_element_type=F32)
    if bias_fn is not None:
        s = s + bias_fn(n0, n1)
    kpos = nm + lax.broadcasted_iota(jnp.int32, (1, n1 - nm), 1)
    tail = jnp.where(mask_fn(t_rows, kpos, nm, n1), s[:, nm - n0:], NEG_INF)
    s = jnp.concatenate([s[:, :nm - n0], tail], axis=1) if nm > n0 else tail
    m = jnp.max(s, axis=1, keepdims=True)
    p = jnp.exp(s - m)
    return p, m, jnp.sum(p, axis=1, keepdims=True)


def _attend(q, k_ref, v_ref, c_lo, c_mask, c_hi, t_rows, mask_fn, bias_fn=None):
    p, m, l = _softmax_terms(q, k_ref, c_lo, c_mask, c_hi, t_rows, mask_fn, bias_fn)
    acc = jnp.dot(p.astype(BF16), v_ref[c_lo * ATT_TK:c_hi * ATT_TK, :], preferred_element_type=F32)
    return m, l, acc


def _causal_mask(t_rows, kpos, n0, n1):
    return kpos <= t_rows


def _row_positions(q0, tq, reps):
    r = lax.broadcasted_iota(jnp.int32, (reps * tq, 1), 0)
    return q0 + (r & (tq - 1))


def _per_query_tile(n_tiles, fn, qi=None):
    for c in range(n_tiles):
        if qi is None:
            pl.when(pl.program_id(0) >= 0)(functools.partial(fn, c, slice(c * ATT_TQ, (c + 1) * ATT_TQ)))
        else:
            pl.when(qi == c)(functools.partial(fn, c, slice(0, ATT_TQ)))


def _first_chunk(c, reach):
    return max(c - (reach + ATT_TK - 1) // ATT_TK, 0)


def _diff_kernel(lam_ref, q_ref, k_ref, v_ref, g_ref, o_ref, *, out_scale):
    tq = ATT_TQ

    def tile(c, rows):
        q = q_ref[rows, :]
        lane = lax.broadcasted_iota(jnp.int32, q.shape, 1)
        zero = jnp.zeros_like(q)
        q2 = jnp.concatenate([jnp.where(lane < HEAD_DIM // 2, q, zero),
                              jnp.where(lane >= HEAD_DIM // 2, q, zero)], axis=0)
        t_rows = _row_positions(c * tq, tq, 2)
        p, _, l = _softmax_terms(q2, k_ref, 0, c, c + 1, t_rows, _causal_mask)
        r = 1.0 / l
        diff_map = p[:tq] * r[:tq] - p[tq:] * (lam_ref[0] * r[tq:])
        a = jnp.dot(diff_map.astype(BF16), v_ref[:(c + 1) * ATT_TK, :], preferred_element_type=F32)
        y = a * lax.rsqrt(jnp.mean(a * a, axis=-1, keepdims=True) + EPS) * g_ref[...]
        o_ref[rows, :] = (y * out_scale).astype(o_ref.dtype)

    _per_query_tile(k_ref.shape[0] // tq, tile)


def _head_spec(t, col):
    return pl.BlockSpec((None, t, LANES), lambda bi, h: (bi, 0, col + h))


def _diff_attention(z3, lam, subln_g, lam_init):
    b, t, _ = z3.shape
    return pl.pallas_call(
        functools.partial(_diff_kernel, out_scale=1.0 - lam_init),
        grid=(b, N_HEADS_DIFF),
        in_specs=[pl.BlockSpec(memory_space=pltpu.SMEM),
                  _head_spec(t, COL_QA), _head_spec(t, COL_KA), _head_spec(t, COL_VA),
                  pl.BlockSpec((1, LANES), lambda bi, h: (0, 0))],
        out_specs=_head_spec(t, 0),
        out_shape=jax.ShapeDtypeStruct((b, t, N_HEADS_DIFF * HEAD_DIM), BF16),
        compiler_params=_cparams(("parallel", "parallel")),
        name="diff_attention",
    )(lam.reshape(1), z3, z3, z3, subln_g.reshape(1, LANES))


def _small_prep_kernel(zs_ref, bf_ref, c_ref, ct_ref, g_ref):
    zs = zs_ref[...]
    t = zs.shape[0]
    c = jax.nn.log_sigmoid(zs + bf_ref[...])
    row = lax.broadcasted_iota(jnp.int32, c.shape, 0)
    shift = 1
    while shift < t:
        c = c + jnp.where(row >= shift, pltpu.roll(c, shift, axis=0), 0.0)
        shift *= 2
    c_ref[...] = c
    ct_ref[...] = jnp.transpose(c)[:8, :]
    g_ref[...] = jax.nn.sigmoid(zs)


def _small_prep(zs3, bf_row):
    b, t, _ = zs3.shape
    return pl.pallas_call(
        _small_prep_kernel,
        grid=(b,),
        in_specs=[pl.BlockSpec((None, t, LANES), lambda bi: (bi, 0, 0)),
                  pl.BlockSpec((1, LANES), lambda bi: (0, 0))],
        out_specs=[pl.BlockSpec((None, t, LANES), lambda bi: (bi, 0, 0)),
                   pl.BlockSpec((None, 8, t), lambda bi: (bi, 0, 0)),
                   pl.BlockSpec((None, t, LANES), lambda bi: (bi, 0, 0))],
        out_shape=[jax.ShapeDtypeStruct((b, t, LANES), F32),
                   jax.ShapeDtypeStruct((b, 8, t), F32),
                   jax.ShapeDtypeStruct((b, t, LANES), F32)],
        compiler_params=_cparams(("parallel",)),
        name="forget_cumsum_gates",
    )(zs3, bf_row)


def _fox_kernel(q_ref, k_ref, v_ref, c_ref, ct_ref, o_ref):
    h = pl.program_id(1)
    tq = ATT_TQ

    def tile(c, rows):
        cc = c_ref[rows, :]
        lane = lax.broadcasted_iota(jnp.int32, cc.shape, 1)
        cq = jnp.sum(jnp.where(lane == h, cc, 0.0), axis=1, keepdims=True)

        def bias_fn(n0, n1):
            return cq - ct_ref[pl.ds(h, 1), n0:n1]

        t_rows = _row_positions(c * tq, tq, 1)
        _, l, acc = _attend(q_ref[rows, :], k_ref, v_ref, 0, c, c + 1, t_rows, _causal_mask, bias_fn)
        o_ref[rows, :] = (acc / l).astype(o_ref.dtype)

    _per_query_tile(k_ref.shape[0] // tq, tile)


def _fox_attention(z3, c_col, c_row):
    b, t, _ = z3.shape
    return pl.pallas_call(
        _fox_kernel,
        grid=(b, N_HEADS_FOX),
        in_specs=[_head_spec(t, COL_QB), _head_spec(t, COL_KB), _head_spec(t, COL_VB),
                  pl.BlockSpec((None, t, LANES), lambda bi, h: (bi, 0, 0)),
                  pl.BlockSpec((None, 8, t), lambda bi, h: (bi, 0, 0))],
        out_specs=_head_spec(t, 0),
        out_shape=jax.ShapeDtypeStruct((b, t, N_HEADS_FOX * HEAD_DIM), BF16),
        compiler_params=_cparams(("parallel", "parallel")),
        name="forgetting_attention",
    )(z3, z3, z3, c_col, c_row)


def _compress_one(x_ref, xs_ref, pe_ref, w1_ref, w2_ref, o_ref):
    t = x_ref.shape[0]
    nb = t // CMP_STRIDE
    xs_ref[...] = x_ref[...].astype(F32)
    pe = pe_ref[...]
    lo, hi = [], []
    for i in range(CMP_STRIDE):
        xi = xs_ref[pl.ds(i, nb, stride=CMP_STRIDE), :]
        lo.append((xi + pe[i:i + 1, :]).astype(BF16))
        hi.append((xi + pe[CMP_STRIDE + i:CMP_STRIDE + i + 1, :]).astype(BF16))
    half = CMP_STRIDE * HEAD_DIM
    a = jnp.dot(jnp.concatenate(lo, axis=1), w1_ref[:half, :], preferred_element_type=F32)
    bb = jnp.dot(jnp.concatenate(hi, axis=1), w1_ref[half:, :], preferred_element_type=F32)
    y = a + pltpu.roll(bb, nb - 1, axis=0)
    out = jnp.dot(jax.nn.gelu(y).astype(BF16), w2_ref[...], preferred_element_type=F32)
    row = lax.broadcasted_iota(jnp.int32, out.shape, 0)
    o_ref[...] = jnp.where(row < nb - 1, out, 0.0).astype(o_ref.dtype)


def _compress_kernel(k_ref, v_ref, pek_ref, pev_ref, wk1_ref, wk2_ref, wv1_ref, wv2_ref,
                     kc_ref, vc_ref, xs_ref):
    _compress_one(k_ref, xs_ref, pek_ref, wk1_ref, wk2_ref, kc_ref)
    _compress_one(v_ref, xs_ref, pev_ref, wv1_ref, wv2_ref, vc_ref)


def _nsa_compress(z3, pe_k, pe_v, wk1, wk2, wv1, wv2):
    b, t, _ = z3.shape
    nb = t // CMP_STRIDE
    full = lambda a: pl.BlockSpec(a.shape, lambda bi, g: (0,) * a.ndim)
    return pl.pallas_call(
        _compress_kernel,
        grid=(b, N_KV_NSA),
        in_specs=[pl.BlockSpec((None, t, LANES), lambda bi, g: (bi, 0, COL_KCC + g)),
                  pl.BlockSpec((None, t, LANES), lambda bi, g: (bi, 0, COL_VCC + g)),
                  full(pe_k), full(pe_v), full(wk1), full(wk2), full(wv1), full(wv2)],
        out_specs=[pl.BlockSpec((None, None, nb, LANES), lambda bi, g: (bi, g, 0, 0)),
                   pl.BlockSpec((None, None, nb, LANES), lambda bi, g: (bi, g, 0, 0))],
        out_shape=[jax.ShapeDtypeStruct((b, N_KV_NSA, nb, LANES), BF16),
                   jax.ShapeDtypeStruct((b, N_KV_NSA, nb, LANES), BF16)],
        scratch_shapes=[pltpu.VMEM((t, LANES), F32)],
        compiler_params=_cparams(("parallel", "parallel")),
        name="nsa_compress",
    )(z3, z3, pe_k, pe_v, wk1, wk2, wv1, wv2)


def _nsa_tile(c, rows, g, q_ref, kc_ref, vc_ref, ks_ref, vs_ref, kw_ref, vw_ref, gate_ref, o_ref, seq):
    tq = ATT_TQ
    reps = N_HEADS_NSA // N_KV_NSA
    q0 = c * tq
    q4 = jnp.concatenate([q_ref[rows, r * LANES:(r + 1) * LANES] for r in range(reps)], axis=0)
    t_rows = _row_positions(q0, tq, reps)

    nb = kc_ref.shape[0]
    n_cmp = (seq - CMP_LEN) // CMP_STRIDE + 1
    s = lax.dot_general(q4, kc_ref[...], (((1,), (1,)), ((), ())), preferred_element_type=F32)
    cidx = lax.broadcasted_iota(jnp.int32, (1, nb), 1)
    valid = (cidx * CMP_STRIDE + (CMP_LEN - 1) <= t_rows) & (cidx < n_cmp)
    s = jnp.where(valid, s, NEG_INF)
    e = jnp.where(valid, jnp.exp(s - jnp.max(s, axis=1, keepdims=True)), 0.0)
    den = jnp.sum(e, axis=1, keepdims=True)
    p_cmp = e / jnp.where(den > 0, den, 1.0)
    o_cmp = jnp.dot(p_cmp.astype(BF16), vc_ref[...], preferred_element_type=F32)

    n_sel = seq // SEL_LEN
    n_sel_pad = LANES
    p_sum = p_cmp[:tq]
    for r in range(1, reps):
        p_sum = p_sum + p_cmp[r * tq:(r + 1) * tq]
    p_hi = p_sum.astype(BF16)
    p_lo = (p_sum - p_hi.astype(F32)).astype(BF16)
    srow = lax.broadcasted_iota(jnp.int32, (n_sel_pad, nb), 0)
    ccol = lax.broadcasted_iota(jnp.int32, (n_sel_pad, nb), 1)
    overlap = ((ccol * CMP_STRIDE < srow * SEL_LEN + SEL_LEN) &
               (ccol * CMP_STRIDE + CMP_LEN > srow * SEL_LEN) & (ccol < n_cmp) & (srow < n_sel))
    ov = jnp.where(overlap, 1.0, 0.0).astype(BF16)
    nt = (((1,), (1,)), ((), ()))
    imp_t = (lax.dot_general(ov, p_hi, nt, preferred_element_type=F32) +
             lax.dot_general(ov, p_lo, nt, preferred_element_type=F32))
    n_rank = ((n_sel + 7) // 8) * 8
    imp_t = imp_t[:n_rank]
    sblk = lax.broadcasted_iota(jnp.int32, (n_rank, tq), 0)
    tpos = q0 + lax.broadcasted_iota(jnp.int32, (n_rank, tq), 1)
    cur = tpos // SEL_LEN
    forced = (sblk == 0) | (sblk == cur) | (sblk == cur - 1)
    val = jnp.where(forced, POS_BIG, jnp.where(sblk * SEL_LEN <= tpos, imp_t, NEG_INF))
    val = jnp.where(sblk < n_sel, val, -2.0 * POS_BIG)
    rank = jnp.zeros((n_rank, tq), F32)
    for s2 in range(n_sel):
        other = val[s2:s2 + 1, :]
        tie = jnp.where(sblk > s2, 1.0, 0.0)
        rank = rank + jnp.where(other > val, 1.0, jnp.where(other == val, tie, 0.0))
    n_top = min(SEL_TOPN, n_sel)
    sel_t = jnp.where((rank < n_top) & (sblk < n_sel), 1.0, 0.0)
    if n_rank < n_sel_pad:
        sel_t = jnp.concatenate([sel_t, jnp.zeros((n_sel_pad - n_rank, tq), F32)], axis=0)
    sel = jnp.transpose(sel_t).astype(BF16)

    def sel_mask(t_r, kpos, n0, n1):
        erow = lax.broadcasted_iota(jnp.int32, (n_sel_pad, n1 - n0), 0)
        ecol = n0 + lax.broadcasted_iota(jnp.int32, (n_sel_pad, n1 - n0), 1)
        expand = jnp.where(ecol // SEL_LEN == erow, 1.0, 0.0).astype(BF16)
        chosen = jnp.dot(sel, expand, preferred_element_type=F32)
        chosen = jnp.concatenate([chosen] * reps, axis=0)
        return (chosen > 0.5) & (kpos <= t_r)

    _, l_s, acc_s = _attend(q4, ks_ref, vs_ref, 0, 0, c + 1, t_rows, sel_mask)
    o_sel = acc_s / l_s

    def win_mask(t_r, kpos, n0, n1):
        return (kpos <= t_r) & (kpos > t_r - WIN_LEN)

    lo = _first_chunk(c, WIN_LEN)
    _, l_w, acc_w = _attend(q4, kw_ref, vw_ref, lo, lo, c + 1, t_rows, win_mask)
    o_win = acc_w / l_w

    gates = gate_ref[rows, :]
    lane = lax.broadcasted_iota(jnp.int32, gates.shape, 1)
    n_fb = N_HEADS_FOX
    outs = []
    for r in range(reps):
        col = n_fb + 3 * (g * reps + r)
        rs = slice(r * tq, (r + 1) * tq)
        o_r = jnp.zeros((tq, LANES), F32)
        for br, o_br in enumerate((o_cmp, o_sel, o_win)):
            gcol = jnp.sum(jnp.where(lane == col + br, gates, 0.0), axis=1, keepdims=True)
            o_r = o_r + gcol * o_br[rs]
        outs.append(o_r)
    o_ref[rows, :] = jnp.concatenate(outs, axis=1).astype(o_ref.dtype)


def _nsa_kernel(*refs, seq):
    g = pl.program_id(1)
    _per_query_tile(seq // ATT_TQ, lambda c, rows: _nsa_tile(c, rows, g, *refs, seq), qi=pl.program_id(2))


def _nsa_attention(z3, kc, vc, gates):
    b, t, _ = z3.shape
    tq = ATT_TQ
    reps = N_HEADS_NSA // N_KV_NSA
    nb = kc.shape[2]
    kv = lambda col: pl.BlockSpec((None, t, LANES), lambda bi, g, qi: (bi, 0, col + g))
    return pl.pallas_call(
        functools.partial(_nsa_kernel, seq=t),
        grid=(b, N_KV_NSA, t // tq),
        in_specs=[pl.BlockSpec((None, tq, reps * LANES), lambda bi, g, qi: (bi, qi, COL_QC // reps + g)),
                  pl.BlockSpec((None, None, nb, LANES), lambda bi, g, qi: (bi, g, 0, 0)),
                  pl.BlockSpec((None, None, nb, LANES), lambda bi, g, qi: (bi, g, 0, 0)),
                  kv(COL_KSC), kv(COL_VSC), kv(COL_KWC), kv(COL_VWC),
                  pl.BlockSpec((None, tq, LANES), lambda bi, g, qi: (bi, qi, 0))],
        out_specs=pl.BlockSpec((None, tq, reps * LANES), lambda bi, g, qi: (bi, qi, g)),
        out_shape=jax.ShapeDtypeStruct((b, t, N_HEADS_NSA * HEAD_DIM), BF16),
        compiler_params=_cparams(("parallel", "parallel", "arbitrary")),
        name="nsa_attention",
    )(z3, kc, vc, z3, z3, z3, z3, gates)


def _dil_kernel(*refs):
    n_g = len(DIL_PATTERNS)
    q_refs, k_refs, v_refs = refs[:n_g], refs[n_g:2 * n_g], refs[2 * n_g:3 * n_g]
    o_ref = refs[3 * n_g]
    tq = ATT_TQ

    def tile(c, rows):
        t_rows = _row_positions(c * tq, tq, 1)
        parts = []
        for gi, (w, r) in enumerate(DIL_PATTERNS):
            def mask(t_r, kpos, n0, n1, w=w, r=r):
                dist = t_r - kpos
                return (dist >= 0) & (dist <= w) & ((dist & (r - 1)) == 0)

            lo = _first_chunk(c, w)
            parts.append(_attend(q_refs[gi][rows, :], k_refs[gi], v_refs[gi], lo, lo, c + 1, t_rows, mask))
        lses = [m + jnp.log(l) for (m, l, _) in parts]
        top = functools.reduce(jnp.maximum, lses)
        ws = [jnp.exp(x - top) for x in lses]
        tot = functools.reduce(lambda a, b2: a + b2, ws)
        out = jnp.zeros((tq, LANES), F32)
        for wgt, (m, l, acc) in zip(ws, parts):
            out = out + (wgt / tot) * (acc / l)
        o_ref[rows, :] = out.astype(o_ref.dtype)

    _per_query_tile(k_refs[0].shape[0] // tq, tile)


def _dilated_attention(z3):
    b, t, _ = z3.shape
    n_g = len(DIL_PATTERNS)
    hg = N_HEADS_DIL // n_g
    specs = [_head_spec(t, col + gi * hg) for col in (COL_QD, COL_KD, COL_VD) for gi in range(n_g)]
    return pl.pallas_call(
        _dil_kernel,
        grid=(b, hg),
        in_specs=specs,
        out_specs=_head_spec(t, 0),
        out_shape=jax.ShapeDtypeStruct((b, t, hg * HEAD_DIM), BF16),
        compiler_params=_cparams(("parallel", "parallel")),
        name="dilated_attention",
    )(*([z3] * (3 * n_g)))


def _in_proj_sizes():
    hd = HEAD_DIM
    kv = N_KV_NSA * hd
    return (N_HEADS_DIFF * hd, N_HEADS_DIFF * hd, N_HEADS_DIFF * hd,
            N_HEADS_FOX * hd, N_HEADS_FOX * hd, N_HEADS_FOX * hd, N_HEADS_FOX,
            N_HEADS_NSA * hd, kv, kv, kv, kv, kv, kv, 3 * N_HEADS_NSA,
            N_HEADS_DIL * hd, N_HEADS_DIL * hd, N_HEADS_DIL * hd)


_SEC_NAMES = ("qa", "ka", "va", "qb", "kb", "vb", "fb", "qc", "kcc", "vcc", "ksc", "vsc", "kwc", "vwc", "gc",
              "qd", "kd", "vd")
_SEC_ORDER = ("qa", "ka", "qc", "kcc", "ksc", "kwc", "qd", "kd", "va", "qb", "kb", "vb", "vcc", "vsc", "vwc", "vd")


def _pack_cols_kernel(blk_ref, cls_ref, a_ref, b_ref, o_ref, *, shifts):
    j = pl.program_id(0)
    for ci, s in enumerate(shifts):
        @pl.when(cls_ref[j] == ci)
        def _(s=s):
            a = a_ref[...]
            if s:
                lane = lax.broadcasted_iota(jnp.int32, a.shape, 1)
                a = jnp.where(lane < LANES - s, pltpu.roll(a, LANES - s, axis=1),
                              pltpu.roll(b_ref[...], LANES - s, axis=1))
            o_ref[...] = a.astype(o_ref.dtype)


def _pack_small_kernel(*refs, parts):
    o_ref = refs[-1]
    lane = lax.broadcasted_iota(jnp.int32, o_ref.shape, 1)
    out = jnp.zeros(o_ref.shape, F32)
    for ref, (_, src_lane, dst_lane, width) in zip(refs[:-1], parts):
        x = ref[...]
        if (dst_lane - src_lane) % LANES:
            x = pltpu.roll(x, (dst_lane - src_lane) % LANES, axis=1)
        out = jnp.where((lane >= dst_lane) & (lane < dst_lane + width), x, out)
    o_ref[...] = out.astype(o_ref.dtype)


def _pack_w_in(w_stack, li):
    _, rows, cols = w_stack.shape
    sizes = _in_proj_sizes()
    offs = np.concatenate([[0], np.cumsum(sizes)])
    start = {n: int(offs[i]) for i, n in enumerate(_SEC_NAMES)}
    size = {n: int(sizes[i]) for i, n in enumerate(_SEC_NAMES)}
    src = np.concatenate([start[n] + np.arange(0, size[n], LANES) for n in _SEC_ORDER])
    shifts = tuple(sorted(set(int(s) for s in src % LANES)))
    blk = jnp.asarray(src // LANES, jnp.int32)
    cls = jnp.asarray([shifts.index(int(s)) for s in src % LANES], jnp.int32)
    last = (cols - 1) // LANES
    main = pl.pallas_call(
        functools.partial(_pack_cols_kernel, shifts=shifts),
        grid_spec=pltpu.PrefetchScalarGridSpec(
            num_scalar_prefetch=2,
            grid=(len(src),),
            in_specs=[pl.BlockSpec((None, rows, LANES), lambda j, blk, cls: (li, 0, blk[j])),
                      pl.BlockSpec((None, rows, LANES), lambda j, blk, cls: (li, 0, jnp.minimum(blk[j] + 1, last)))],
            out_specs=pl.BlockSpec((rows, LANES), lambda j, blk, cls: (0, j))),
        out_shape=jax.ShapeDtypeStruct((rows, len(src) * LANES), BF16),
        compiler_params=_cparams(("parallel",)),
        name="w_in_pack",
    )(blk, cls, w_stack, w_stack)
    parts, dest = [], 0
    for n in ("fb", "gc"):
        assert start[n] // LANES == (start[n] + size[n] - 1) // LANES
        parts.append((start[n] // LANES, start[n] % LANES, dest, size[n]))
        dest += size[n]
    small = pl.pallas_call(
        functools.partial(_pack_small_kernel, parts=tuple(parts)),
        grid=(1,),
        in_specs=[pl.BlockSpec((None, rows, LANES), lambda i, b=b: (li, 0, b)) for b, _, _, _ in parts],
        out_specs=pl.BlockSpec((rows, LANES), lambda i: (0, 0)),
        out_shape=jax.ShapeDtypeStruct((rows, LANES), BF16),
        compiler_params=_cparams(("arbitrary",)),
        name="w_in_pack_narrow",
    )(*([w_stack] * len(parts)))
    scales = {"qa": (HEAD_DIM // 2) ** -0.5, "qb": HEAD_DIM ** -0.5, "qc": HEAD_DIM ** -0.5, "qd": HEAD_DIM ** -0.5}
    scale_row = jnp.concatenate([jnp.full((1, size[n]), scales.get(n, 1.0), F32) for n in _SEC_ORDER], axis=1)
    return main, small, scale_row


def _rope_tables(t):
    pos = jnp.arange(t, dtype=F32)[:, None]
    tabs = []
    for d in (HEAD_DIM // 2, HEAD_DIM):
        half = d // 2
        inv = ROPE_THETA ** (-jnp.arange(half, dtype=F32) * 2.0 / d)
        ang = pos * inv[None, :]
        cos, sin = jnp.cos(ang), jnp.sin(ang)
        reps = LANES // d
        tabs.append(jnp.tile(jnp.concatenate([cos, cos], axis=1), (1, reps)))
        tabs.append(jnp.tile(jnp.concatenate([-sin, sin], axis=1), (1, reps)))
    return tabs


def kernel(x, p, ffn1_pre_g, ffn1_w_gate, ffn1_w_up, ffn1_w_down, ffn1_post_g, mix_pre_g, w_in, fox_bf, diff_lam_q1, diff_lam_k1, diff_lam_q2, diff_lam_k2, diff_subln_g, nsa_pe_k, nsa_pe_v, nsa_wk1, nsa_wk2, nsa_wv1, nsa_wv2, w_out, mix_post_g, ffn2_pre_g, ffn2_w_gate, ffn2_w_up, ffn2_w_down, ffn2_post_g, ple_pre_g, ple_w_gate, ple_w_proj, ple_post_g):
    b, t, d = x.shape
    depth = w_in.shape[0]
    m = b * t
    tabs = _rope_tables(t)
    h = x.reshape(m, d)
    xn = _rmsnorm(h, ffn1_pre_g[0])

    def ffn(h, xn, li, wg, wu, wd, g_post, g_next):
        act = _gateup(xn, wg, wu, li)
        return _proj_residual(act, _cast_pad(wd, li, row_mult=FF_PAD), h, g_post, g_next, 0.5)

    for li in range(depth):
        h, xn = ffn(h, xn, li, ffn1_w_gate, ffn1_w_up, ffn1_w_down, ffn1_post_g[li], mix_pre_g[li])

        w_main, w_small, scale_row = _pack_w_in(w_in, li)
        z, zs = _inproj(xn, w_main, w_small, scale_row, tabs, t)
        z3 = z.reshape(b, t, z.shape[1])
        bf_row = jnp.pad(fox_bf[li].astype(F32), (0, LANES - N_HEADS_FOX)).reshape(1, LANES)
        c_col, c_row, gates = _small_prep(zs.reshape(b, t, LANES), bf_row)

        lam_init = 0.8 - 0.6 * math.exp(-0.3 * li)
        lam = (jnp.exp(jnp.sum(diff_lam_q1[li].astype(F32) * diff_lam_k1[li].astype(F32)))
               - jnp.exp(jnp.sum(diff_lam_q2[li].astype(F32) * diff_lam_k2[li].astype(F32))) + lam_init)
        o_a = _diff_attention(z3, lam, diff_subln_g[li].astype(F32), lam_init)
        o_b = _fox_attention(z3, c_col, c_row)
        kc, vc = _nsa_compress(z3, nsa_pe_k[li], nsa_pe_v[li], nsa_wk1[li].astype(BF16), nsa_wk2[li].astype(BF16),
                               nsa_wv1[li].astype(BF16), nsa_wv2[li].astype(BF16))
        o_c = _nsa_attention(z3, kc, vc, gates)
        o_d = _dilated_attention(z3)
        o = jnp.concatenate([o_a, o_b, o_c, o_d], axis=-1).reshape(m, -1)
        h, xn = _proj_residual(o, _cast_pad(w_out, li), h, mix_post_g[li], ffn2_pre_g[li], 1.0)

        h, xn = ffn(h, xn, li, ffn2_w_gate, ffn2_w_up, ffn2_w_down, ffn2_post_g[li], ple_pre_g[li])

        g_next = ffn1_pre_g[li + 1] if li + 1 < depth else None
        h, xn = _proj_residual(xn, _cast_pad(ple_w_gate, li), h, ple_post_g[li], g_next, 1.0,
                               p=p[li].reshape(m, -1), wp=ple_w_proj[li].astype(BF16))
    return h.reshape(b, t, d)
```

```python
import functools
import math

import numpy as np
import jax
import jax.numpy as jnp
from jax import lax
from jax.experimental import pallas as pl
from jax.experimental.pallas import tpu as pltpu

F32 = jnp.float32
BF16 = jnp.bfloat16

HEAD_DIM = 128
N_HEADS_DIFF = 8
N_HEADS_FOX = 7
N_HEADS_NSA = 8
N_KV_NSA = 2
N_HEADS_DIL = 9
ROPE_THETA = 10000.0
EPS = 1e-6
CMP_LEN = 32
CMP_STRIDE = 16
SEL_LEN = 64
SEL_TOPN = 16
WIN_LEN = 512
DIL_PATTERNS = ((128, 1), (512, 4), (2048, 16))
NEG_INF = -1e30
POS_BIG = 1e30

V7X_VMEM_LIMIT_BYTES = 60 * 1024 * 1024
LANES = 128
FF_PAD = 512
ATT_TQ = 256
ATT_TK = 256

COL_QA, COL_KA = 0, 8
COL_QC, COL_KCC, COL_KSC, COL_KWC, COL_QD, COL_KD = 16, 24, 26, 28, 30, 39
COL_VA, COL_QB, COL_KB, COL_VB, COL_VCC, COL_VSC, COL_VWC, COL_VD = 48, 56, 63, 70, 77, 79, 81, 83
N_COLBLK = 92
N_ROPE64_BLK = 16
N_ROPE128_BLK = 32
IN_TN = 512


def _cparams(sem):
    return pltpu.CompilerParams(dimension_semantics=sem, vmem_limit_bytes=V7X_VMEM_LIMIT_BYTES)


def _pick(n, prefs):
    for p in prefs:
        if n % p == 0:
            return p
    return n


def _rmsnorm_kernel(x_ref, g_ref, o_ref):
    x = x_ref[...]
    ms = jnp.mean(x * x, axis=-1, keepdims=True)
    o_ref[...] = (x * lax.rsqrt(ms + EPS) * g_ref[...]).astype(o_ref.dtype)


def _rmsnorm(x, g):
    m, d = x.shape
    bm = _pick(m, (256, 128, 64, 32, 16, 8))
    return pl.pallas_call(
        _rmsnorm_kernel,
        grid=(m // bm,),
        in_specs=[pl.BlockSpec((bm, d), lambda i: (i, 0)), pl.BlockSpec((1, d), lambda i: (0, 0))],
        out_specs=pl.BlockSpec((bm, d), lambda i: (i, 0)),
        out_shape=jax.ShapeDtypeStruct((m, d), BF16),
        compiler_params=_cparams(("parallel",)),
        name="rmsnorm",
    )(x, g.reshape(1, d))


def _gateup_kernel(x_ref, wg_ref, wu_ref, o_ref, *, f_valid):
    x = x_ref[...]
    g = jnp.dot(x, wg_ref[...].astype(BF16), preferred_element_type=F32)
    u = jnp.dot(x, wu_ref[...].astype(BF16), preferred_element_type=F32)
    y = g * jax.nn.sigmoid(g) * u
    bf = y.shape[1]
    if f_valid % bf:
        col = pl.program_id(1) * bf + lax.broadcasted_iota(jnp.int32, (1, bf), 1)
        y = jnp.where(col < f_valid, y, 0.0)
    o_ref[...] = y.astype(o_ref.dtype)


def _gateup(xn, wg_stack, wu_stack, li):
    m, d = xn.shape
    f = wg_stack.shape[2]
    f_pad = -(-f // FF_PAD) * FF_PAD
    bm = _pick(m, (1024, 512, 256, 128))
    bf = _pick(f_pad, (512, 256, 128))
    w_spec = pl.BlockSpec((None, d, bf), lambda i, j: (li, 0, j))
    return pl.pallas_call(
        functools.partial(_gateup_kernel, f_valid=f),
        grid=(m // bm, f_pad // bf),
        in_specs=[pl.BlockSpec((bm, d), lambda i, j: (i, 0), pipeline_mode=pl.Buffered(1)), w_spec, w_spec],
        out_specs=pl.BlockSpec((bm, bf), lambda i, j: (i, j)),
        out_shape=jax.ShapeDtypeStruct((m, f_pad), BF16),
        compiler_params=_cparams(("parallel", "arbitrary")),
        name="ffn_gateup",
    )(xn, wg_stack, wu_stack)


EPI_ROWS = 32


def _residual_epilogue(o_ref, h_ref, gpost_ref, gnext_ref, xn_ref):
    bm = o_ref.shape[0]
    rows = min(EPI_ROWS, bm)

    def body(r, carry):
        sl = pl.ds(pl.multiple_of(r * rows, rows), rows)
        f = o_ref[sl, :]
        hn = h_ref[sl, :] + f * lax.rsqrt(jnp.mean(f * f, axis=-1, keepdims=True) + EPS) * gpost_ref[...]
        o_ref[sl, :] = hn
        if xn_ref is not None:
            ms = jnp.mean(hn * hn, axis=-1, keepdims=True)
            xn_ref[sl, :] = (hn * lax.rsqrt(ms + EPS) * gnext_ref[...]).astype(xn_ref.dtype)
        return carry

    lax.fori_loop(0, bm // rows, body, 0, unroll=2)


def _proj_k_kernel(a_ref, w_ref, h_ref, gpost_ref, gnext_ref, o_ref, *rest, nk):
    k = pl.program_id(1)

    @pl.when(k == 0)
    def _():
        o_ref[...] = jnp.dot(a_ref[...], w_ref[...], preferred_element_type=F32)

    @pl.when(k > 0)
    def _():
        o_ref[...] += jnp.dot(a_ref[...], w_ref[...], preferred_element_type=F32)

    @pl.when(k == nk - 1)
    def _():
        _residual_epilogue(o_ref, h_ref, gpost_ref, gnext_ref, rest[0] if rest else None)


def _proj_n_kernel(*refs, ple, nj, emit_next):
    a_ref, w_ref, h_ref, gpost_ref, gnext_ref = refs[:5]
    refs = refs[5:]
    if ple:
        p_ref, wp_ref = refs[:2]
        refs = refs[2:]
    o_ref = refs[0]
    xn_ref = refs[1] if emit_next else None
    j = pl.program_id(1)
    bn = w_ref.shape[1]
    f = jnp.dot(a_ref[...], w_ref[...], preferred_element_type=F32)
    if ple:
        f = jax.nn.sigmoid(f) * jnp.dot(p_ref[...].astype(BF16), wp_ref[...], preferred_element_type=F32)
    o_ref[:, pl.ds(pl.multiple_of(j * bn, bn), bn)] = f

    @pl.when(j == nj - 1)
    def _():
        _residual_epilogue(o_ref, h_ref, gpost_ref, gnext_ref, xn_ref)


PROJ_FULL_K_MAX = 4096


def _proj_residual(a, w, h, g_post, g_next, coef, p=None, wp=None):
    m, kdim = a.shape
    d = w.shape[1]
    ple = p is not None
    emit_next = g_next is not None
    full_k = kdim <= PROJ_FULL_K_MAX
    bm = _pick(m, (512, 256, 128)) if full_k else _pick(m, (1024, 512, 256, 128))
    row_mode = None if full_k else pl.Buffered(1)
    row_specs = [pl.BlockSpec((bm, d), lambda i, k: (i, 0), pipeline_mode=row_mode),
                 pl.BlockSpec((1, d), lambda i, k: (0, 0)),
                 pl.BlockSpec((1, d), lambda i, k: (0, 0))]
    row_args = [h, (coef * g_post).reshape(1, d), (g_next if emit_next else g_post).reshape(1, d)]
    out_specs = [pl.BlockSpec((bm, d), lambda i, k: (i, 0), pipeline_mode=row_mode)]
    out_shape = [jax.ShapeDtypeStruct((m, d), F32)]
    if emit_next:
        out_specs.append(pl.BlockSpec((bm, d), lambda i, k: (i, 0), pipeline_mode=pl.Buffered(1)))
        out_shape.append(jax.ShapeDtypeStruct((m, d), BF16))
    if full_k:
        bn = _pick(d, (512, 256, 128))
        nj = d // bn
        in_specs = [pl.BlockSpec((bm, kdim), lambda i, j: (i, 0)),
                    pl.BlockSpec((kdim, bn), lambda i, j: (0, j))] + row_specs
        args = [a, w] + row_args
        if ple:
            in_specs += [pl.BlockSpec((bm, p.shape[1]), lambda i, j: (i, 0)),
                         pl.BlockSpec((wp.shape[0], bn), lambda i, j: (0, j))]
            args += [p, wp]
        body = functools.partial(_proj_n_kernel, ple=ple, nj=nj, emit_next=emit_next)
        grid = (m // bm, nj)
    else:
        assert not ple
        bk = _pick(kdim, (512, 256, 128))
        nk = kdim // bk
        in_specs = [pl.BlockSpec((bm, bk), lambda i, k: (i, k)),
                    pl.BlockSpec((bk, d), lambda i, k: (k, 0))] + row_specs
        args = [a, w] + row_args
        body = functools.partial(_proj_k_kernel, nk=nk)
        grid = (m // bm, nk)
    res = pl.pallas_call(
        body,
        grid=grid,
        in_specs=in_specs,
        out_specs=out_specs,
        out_shape=out_shape,
        compiler_params=_cparams(("parallel", "arbitrary")),
        name="ple_residual" if ple else "proj_residual",
    )(*args)
    return (res[0], res[1]) if emit_next else (res[0], None)


def _cast_pad_kernel(x_ref, o_ref, *, rows, cols, masked):
    x = x_ref[...]
    if masked:
        br, bc = x.shape
        r = pl.program_id(0) * br + lax.broadcasted_iota(jnp.int32, x.shape, 0)
        c = pl.program_id(1) * bc + lax.broadcasted_iota(jnp.int32, x.shape, 1)
        x = jnp.where((r < rows) & (c < cols), x, 0.0)
    o_ref[...] = x.astype(o_ref.dtype)


def _cast_pad(w_stack, li, row_mult=1, col_mult=1):
    _, rows, cols = w_stack.shape
    rows_p = -(-rows // row_mult) * row_mult
    cols_p = -(-cols // col_mult) * col_mult
    br = _pick(rows_p, (512, 256, 128))
    bc = _pick(cols_p, (1024, 512, 256, 128))
    last_i, last_j = (rows - 1) // br, (cols - 1) // bc
    return pl.pallas_call(
        functools.partial(_cast_pad_kernel, rows=rows, cols=cols, masked=(rows_p, cols_p) != (rows, cols)),
        grid=(rows_p // br, cols_p // bc),
        in_specs=[pl.BlockSpec((None, br, bc), lambda i, j: (li, jnp.minimum(i, last_i), jnp.minimum(j, last_j)))],
        out_specs=pl.BlockSpec((br, bc), lambda i, j: (i, j)),
        out_shape=jax.ShapeDtypeStruct((rows_p, cols_p), BF16),
        compiler_params=_cparams(("parallel", "parallel")),
        name="weight_cast_pad",
    )(w_stack)


def _rope_tile(acc, cos_ref, sin_ref, half, n_slabs):
    cos = cos_ref[...]
    sin = sin_ref[...]
    outs = []
    for s in range(n_slabs):
        a = acc[:, s * LANES:(s + 1) * LANES]
        if half == LANES // 2:
            partner = pltpu.roll(a, LANES // 2, axis=1)
        else:
            lane = lax.broadcasted_iota(jnp.int32, a.shape, 1)
            fwd = pltpu.roll(a, LANES - half, axis=1)
            bwd = pltpu.roll(a, half, axis=1)
            partner = jnp.where((lane & (2 * half - 1)) < half, fwd, bwd)
        outs.append(a * cos + partner * sin)
    return jnp.concatenate(outs, axis=1)


def _inproj_kernel(x_ref, w_ref, ws_ref, scale_ref, c64_ref, s64_ref, c128_ref, s128_ref,
                   o_ref, os_ref, *, n64, n128):
    j = pl.program_id(1)
    n_slabs = w_ref.shape[1] // LANES

    def product():
        return jnp.dot(x_ref[...], w_ref[...], preferred_element_type=F32)

    @pl.when(j == 0)
    def _():
        os_ref[...] = jnp.dot(x_ref[...], ws_ref[...], preferred_element_type=F32)

    @pl.when(j < n64)
    def _():
        o_ref[...] = (_rope_tile(product(), c64_ref, s64_ref, 32, n_slabs) * scale_ref[...]).astype(o_ref.dtype)

    @pl.when((j >= n64) & (j < n64 + n128))
    def _():
        o_ref[...] = (_rope_tile(product(), c128_ref, s128_ref, 64, n_slabs) * scale_ref[...]).astype(o_ref.dtype)

    @pl.when(j >= n64 + n128)
    def _():
        o_ref[...] = (product() * scale_ref[...]).astype(o_ref.dtype)


def _inproj(xn, w_main, w_small, scale_row, tabs, seq):
    m, d = xn.shape
    n = w_main.shape[1]
    bm = _pick(seq, (1024, 512, 256))
    bn = IN_TN
    tpb = seq // bm
    tab_spec = pl.BlockSpec((bm, LANES), lambda i, j: (i % tpb, 0))
    return pl.pallas_call(
        functools.partial(_inproj_kernel, n64=N_ROPE64_BLK * LANES // bn, n128=N_ROPE128_BLK * LANES // bn),
        grid=(m // bm, n // bn),
        in_specs=[pl.BlockSpec((bm, d), lambda i, j: (i, 0)),
                  pl.BlockSpec((d, bn), lambda i, j: (0, j)),
                  pl.BlockSpec((d, LANES), lambda i, j: (0, 0)),
                  pl.BlockSpec((1, bn), lambda i, j: (0, j)),
                  tab_spec, tab_spec, tab_spec, tab_spec],
        out_specs=[pl.BlockSpec((bm, bn), lambda i, j: (i, j)),
                   pl.BlockSpec((bm, LANES), lambda i, j: (i, 0))],
        out_shape=[jax.ShapeDtypeStruct((m, n), BF16), jax.ShapeDtypeStruct((m, LANES), F32)],
        compiler_params=_cparams(("parallel", "arbitrary")),
        name="in_proj_rope",
    )(xn, w_main, w_small, scale_row, *tabs)


def _softmax_terms(q, k_ref, c_lo, c_mask, c_hi, t_rows, mask_fn, bias_fn=None):
    n0, nm, n1 = c_lo * ATT_TK, c_mask * ATT_TK, c_hi * ATT_TK
    s = lax.dot_general(q, k_ref[n0:n1, :], (((1,), (1,)), ((), ())), preferred_element_type=F32)
    if bias_fn is not None:
        s = s + bias_fn(n0, n1)
    kpos = nm + lax.broadcasted_iota(jnp.int32, (1, n1 - nm), 1)
    tail = jnp.where(mask_fn(t_rows, kpos, nm, n1), s[:, nm - n0:], NEG_INF)
    s = jnp.concatenate([s[:, :nm - n0], tail], axis=1) if nm > n0 else tail
    m = jnp.max(s, axis=1, keepdims=True)
    p = jnp.exp(s - m)
    return p, m, jnp.sum(p, axis=1, keepdims=True)


def _attend(q, k_ref, v_ref, c_lo, c_mask, c_hi, t_rows, mask_fn, bias_fn=None):
    p, m, l = _softmax_terms(q, k_ref, c_lo, c_mask, c_hi, t_rows, mask_fn, bias_fn)
    acc = jnp.dot(p.astype(BF16), v_ref[c_lo * ATT_TK:c_hi * ATT_TK, :], preferred_element_type=F32)
    return m, l, acc


def _causal_mask(t_rows, kpos, n0, n1):
    return kpos <= t_rows


def _row_positions(q0, tq, reps):
    r = lax.broadcasted_iota(jnp.int32, (reps * tq, 1), 0)
    return q0 + (r & (tq - 1))


def _per_query_tile(n_tiles, fn, qi=None):
    for c in range(n_tiles):
        if qi is None:
            pl.when(pl.program_id(0) >= 0)(functools.partial(fn, c, slice(c * ATT_TQ, (c + 1) * ATT_TQ)))
        else:
            pl.when(qi == c)(functools.partial(fn, c, slice(0, ATT_TQ)))


def _first_chunk(c, reach):
    return max(c - (reach + ATT_TK - 1) // ATT_TK, 0)


def _diff_kernel(lam_ref, q_ref, k_ref, v_ref, g_ref, o_ref, *, out_scale):
    tq = ATT_TQ

    def tile(c, rows):
        q = q_ref[rows, :]
        lane = lax.broadcasted_iota(jnp.int32, q.shape, 1)
        zero = jnp.zeros_like(q)
        q2 = jnp.concatenate([jnp.where(lane < HEAD_DIM // 2, q, zero),
                              jnp.where(lane >= HEAD_DIM // 2, q, zero)], axis=0)
        t_rows = _row_positions(c * tq, tq, 2)
        p, _, l = _softmax_terms(q2, k_ref, 0, c, c + 1, t_rows, _causal_mask)
        r = 1.0 / l
        diff_map = p[:tq] * r[:tq] - p[tq:] * (lam_ref[0] * r[tq:])
        a = jnp.dot(diff_map.astype(BF16), v_ref[:(c + 1) * ATT_TK, :], preferred_element_type=F32)
        y = a * lax.rsqrt(jnp.mean(a * a, axis=-1, keepdims=True) + EPS) * g_ref[...]
        o_ref[rows, :] = (y * out_scale).astype(o_ref.dtype)

    _per_query_tile(k_ref.shape[0] // tq, tile)


def _head_spec(t, col):
    return pl.BlockSpec((None, t, LANES), lambda bi, h: (bi, 0, col + h))


def _diff_attention(z3, lam, subln_g, lam_init):
    b, t, _ = z3.shape
    return pl.pallas_call(
        functools.partial(_diff_kernel, out_scale=1.0 - lam_init),
        grid=(b, N_HEADS_DIFF),
        in_specs=[pl.BlockSpec(memory_space=pltpu.SMEM),
                  _head_spec(t, COL_QA), _head_spec(t, COL_KA), _head_spec(t, COL_VA),
                  pl.BlockSpec((1, LANES), lambda bi, h: (0, 0))],
        out_specs=_head_spec(t, 0),
        out_shape=jax.ShapeDtypeStruct((b, t, N_HEADS_DIFF * HEAD_DIM), BF16),
        compiler_params=_cparams(("parallel", "parallel")),
        name="diff_attention",
    )(lam.reshape(1), z3, z3, z3, subln_g.reshape(1, LANES))


def _small_prep_kernel(zs_ref, bf_ref, c_ref, ct_ref, g_ref):
    zs = zs_ref[...]
    t = zs.shape[0]
    c = jax.nn.log_sigmoid(zs + bf_ref[...])
    row = lax.broadcasted_iota(jnp.int32, c.shape, 0)
    shift = 1
    while shift < t:
        c = c + jnp.where(row >= shift, pltpu.roll(c, shift, axis=0), 0.0)
        shift *= 2
    c_ref[...] = c
    ct_ref[...] = jnp.transpose(c)[:8, :]
    g_ref[...] = jax.nn.sigmoid(zs)


def _small_prep(zs3, bf_row):
    b, t, _ = zs3.shape
    return pl.pallas_call(
        _small_prep_kernel,
        grid=(b,),
        in_specs=[pl.BlockSpec((None, t, LANES), lambda bi: (bi, 0, 0)),
                  pl.BlockSpec((1, LANES), lambda bi: (0, 0))],
        out_specs=[pl.BlockSpec((None, t, LANES), lambda bi: (bi, 0, 0)),
                   pl.BlockSpec((None, 8, t), lambda bi: (bi, 0, 0)),
                   pl.BlockSpec((None, t, LANES), lambda bi: (bi, 0, 0))],
        out_shape=[jax.ShapeDtypeStruct((b, t, LANES), F32),
                   jax.ShapeDtypeStruct((b, 8, t), F32),
                   jax.ShapeDtypeStruct((b, t, LANES), F32)],
        compiler_params=_cparams(("parallel",)),
        name="forget_cumsum_gates",
    )(zs3, bf_row)


def _fox_kernel(q_ref, k_ref, v_ref, c_ref, ct_ref, o_ref):
    h = pl.program_id(1)
    tq = ATT_TQ

    def tile(c, rows):
        cc = c_ref[rows, :]
        lane = lax.broadcasted_iota(jnp.int32, cc.shape, 1)
        cq = jnp.sum(jnp.where(lane == h, cc, 0.0), axis=1, keepdims=True)

        def bias_fn(n0, n1):
            return cq - ct_ref[pl.ds(h, 1), n0:n1]

        t_rows = _row_positions(c * tq, tq, 1)
        _, l, acc = _attend(q_ref[rows, :], k_ref, v_ref, 0, c, c + 1, t_rows, _causal_mask, bias_fn)
        o_ref[rows, :] = (acc / l).astype(o_ref.dtype)

    _per_query_tile(k_ref.shape[0] // tq, tile)


def _fox_attention(z3, c_col, c_row):
    b, t, _ = z3.shape
    return pl.pallas_call(
        _fox_kernel,
        grid=(b, N_HEADS_FOX),
        in_specs=[_head_spec(t, COL_QB), _head_spec(t, COL_KB), _head_spec(t, COL_VB),
                  pl.BlockSpec((None, t, LANES), lambda bi, h: (bi, 0, 0)),
                  pl.BlockSpec((None, 8, t), lambda bi, h: (bi, 0, 0))],
        out_specs=_head_spec(t, 0),
        out_shape=jax.ShapeDtypeStruct((b, t, N_HEADS_FOX * HEAD_DIM), BF16),
        compiler_params=_cparams(("parallel", "parallel")),
        name="forgetting_attention",
    )(z3, z3, z3, c_col, c_row)


def _compress_one(x_ref, xs_ref, pe_ref, w1_ref, w2_ref, o_ref):
    t = x_ref.shape[0]
    nb = t // CMP_STRIDE
    xs_ref[...] = x_ref[...].astype(F32)
    pe = pe_ref[...]
    lo, hi = [], []
    for i in range(CMP_STRIDE):
        xi = xs_ref[pl.ds(i, nb, stride=CMP_STRIDE), :]
        lo.append((xi + pe[i:i + 1, :]).astype(BF16))
        hi.append((xi + pe[CMP_STRIDE + i:CMP_STRIDE + i + 1, :]).astype(BF16))
    half = CMP_STRIDE * HEAD_DIM
    a = jnp.dot(jnp.concatenate(lo, axis=1), w1_ref[:half, :], preferred_element_type=F32)
    bb = jnp.dot(jnp.concatenate(hi, axis=1), w1_ref[half:, :], preferred_element_type=F32)
    y = a + pltpu.roll(bb, nb - 1, axis=0)
    out = jnp.dot(jax.nn.gelu(y).astype(BF16), w2_ref[...], preferred_element_type=F32)
    row = lax.broadcasted_iota(jnp.int32, out.shape, 0)
    o_ref[...] = jnp.where(row < nb - 1, out, 0.0).astype(o_ref.dtype)


def _compress_kernel(k_ref, v_ref, pek_ref, pev_ref, wk1_ref, wk2_ref, wv1_ref, wv2_ref,
                     kc_ref, vc_ref, xs_ref):
    _compress_one(k_ref, xs_ref, pek_ref, wk1_ref, wk2_ref, kc_ref)
    _compress_one(v_ref, xs_ref, pev_ref, wv1_ref, wv2_ref, vc_ref)


def _nsa_compress(z3, pe_k, pe_v, wk1, wk2, wv1, wv2):
    b, t, _ = z3.shape
    nb = t // CMP_STRIDE
    full = lambda a: pl.BlockSpec(a.shape, lambda bi, g: (0,) * a.ndim)
    return pl.pallas_call(
        _compress_kernel,
        grid=(b, N_KV_NSA),
        in_specs=[pl.BlockSpec((None, t, LANES), lambda bi, g: (bi, 0, COL_KCC + g)),
                  pl.BlockSpec((None, t, LANES), lambda bi, g: (bi, 0, COL_VCC + g)),
                  full(pe_k), full(pe_v), full(wk1), full(wk2), full(wv1), full(wv2)],
        out_specs=[pl.BlockSpec((None, None, nb, LANES), lambda bi, g: (bi, g, 0, 0)),
                   pl.BlockSpec((None, None, nb, LANES), lambda bi, g: (bi, g, 0, 0))],
        out_shape=[jax.ShapeDtypeStruct((b, N_KV_NSA, nb, LANES), BF16),
                   jax.ShapeDtypeStruct((b, N_KV_NSA, nb, LANES), BF16)],
        scratch_shapes=[pltpu.VMEM((t, LANES), F32)],
        compiler_params=_cparams(("parallel", "parallel")),
        name="nsa_compress",
    )(z3, z3, pe_k, pe_v, wk1, wk2, wv1, wv2)


def _nsa_tile(c, rows, g, q_ref, kc_ref, vc_ref, ks_ref, vs_ref, kw_ref, vw_ref, gate_ref, o_ref, seq):
    tq = ATT_TQ
    reps = N_HEADS_NSA // N_KV_NSA
    q0 = c * tq
    q4 = jnp.concatenate([q_ref[rows, r * LANES:(r + 1) * LANES] for r in range(reps)], axis=0)
    t_rows = _row_positions(q0, tq, reps)

    nb = kc_ref.shape[0]
    n_cmp = (seq - CMP_LEN) // CMP_STRIDE + 1
    s = lax.dot_general(q4, kc_ref[...], (((1,), (1,)), ((), ())), preferred_element_type=F32)
    cidx = lax.broadcasted_iota(jnp.int32, (1, nb), 1)
    valid = (cidx * CMP_STRIDE + (CMP_LEN - 1) <= t_rows) & (cidx < n_cmp)
    s = jnp.where(valid, s, NEG_INF)
    e = jnp.where(valid, jnp.exp(s - jnp.max(s, axis=1, keepdims=True)), 0.0)
    den = jnp.sum(e, axis=1, keepdims=True)
    p_cmp = e / jnp.where(den > 0, den, 1.0)
    o_cmp = jnp.dot(p_cmp.astype(BF16), vc_ref[...], preferred_element_type=F32)

    n_sel = seq // SEL_LEN
    n_sel_pad = LANES
    p_sum = p_cmp[:tq]
    for r in range(1, reps):
        p_sum = p_sum + p_cmp[r * tq:(r + 1) * tq]
    p_hi = p_sum.astype(BF16)
    p_lo = (p_sum - p_hi.astype(F32)).astype(BF16)
    srow = lax.broadcasted_iota(jnp.int32, (n_sel_pad, nb), 0)
    ccol = lax.broadcasted_iota(jnp.int32, (n_sel_pad, nb), 1)
    overlap = ((ccol * CMP_STRIDE < srow * SEL_LEN + SEL_LEN) &
               (ccol * CMP_STRIDE + CMP_LEN > srow * SEL_LEN) & (ccol < n_cmp) & (srow < n_sel))
    ov = jnp.where(overlap, 1.0, 0.0).astype(BF16)
    nt = (((1,), (1,)), ((), ()))
    imp_t = (lax.dot_general(ov, p_hi, nt, preferred_element_type=F32) +
             lax.dot_general(ov, p_lo, nt, preferred_element_type=F32))
    n_rank = ((n_sel + 7) // 8) * 8
    imp_t = imp_t[:n_rank]
    sblk = lax.broadcasted_iota(jnp.int32, (n_rank, tq), 0)
    tpos = q0 + lax.broadcasted_iota(jnp.int32, (n_rank, tq), 1)
    cur = tpos // SEL_LEN
    forced = (sblk == 0) | (sblk == cur) | (sblk == cur - 1)
    val = jnp.where(forced, POS_BIG, jnp.where(sblk * SEL_LEN <= tpos, imp_t, NEG_INF))
    val = jnp.where(sblk < n_sel, val, -2.0 * POS_BIG)
    rank = jnp.zeros((n_rank, tq), F32)
    for s2 in range(n_sel):
        other = val[s2:s2 + 1, :]
        tie = jnp.where(sblk > s2, 1.0, 0.0)
        rank = rank + jnp.where(other > val, 1.0, jnp.where(other == val, tie, 0.0))
    n_top = min(SEL_TOPN, n_sel)
    sel_t = jnp.where((rank < n_top) & (sblk < n_sel), 1.0, 0.0)
    if n_rank < n_sel_pad:
        sel_t = jnp.concatenate([sel_t, jnp.zeros((n_sel_pad - n_rank, tq), F32)], axis=0)
    sel = jnp.transpose(sel_t).astype(BF16)

    def sel_mask(t_r, kpos, n0, n1):
        erow = lax.broadcasted_iota(jnp.int32, (n_sel_pad, n1 - n0), 0)
        ecol = n0 + lax.broadcasted_iota(jnp.int32, (n_sel_pad, n1 - n0), 1)
        expand = jnp.where(ecol // SEL_LEN == erow, 1.0, 0.0).astype(BF16)
        chosen = jnp.dot(sel, expand, preferred_element_type=F32)
        chosen = jnp.concatenate([chosen] * reps, axis=0)
        return (chosen > 0.5) & (kpos <= t_r)

    _, l_s, acc_s = _attend(q4, ks_ref, vs_ref, 0, 0, c + 1, t_rows, sel_mask)
    o_sel = acc_s / l_s

    def win_mask(t_r, kpos, n0, n1):
        return (kpos <= t_r) & (kpos > t_r - WIN_LEN)

    lo = _first_chunk(c, WIN_LEN)
    _, l_w, acc_w = _attend(q4, kw_ref, vw_ref, lo, lo, c + 1, t_rows, win_mask)
    o_win = acc_w / l_w

    gates = gate_ref[rows, :]
    lane = lax.broadcasted_iota(jnp.int32, gates.shape, 1)
    n_fb = N_HEADS_FOX
    outs = []
    for r in range(reps):
        col = n_fb + 3 * (g * reps + r)
        rs = slice(r * tq, (r + 1) * tq)
        o_r = jnp.zeros((tq, LANES), F32)
        for br, o_br in enumerate((o_cmp, o_sel, o_win)):
            gcol = jnp.sum(jnp.where(lane == col + br, gates, 0.0), axis=1, keepdims=True)
            o_r = o_r + gcol * o_br[rs]
        outs.append(o_r)
    o_ref[rows, :] = jnp.concatenate(outs, axis=1).astype(o_ref.dtype)


def _nsa_kernel(*refs, seq):
    g = pl.program_id(1)
    _per_query_tile(seq // ATT_TQ, lambda c, rows: _nsa_tile(c, rows, g, *refs, seq), qi=pl.program_id(2))


def _nsa_attention(z3, kc, vc, gates):
    b, t, _ = z3.shape
    tq = ATT_TQ
    reps = N_HEADS_NSA // N_KV_NSA
    nb = kc.shape[2]
    kv = lambda col: pl.BlockSpec((None, t, LANES), lambda bi, g, qi: (bi, 0, col + g))
    return pl.pallas_call(
        functools.partial(_nsa_kernel, seq=t),
        grid=(b, N_KV_NSA, t // tq),
        in_specs=[pl.BlockSpec((None, tq, reps * LANES), lambda bi, g, qi: (bi, qi, COL_QC // reps + g)),
                  pl.BlockSpec((None, None, nb, LANES), lambda bi, g, qi: (bi, g, 0, 0)),
                  pl.BlockSpec((None, None, nb, LANES), lambda bi, g, qi: (bi, g, 0, 0)),
                  kv(COL_KSC), kv(COL_VSC), kv(COL_KWC), kv(COL_VWC),
                  pl.BlockSpec((None, tq, LANES), lambda bi, g, qi: (bi, qi, 0))],
        out_specs=pl.BlockSpec((None, tq, reps * LANES), lambda bi, g, qi: (bi, qi, g)),
        out_shape=jax.ShapeDtypeStruct((b, t, N_HEADS_NSA * HEAD_DIM), BF16),
        compiler_params=_cparams(("parallel", "parallel", "arbitrary")),
        name="nsa_attention",
    )(z3, kc, vc, z3, z3, z3, z3, gates)


def _dil_kernel(*refs):
    n_g = len(DIL_PATTERNS)
    q_refs, k_refs, v_refs = refs[:n_g], refs[n_g:2 * n_g], refs[2 * n_g:3 * n_g]
    o_ref = refs[3 * n_g]
    tq = ATT_TQ

    def tile(c, rows):
        t_rows = _row_positions(c * tq, tq, 1)
        parts = []
        for gi, (w, r) in enumerate(DIL_PATTERNS):
            def mask(t_r, kpos, n0, n1, w=w, r=r):
                dist = t_r - kpos
                return (dist >= 0) & (dist <= w) & ((dist & (r - 1)) == 0)

            lo = _first_chunk(c, w)
            parts.append(_attend(q_refs[gi][rows, :], k_refs[gi], v_refs[gi], lo, lo, c + 1, t_rows, mask))
        lses = [m + jnp.log(l) for (m, l, _) in parts]
        top = functools.reduce(jnp.maximum, lses)
        ws = [jnp.exp(x - top) for x in lses]
        tot = functools.reduce(lambda a, b2: a + b2, ws)
        out = jnp.zeros((tq, LANES), F32)
        for wgt, (m, l, acc) in zip(ws, parts):
            out = out + (wgt / tot) * (acc / l)
        o_ref[rows, :] = out.astype(o_ref.dtype)

    _per_query_tile(k_refs[0].shape[0] // tq, tile)


def _dilated_attention(z3):
    b, t, _ = z3.shape
    n_g = len(DIL_PATTERNS)
    hg = N_HEADS_DIL // n_g
    specs = [_head_spec(t, col + gi * hg) for col in (COL_QD, COL_KD, COL_VD) for gi in range(n_g)]
    return pl.pallas_call(
        _dil_kernel,
        grid=(b, hg),
        in_specs=specs,
        out_specs=_head_spec(t, 0),
        out_shape=jax.ShapeDtypeStruct((b, t, hg * HEAD_DIM), BF16),
        compiler_params=_cparams(("parallel", "parallel")),
        name="dilated_attention",
    )(*([z3] * (3 * n_g)))


def _in_proj_sizes():
    hd = HEAD_DIM
    kv = N_KV_NSA * hd
    return (N_HEADS_DIFF * hd, N_HEADS_DIFF * hd, N_HEADS_DIFF * hd,
            N_HEADS_FOX * hd, N_HEADS_FOX * hd, N_HEADS_FOX * hd, N_HEADS_FOX,
            N_HEADS_NSA * hd, kv, kv, kv, kv, kv, kv, 3 * N_HEADS_NSA,
            N_HEADS_DIL * hd, N_HEADS_DIL * hd, N_HEADS_DIL * hd)


_SEC_NAMES = ("qa", "ka", "va", "qb", "kb", "vb", "fb", "qc", "kcc", "vcc", "ksc", "vsc", "kwc", "vwc", "gc",
              "qd", "kd", "vd")
_SEC_ORDER = ("qa", "ka", "qc", "kcc", "ksc", "kwc", "qd", "kd", "va", "qb", "kb", "vb", "vcc", "vsc", "vwc", "vd")


def _pack_cols_kernel(blk_ref, cls_ref, a_ref, b_ref, o_ref, *, shifts):
    j = pl.program_id(0)
    for ci, s in enumerate(shifts):
        @pl.when(cls_ref[j] == ci)
        def _(s=s):
            a = a_ref[...]
            if s:
                lane = lax.broadcasted_iota(jnp.int32, a.shape, 1)
                a = jnp.where(lane < LANES - s, pltpu.roll(a, LANES - s, axis=1),
                              pltpu.roll(b_ref[...], LANES - s, axis=1))
            o_ref[...] = a.astype(o_ref.dtype)


def _pack_small_kernel(*refs, parts):
    o_ref = refs[-1]
    lane = lax.broadcasted_iota(jnp.int32, o_ref.shape, 1)
    out = jnp.zeros(o_ref.shape, F32)
    for ref, (_, src_lane, dst_lane, width) in zip(refs[:-1], parts):
        x = ref[...]
        if (dst_lane - src_lane) % LANES:
            x = pltpu.roll(x, (dst_lane - src_lane) % LANES, axis=1)
        out = jnp.where((lane >= dst_lane) & (lane < dst_lane + width), x, out)
    o_ref[...] = out.astype(o_ref.dtype)


def _pack_w_in(w_stack, li):
    _, rows, cols = w_stack.shape
    sizes = _in_proj_sizes()
    offs = np.concatenate([[0], np.cumsum(sizes)])
    start = {n: int(offs[i]) for i, n in enumerate(_SEC_NAMES)}
    size = {n: int(sizes[i]) for i, n in enumerate(_SEC_NAMES)}
    src = np.concatenate([start[n] + np.arange(0, size[n], LANES) for n in _SEC_ORDER])
    shifts = tuple(sorted(set(int(s) for s in src % LANES)))
    blk = jnp.asarray(src // LANES, jnp.int32)
    cls = jnp.asarray([shifts.index(int(s)) for s in src % LANES], jnp.int32)
    last = (cols - 1) // LANES
    main = pl.pallas_call(
        functools.partial(_pack_cols_kernel, shifts=shifts),
        grid_spec=pltpu.PrefetchScalarGridSpec(
            num_scalar_prefetch=2,
            grid=(len(src),),
            in_specs=[pl.BlockSpec((None, rows, LANES), lambda j, blk, cls: (li, 0, blk[j])),
                      pl.BlockSpec((None, rows, LANES), lambda j, blk, cls: (li, 0, jnp.minimum(blk[j] + 1, last)))],
            out_specs=pl.BlockSpec((rows, LANES), lambda j, blk, cls: (0, j))),
        out_shape=jax.ShapeDtypeStruct((rows, len(src) * LANES), BF16),
        compiler_params=_cparams(("parallel",)),
        name="w_in_pack",
    )(blk, cls, w_stack, w_stack)
    parts, dest = [], 0
    for n in ("fb", "gc"):
        assert start[n] // LANES == (start[n] + size[n] - 1) // LANES
        parts.append((start[n] // LANES, start[n] % LANES, dest, size[n]))
        dest += size[n]
    small = pl.pallas_call(
        functools.partial(_pack_small_kernel, parts=tuple(parts)),
        grid=(1,),
        in_specs=[pl.BlockSpec((None, rows, LANES), lambda i, b=b: (li, 0, b)) for b, _, _, _ in parts],
        out_specs=pl.BlockSpec((rows, LANES), lambda i: (0, 0)),
        out_shape=jax.ShapeDtypeStruct((rows, LANES), BF16),
        compiler_params=_cparams(("arbitrary",)),
        name="w_in_pack_narrow",
    )(*([w_stack] * len(parts)))
    scales = {"qa": (HEAD_DIM // 2) ** -0.5, "qb": HEAD_DIM ** -0.5, "qc": HEAD_DIM ** -0.5, "qd": HEAD_DIM ** -0.5}
    scale_row = jnp.concatenate([jnp.full((1, size[n]), scales.get(n, 1.0), F32) for n in _SEC_ORDER], axis=1)
    return main, small, scale_row


def _rope_tables(t):
    pos = jnp.arange(t, dtype=F32)[:, None]
    tabs = []
    for d in (HEAD_DIM // 2, HEAD_DIM):
        half = d // 2
        inv = ROPE_THETA ** (-jnp.arange(half, dtype=F32) * 2.0 / d)
        ang = pos * inv[None, :]
        cos, sin = jnp.cos(ang), jnp.sin(ang)
        reps = LANES // d
        tabs.append(jnp.tile(jnp.concatenate([cos, cos], axis=1), (1, reps)))
        tabs.append(jnp.tile(jnp.concatenate([-sin, sin], axis=1), (1, reps)))
    return tabs


def kernel(x, p, ffn1_pre_g, ffn1_w_gate, ffn1_w_up, ffn1_w_down, ffn1_post_g, mix_pre_g, w_in, fox_bf, diff_lam_q1, diff_lam_k1, diff_lam_q2, diff_lam_k2, diff_subln_g, nsa_pe_k, nsa_pe_v, nsa_wk1, nsa_wk2, nsa_wv1, nsa_wv2, w_out, mix_post_g, ffn2_pre_g, ffn2_w_gate, ffn2_w_up, ffn2_w_down, ffn2_post_g, ple_pre_g, ple_w_gate, ple_w_proj, ple_post_g):
    b, t, d = x.shape
    depth = w_in.shape[0]
    m = b * t
    tabs = _rope_tables(t)
    h = x.reshape(m, d)
    xn = _rmsnorm(h, ffn1_pre_g[0])

    def ffn(h, xn, li, wg, wu, wd, g_post, g_next):
        act = _gateup(xn, wg, wu, li)
        return _proj_residual(act, _cast_pad(wd, li, row_mult=FF_PAD), h, g_post, g_next, 0.5)

    for li in range(depth):
        h, xn = ffn(h, xn, li, ffn1_w_gate, ffn1_w_up, ffn1_w_down, ffn1_post_g[li], mix_pre_g[li])

        w_main, w_small, scale_row = _pack_w_in(w_in, li)
        z, zs = _inproj(xn, w_main, w_small, scale_row, tabs, t)
        z3 = z.reshape(b, t, z.shape[1])
        bf_row = jnp.pad(fox_bf[li].astype(F32), (0, LANES - N_HEADS_FOX)).reshape(1, LANES)
        c_col, c_row, gates = _small_prep(zs.reshape(b, t, LANES), bf_row)

        lam_init = 0.8 - 0.6 * math.exp(-0.3 * li)
        lam = (jnp.exp(jnp.sum(diff_lam_q1[li].astype(F32) * diff_lam_k1[li].astype(F32)))
               - jnp.exp(jnp.sum(diff_lam_q2[li].astype(F32) * diff_lam_k2[li].astype(F32))) + lam_init)
        o_a = _diff_attention(z3, lam, diff_subln_g[li].astype(F32), lam_init)
        o_b = _fox_attention(z3, c_col, c_row)
        kc, vc = _nsa_compress(z3, nsa_pe_k[li], nsa_pe_v[li], nsa_wk1[li].astype(BF16), nsa_wk2[li].astype(BF16),
                               nsa_wv1[li].astype(BF16), nsa_wv2[li].astype(BF16))
        o_c = _nsa_attention(z3, kc, vc, gates)
        o_d = _dilated_attention(z3)
        o = jnp.concatenate([o_a, o_b, o_c, o_d], axis=-1).reshape(m, -1)
        h, xn = _proj_residual(o, _cast_pad(w_out, li), h, mix_post_g[li], ffn2_pre_g[li], 1.0)

        h, xn = ffn(h, xn, li, ffn2_w_gate, ffn2_w_up, ffn2_w_down, ffn2_post_g[li], ple_pre_g[li])

        g_next = ffn1_pre_g[li + 1] if li + 1 < depth else None
        h, xn = _proj_residual(xn, _cast_pad(ple_w_gate, li), h, ple_post_g[li], g_next, 1.0,
                               p=p[li].reshape(m, -1), wp=ple_w_proj[li].astype(BF16))
    return h.reshape(b, t, d)
```

```python
import functools
import math

import numpy as np
import jax
import jax.numpy as jnp
from jax import lax
from jax.experimental import pallas as pl
from jax.experimental.pallas import tpu as pltpu

F32 = jnp.float32
BF16 = jnp.bfloat16

HEAD_DIM = 128
N_HEADS_DIFF = 8
N_HEADS_FOX = 7
N_HEADS_NSA = 8
N_KV_NSA = 2
N_HEADS_DIL = 9
ROPE_THETA = 10000.0
EPS = 1e-6
CMP_LEN = 32
CMP_STRIDE = 16
SEL_LEN = 64
SEL_TOPN = 16
WIN_LEN = 512
DIL_PATTERNS = ((128, 1), (512, 4), (2048, 16))
NEG_INF = -1e30
POS_BIG = 1e30

V7X_VMEM_LIMIT_BYTES = 60 * 1024 * 1024
LANES = 128
FF_PAD = 512
ATT_TQ = 256
ATT_TK = 256

COL_QA, COL_KA = 0, 8
COL_QC, COL_KCC, COL_KSC, COL_KWC, COL_QD, COL_KD = 16, 24, 26, 28, 30, 39
COL_VA, COL_QB, COL_KB, COL_VB, COL_VCC, COL_VSC, COL_VWC, COL_VD = 48, 56, 63, 70, 77, 79, 81, 83
N_COLBLK = 92
N_ROPE64_BLK = 16
N_ROPE128_BLK = 32
IN_TN = 512


def _cparams(sem):
    return pltpu.CompilerParams(dimension_semantics=sem, vmem_limit_bytes=V7X_VMEM_LIMIT_BYTES)


def _pick(n, prefs):
    for p in prefs:
        if n % p == 0:
            return p
    return n


def _rmsnorm_kernel(x_ref, g_ref, o_ref):
    x = x_ref[...]
    ms = jnp.mean(x * x, axis=-1, keepdims=True)
    o_ref[...] = (x * lax.rsqrt(ms + EPS) * g_ref[...]).astype(o_ref.dtype)


def _rmsnorm(x, g):
    m, d = x.shape
    bm = _pick(m, (256, 128, 64, 32, 16, 8))
    return pl.pallas_call(
        _rmsnorm_kernel,
        grid=(m // bm,),
        in_specs=[pl.BlockSpec((bm, d), lambda i: (i, 0)), pl.BlockSpec((1, d), lambda i: (0, 0))],
        out_specs=pl.BlockSpec((bm, d), lambda i: (i, 0)),
        out_shape=jax.ShapeDtypeStruct((m, d), BF16),
        compiler_params=_cparams(("parallel",)),
        name="rmsnorm",
    )(x, g.reshape(1, d))


def _gateup_kernel(x_ref, wg_ref, wu_ref, o_ref, *, f_valid):
    x = x_ref[...]
    g = jnp.dot(x, wg_ref[...].astype(BF16), preferred_element_type=F32)
    u = jnp.dot(x, wu_ref[...].astype(BF16), preferred_element_type=F32)
    y = g * jax.nn.sigmoid(g) * u
    bf = y.shape[1]
    if f_valid % bf:
        col = pl.program_id(1) * bf + lax.broadcasted_iota(jnp.int32, (1, bf), 1)
        y = jnp.where(col < f_valid, y, 0.0)
    o_ref[...] = y.astype(o_ref.dtype)


def _gateup(xn, wg_stack, wu_stack, li):
    m, d = xn.shape
    f = wg_stack.shape[2]
    f_pad = -(-f // FF_PAD) * FF_PAD
    bm = _pick(m, (2048, 1024, 512, 256, 128))
    bf = _pick(f_pad, (256, 128))
    last = (f - 1) // bf
    w_spec = pl.BlockSpec((None, d, bf), lambda i, j: (li, 0, jnp.minimum(j, last)))
    return pl.pallas_call(
        functools.partial(_gateup_kernel, f_valid=f),
        grid=(m // bm, f_pad // bf),
        in_specs=[pl.BlockSpec((bm, d), lambda i, j: (i, 0)), w_spec, w_spec],
        out_specs=pl.BlockSpec((bm, bf), lambda i, j: (i, j)),
        out_shape=jax.ShapeDtypeStruct((m, f_pad), BF16),
        compiler_params=_cparams(("parallel", "arbitrary")),
        name="ffn_gateup",
    )(xn, wg_stack, wu_stack)


EPI_ROWS = 32


def _residual_epilogue(o_ref, h_ref, gpost_ref, gnext_ref, xn_ref):
    bm = o_ref.shape[0]
    rows = min(EPI_ROWS, bm)

    def body(r, carry):
        sl = pl.ds(pl.multiple_of(r * rows, rows), rows)
        f = o_ref[sl, :]
        hn = h_ref[sl, :] + f * lax.rsqrt(jnp.mean(f * f, axis=-1, keepdims=True) + EPS) * gpost_ref[...]
        o_ref[sl, :] = hn
        if xn_ref is not None:
            ms = jnp.mean(hn * hn, axis=-1, keepdims=True)
            xn_ref[sl, :] = (hn * lax.rsqrt(ms + EPS) * gnext_ref[...]).astype(xn_ref.dtype)
        return carry

    lax.fori_loop(0, bm // rows, body, 0, unroll=2)


def _proj_k_kernel(a_ref, w_ref, h_ref, gpost_ref, gnext_ref, o_ref, *rest, nk):
    k = pl.program_id(1)

    @pl.when(k == 0)
    def _():
        o_ref[...] = jnp.dot(a_ref[...], w_ref[...], preferred_element_type=F32)

    @pl.when(k > 0)
    def _():
        o_ref[...] += jnp.dot(a_ref[...], w_ref[...], preferred_element_type=F32)

    @pl.when(k == nk - 1)
    def _():
        _residual_epilogue(o_ref, h_ref, gpost_ref, gnext_ref, rest[0] if rest else None)


def _proj_n_kernel(*refs, ple, nj, emit_next):
    a_ref, w_ref, h_ref, gpost_ref, gnext_ref = refs[:5]
    refs = refs[5:]
    if ple:
        p_ref, wp_ref = refs[:2]
        refs = refs[2:]
    o_ref = refs[0]
    xn_ref = refs[1] if emit_next else None
    j = pl.program_id(1)
    bn = w_ref.shape[1]
    f = jnp.dot(a_ref[...], w_ref[...], preferred_element_type=F32)
    if ple:
        f = jax.nn.sigmoid(f) * jnp.dot(p_ref[...].astype(BF16), wp_ref[...], preferred_element_type=F32)
    o_ref[:, pl.ds(pl.multiple_of(j * bn, bn), bn)] = f

    @pl.when(j == nj - 1)
    def _():
        _residual_epilogue(o_ref, h_ref, gpost_ref, gnext_ref, xn_ref)


def _proj_w_kernel(*refs, ple, emit_next):
    a_ref, w_ref, h_ref, gpost_ref, gnext_ref = refs[:5]
    refs = refs[5:]
    if ple:
        p_ref, wp_ref = refs[:2]
        refs = refs[2:]
    o_ref = refs[0]
    f = jnp.dot(a_ref[...], w_ref[...], preferred_element_type=F32)
    if ple:
        f = jax.nn.sigmoid(f) * jnp.dot(p_ref[...].astype(BF16), wp_ref[...], preferred_element_type=F32)
    o_ref[...] = f
    _residual_epilogue(o_ref, h_ref, gpost_ref, gnext_ref, refs[1] if emit_next else None)


PROJ_FULL_K_MAX = 4096
PROJ_RESIDENT_W_MAX_BYTES = 36 * 1024 * 1024
PROJ_RESIDENT_BUDGET_BYTES = 56 * 1024 * 1024


def _proj_residual(a, w, h, g_post, g_next, coef, p=None, wp=None):
    m, kdim = a.shape
    d = w.shape[1]
    ple = p is not None
    emit_next = g_next is not None
    row_args = [h, (coef * g_post).reshape(1, d), (g_next if emit_next else g_post).reshape(1, d)]
    out_shape = [jax.ShapeDtypeStruct((m, d), F32)] + ([jax.ShapeDtypeStruct((m, d), BF16)] if emit_next else [])
    w_bytes = w.size * w.dtype.itemsize + (wp.size * wp.dtype.itemsize if ple else 0)
    if w_bytes <= PROJ_RESIDENT_W_MAX_BYTES:
        row_bytes = 2 * (2 * kdim + 4 * d + 4 * d + 2 * d) + 4 * d
        bm = _pick(m, tuple(t for t in (512, 256, 128) if t * row_bytes + w_bytes <= PROJ_RESIDENT_BUDGET_BYTES))
        once = dict(pipeline_mode=pl.Buffered(1))
        in_specs = [pl.BlockSpec((bm, kdim), lambda i: (i, 0)),
                    pl.BlockSpec((kdim, d), lambda i: (0, 0), **once),
                    pl.BlockSpec((bm, d), lambda i: (i, 0)),
                    pl.BlockSpec((1, d), lambda i: (0, 0)),
                    pl.BlockSpec((1, d), lambda i: (0, 0))]
        args = [a, w] + row_args
        if ple:
            in_specs += [pl.BlockSpec((bm, p.shape[1]), lambda i: (i, 0)),
                         pl.BlockSpec(wp.shape, lambda i: (0, 0), **once)]
            args += [p, wp]
        res = pl.pallas_call(
            functools.partial(_proj_w_kernel, ple=ple, emit_next=emit_next),
            grid=(m // bm,),
            in_specs=in_specs,
            out_specs=[pl.BlockSpec((bm, d), lambda i: (i, 0))] * len(out_shape),
            out_shape=out_shape,
            compiler_params=_cparams(("parallel",)),
            name="ple_residual" if ple else "proj_residual",
        )(*args)
        return (res[0], res[1]) if emit_next else (res[0], None)
    bm = _pick(m, (512, 256, 128))
    row_specs = [pl.BlockSpec((bm, d), lambda i, k: (i, 0)),
                 pl.BlockSpec((1, d), lambda i, k: (0, 0)),
                 pl.BlockSpec((1, d), lambda i, k: (0, 0))]
    out_specs = [pl.BlockSpec((bm, d), lambda i, k: (i, 0))]
    if emit_next:
        out_specs.append(pl.BlockSpec((bm, d), lambda i, k: (i, 0), pipeline_mode=pl.Buffered(1)))
    if kdim <= PROJ_FULL_K_MAX:
        bn = _pick(d, (512, 256, 128))
        nj = d // bn
        in_specs = [pl.BlockSpec((bm, kdim), lambda i, j: (i, 0)),
                    pl.BlockSpec((kdim, bn), lambda i, j: (0, j))] + row_specs
        args = [a, w] + row_args
        if ple:
            in_specs += [pl.BlockSpec((bm, p.shape[1]), lambda i, j: (i, 0)),
                         pl.BlockSpec((wp.shape[0], bn), lambda i, j: (0, j))]
            args += [p, wp]
        body = functools.partial(_proj_n_kernel, ple=ple, nj=nj, emit_next=emit_next)
        grid = (m // bm, nj)
    else:
        assert not ple
        bk = _pick(kdim, (1024, 512, 256, 128))
        nk = kdim // bk
        in_specs = [pl.BlockSpec((bm, bk), lambda i, k: (i, k)),
                    pl.BlockSpec((bk, d), lambda i, k: (k, 0))] + row_specs
        args = [a, w] + row_args
        body = functools.partial(_proj_k_kernel, nk=nk)
        grid = (m // bm, nk)
    res = pl.pallas_call(
        body,
        grid=grid,
        in_specs=in_specs,
        out_specs=out_specs,
        out_shape=out_shape,
        compiler_params=_cparams(("parallel", "arbitrary")),
        name="ple_residual" if ple else "proj_residual",
    )(*args)
    return (res[0], res[1]) if emit_next else (res[0], None)


def _cast_pad_kernel(x_ref, o_ref, *, rows, cols, masked):
    x = x_ref[...]
    if masked:
        br, bc = x.shape
        r = pl.program_id(0) * br + lax.broadcasted_iota(jnp.int32, x.shape, 0)
        c = pl.program_id(1) * bc + lax.broadcasted_iota(jnp.int32, x.shape, 1)
        x = jnp.where((r < rows) & (c < cols), x, 0.0)
    o_ref[...] = x.astype(o_ref.dtype)


def _cast_pad(w_stack, li, row_mult=1, col_mult=1):
    _, rows, cols = w_stack.shape
    rows_p = -(-rows // row_mult) * row_mult
    cols_p = -(-cols // col_mult) * col_mult
    br = _pick(rows_p, (512, 256, 128))
    bc = _pick(cols_p, (1024, 512, 256, 128))
    last_i, last_j = (rows - 1) // br, (cols - 1) // bc
    return pl.pallas_call(
        functools.partial(_cast_pad_kernel, rows=rows, cols=cols, masked=(rows_p, cols_p) != (rows, cols)),
        grid=(rows_p // br, cols_p // bc),
        in_specs=[pl.BlockSpec((None, br, bc), lambda i, j: (li, jnp.minimum(i, last_i), jnp.minimum(j, last_j)))],
        out_specs=pl.BlockSpec((br, bc), lambda i, j: (i, j)),
        out_shape=jax.ShapeDtypeStruct((rows_p, cols_p), BF16),
        compiler_params=_cparams(("parallel", "parallel")),
        name="weight_cast_pad",
    )(w_stack)


def _rope_tile(acc, cos_ref, sin_ref, half, n_slabs):
    cos = cos_ref[...]
    sin = sin_ref[...]
    outs = []
    for s in range(n_slabs):
        a = acc[:, s * LANES:(s + 1) * LANES]
        if half == LANES // 2:
            partner = pltpu.roll(a, LANES // 2, axis=1)
        else:
            lane = lax.broadcasted_iota(jnp.int32, a.shape, 1)
            fwd = pltpu.roll(a, LANES - half, axis=1)
            bwd = pltpu.roll(a, half, axis=1)
            partner = jnp.where((lane & (2 * half - 1)) < half, fwd, bwd)
        outs.append(a * cos + partner * sin)
    return jnp.concatenate(outs, axis=1)


def _inproj_kernel(x_ref, w_ref, ws_ref, scale_ref, c64_ref, s64_ref, c128_ref, s128_ref,
                   o_ref, os_ref, *, n64, n128):
    j = pl.program_id(1)
    n_slabs = w_ref.shape[1] // LANES

    def product():
        return jnp.dot(x_ref[...], w_ref[...], preferred_element_type=F32)

    @pl.when(j == 0)
    def _():
        os_ref[...] = jnp.dot(x_ref[...], ws_ref[...], preferred_element_type=F32)

    @pl.when(j < n64)
    def _():
        o_ref[...] = (_rope_tile(product(), c64_ref, s64_ref, 32, n_slabs) * scale_ref[...]).astype(o_ref.dtype)

    @pl.when((j >= n64) & (j < n64 + n128))
    def _():
        o_ref[...] = (_rope_tile(product(), c128_ref, s128_ref, 64, n_slabs) * scale_ref[...]).astype(o_ref.dtype)

    @pl.when(j >= n64 + n128)
    def _():
        o_ref[...] = (product() * scale_ref[...]).astype(o_ref.dtype)


def _inproj(xn, w_main, w_small, scale_row, tabs, seq):
    m, d = xn.shape
    n = w_main.shape[1]
    bm = _pick(seq, (1024, 512, 256))
    bn = IN_TN
    tpb = seq // bm
    tab_spec = pl.BlockSpec((bm, LANES), lambda i, j: (i % tpb, 0))
    return pl.pallas_call(
        functools.partial(_inproj_kernel, n64=N_ROPE64_BLK * LANES // bn, n128=N_ROPE128_BLK * LANES // bn),
        grid=(m // bm, n // bn),
        in_specs=[pl.BlockSpec((bm, d), lambda i, j: (i, 0)),
                  pl.BlockSpec((d, bn), lambda i, j: (0, j)),
                  pl.BlockSpec((d, LANES), lambda i, j: (0, 0)),
                  pl.BlockSpec((1, bn), lambda i, j: (0, j)),
                  tab_spec, tab_spec, tab_spec, tab_spec],
        out_specs=[pl.BlockSpec((bm, bn), lambda i, j: (i, j)),
                   pl.BlockSpec((bm, LANES), lambda i, j: (i, 0))],
        out_shape=[jax.ShapeDtypeStruct((m, n), BF16), jax.ShapeDtypeStruct((m, LANES), F32)],
        compiler_params=_cparams(("parallel", "arbitrary")),
        name="in_proj_rope",
    )(xn, w_main, w_small, scale_row, *tabs)


def _softmax_terms(q, k_ref, c_lo, c_mask, c_hi, t_rows, mask_fn, bias_fn=None):
    n0, nm, n1 = c_lo * ATT_TK, c_mask * ATT_TK, c_hi * ATT_TK
    s = lax.dot_general(q, k_ref[n0:n1, :], (((1,), (1,)), ((), ())), preferred_element_type=F32)
    if bias_fn is not None:
        s = s + bias_fn(n0, n1)
    kpos = nm + lax.broadcasted_iota(jnp.int32, (1, n1 - nm), 1)
    tail = jnp.where(mask_fn(t_rows, kpos, nm, n1), s[:, nm - n0:], NEG_INF)
    s = jnp.concatenate([s[:, :nm - n0], tail], axis=1) if nm > n0 else tail
    m = jnp.max(s, axis=1, keepdims=True)
    p = jnp.exp(s - m)
    return p, m, jnp.sum(p, axis=1, keepdims=True)


def _attend(q, k_ref, v_ref, c_lo, c_mask, c_hi, t_rows, mask_fn, bias_fn=None):
    p, m, l = _softmax_terms(q, k_ref, c_lo, c_mask, c_hi, t_rows, mask_fn, bias_fn)
    acc = jnp.dot(p.astype(BF16), v_ref[c_lo * ATT_TK:c_hi * ATT_TK, :], preferred_element_type=F32)
    return m, l, acc


def _causal_mask(t_rows, kpos, n0, n1):
    return kpos <= t_rows


def _row_positions(q0, tq, reps):
    r = lax.broadcasted_iota(jnp.int32, (reps * tq, 1), 0)
    return q0 + (r & (tq - 1))


def _per_query_tile(n_tiles, fn, qi=None):
    for c in range(n_tiles):
        if qi is None:
            pl.when(pl.program_id(0) >= 0)(functools.partial(fn, c, slice(c * ATT_TQ, (c + 1) * ATT_TQ)))
        else:
            pl.when(qi == c)(functools.partial(fn, c, slice(0, ATT_TQ)))


def _first_chunk(c, reach):
    return max(c - (reach + ATT_TK - 1) // ATT_TK, 0)


def _diff_kernel(lam_ref, q_ref, k_ref, v_ref, g_ref, o_ref, *, out_scale):
    tq = ATT_TQ

    def tile(c, rows):
        q = q_ref[rows, :]
        lane = lax.broadcasted_iota(jnp.int32, q.shape, 1)
        zero = jnp.zeros_like(q)
        q2 = jnp.concatenate([jnp.where(lane < HEAD_DIM // 2, q, zero),
                              jnp.where(lane >= HEAD_DIM // 2, q, zero)], axis=0)
        t_rows = _row_positions(c * tq, tq, 2)
        p, _, l = _softmax_terms(q2, k_ref, 0, c, c + 1, t_rows, _causal_mask)
        r = 1.0 / l
        diff_map = p[:tq] * r[:tq] - p[tq:] * (lam_ref[0] * r[tq:])
        a = jnp.dot(diff_map.astype(BF16), v_ref[:(c + 1) * ATT_TK, :], preferred_element_type=F32)
        y = a * lax.rsqrt(jnp.mean(a * a, axis=-1, keepdims=True) + EPS) * g_ref[...]
        o_ref[rows, :] = (y * out_scale).astype(o_ref.dtype)

    _per_query_tile(k_ref.shape[0] // tq, tile)


def _head_spec(t, col):
    return pl.BlockSpec((None, t, LANES), lambda bi, h: (bi, 0, col + h))


def _diff_attention(z3, lam, subln_g, lam_init):
    b, t, _ = z3.shape
    return pl.pallas_call(
        functools.partial(_diff_kernel, out_scale=1.0 - lam_init),
        grid=(b, N_HEADS_DIFF),
        in_specs=[pl.BlockSpec(memory_space=pltpu.SMEM),
                  _head_spec(t, COL_QA), _head_spec(t, COL_KA), _head_spec(t, COL_VA),
                  pl.BlockSpec((1, LANES), lambda bi, h: (0, 0))],
        out_specs=_head_spec(t, 0),
        out_shape=jax.ShapeDtypeStruct((b, t, N_HEADS_DIFF * HEAD_DIM), BF16),
        compiler_params=_cparams(("parallel", "parallel")),
        name="diff_attention",
    )(lam.reshape(1), z3, z3, z3, subln_g.reshape(1, LANES))


def _small_prep_kernel(zs_ref, bf_ref, c_ref, ct_ref, g_ref):
    zs = zs_ref[...]
    t = zs.shape[0]
    c = jax.nn.log_sigmoid(zs + bf_ref[...])
    row = lax.broadcasted_iota(jnp.int32, c.shape, 0)
    shift = 1
    while shift < t:
        c = c + jnp.where(row >= shift, pltpu.roll(c, shift, axis=0), 0.0)
        shift *= 2
    c_ref[...] = c
    ct_ref[...] = jnp.transpose(c)[:8, :]
    g_ref[...] = jax.nn.sigmoid(zs)


def _small_prep(zs3, bf_row):
    b, t, _ = zs3.shape
    return pl.pallas_call(
        _small_prep_kernel,
        grid=(b,),
        in_specs=[pl.BlockSpec((None, t, LANES), lambda bi: (bi, 0, 0)),
                  pl.BlockSpec((1, LANES), lambda bi: (0, 0))],
        out_specs=[pl.BlockSpec((None, t, LANES), lambda bi: (bi, 0, 0)),
                   pl.BlockSpec((None, 8, t), lambda bi: (bi, 0, 0)),
                   pl.BlockSpec((None, t, LANES), lambda bi: (bi, 0, 0))],
        out_shape=[jax.ShapeDtypeStruct((b, t, LANES), F32),
                   jax.ShapeDtypeStruct((b, 8, t), F32),
                   jax.ShapeDtypeStruct((b, t, LANES), F32)],
        compiler_params=_cparams(("parallel",)),
        name="forget_cumsum_gates",
    )(zs3, bf_row)


def _fox_kernel(q_ref, k_ref, v_ref, c_ref, ct_ref, o_ref):
    h = pl.program_id(1)
    tq = ATT_TQ

    def tile(c, rows):
        cc = c_ref[rows, :]
        lane = lax.broadcasted_iota(jnp.int32, cc.shape, 1)
        cq = jnp.sum(jnp.where(lane == h, cc, 0.0), axis=1, keepdims=True)

        def bias_fn(n0, n1):
            return cq - ct_ref[pl.ds(h, 1), n0:n1]

        t_rows = _row_positions(c * tq, tq, 1)
        _, l, acc = _attend(q_ref[rows, :], k_ref, v_ref, 0, c, c + 1, t_rows, _causal_mask, bias_fn)
        o_ref[rows, :] = (acc / l).astype(o_ref.dtype)

    _per_query_tile(k_ref.shape[0] // tq, tile)


def _fox_attention(z3, c_col, c_row):
    b, t, _ = z3.shape
    return pl.pallas_call(
        _fox_kernel,
        grid=(b, N_HEADS_FOX),
        in_specs=[_head_spec(t, COL_QB), _head_spec(t, COL_KB), _head_spec(t, COL_VB),
                  pl.BlockSpec((None, t, LANES), lambda bi, h: (bi, 0, 0)),
                  pl.BlockSpec((None, 8, t), lambda bi, h: (bi, 0, 0))],
        out_specs=_head_spec(t, 0),
        out_shape=jax.ShapeDtypeStruct((b, t, N_HEADS_FOX * HEAD_DIM), BF16),
        compiler_params=_cparams(("parallel", "parallel")),
        name="forgetting_attention",
    )(z3, z3, z3, c_col, c_row)


def _compress_one(x_ref, xs_ref, pe_ref, w1_ref, w2_ref, o_ref):
    t = x_ref.shape[0]
    nb = t // CMP_STRIDE
    xs_ref[...] = x_ref[...].astype(F32)
    pe = pe_ref[...]
    lo, hi = [], []
    for i in range(CMP_STRIDE):
        xi = xs_ref[pl.ds(i, nb, stride=CMP_STRIDE), :]
        lo.append((xi + pe[i:i + 1, :]).astype(BF16))
        hi.append((xi + pe[CMP_STRIDE + i:CMP_STRIDE + i + 1, :]).astype(BF16))
    half = CMP_STRIDE * HEAD_DIM
    a = jnp.dot(jnp.concatenate(lo, axis=1), w1_ref[:half, :], preferred_element_type=F32)
    bb = jnp.dot(jnp.concatenate(hi, axis=1), w1_ref[half:, :], preferred_element_type=F32)
    y = a + pltpu.roll(bb, nb - 1, axis=0)
    out = jnp.dot(jax.nn.gelu(y).astype(BF16), w2_ref[...], preferred_element_type=F32)
    row = lax.broadcasted_iota(jnp.int32, out.shape, 0)
    o_ref[...] = jnp.where(row < nb - 1, out, 0.0).astype(o_ref.dtype)


def _compress_kernel(k_ref, v_ref, pek_ref, pev_ref, wk1_ref, wk2_ref, wv1_ref, wv2_ref,
                     kc_ref, vc_ref, xs_ref):
    _compress_one(k_ref, xs_ref, pek_ref, wk1_ref, wk2_ref, kc_ref)
    _compress_one(v_ref, xs_ref, pev_ref, wv1_ref, wv2_ref, vc_ref)


def _nsa_compress(z3, pe_k, pe_v, wk1, wk2, wv1, wv2):
    b, t, _ = z3.shape
    nb = t // CMP_STRIDE
    full = lambda a: pl.BlockSpec(a.shape, lambda bi, g: (0,) * a.ndim)
    return pl.pallas_call(
        _compress_kernel,
        grid=(b, N_KV_NSA),
        in_specs=[pl.BlockSpec((None, t, LANES), lambda bi, g: (bi, 0, COL_KCC + g)),
                  pl.BlockSpec((None, t, LANES), lambda bi, g: (bi, 0, COL_VCC + g)),
                  full(pe_k), full(pe_v), full(wk1), full(wk2), full(wv1), full(wv2)],
        out_specs=[pl.BlockSpec((None, None, nb, LANES), lambda bi, g: (bi, g, 0, 0)),
                   pl.BlockSpec((None, None, nb, LANES), lambda bi, g: (bi, g, 0, 0))],
        out_shape=[jax.ShapeDtypeStruct((b, N_KV_NSA, nb, LANES), BF16),
                   jax.ShapeDtypeStruct((b, N_KV_NSA, nb, LANES), BF16)],
        scratch_shapes=[pltpu.VMEM((t, LANES), F32)],
        compiler_params=_cparams(("parallel", "parallel")),
        name="nsa_compress",
    )(z3, z3, pe_k, pe_v, wk1, wk2, wv1, wv2)


def _nsa_tile(c, rows, g, q_ref, kc_ref, vc_ref, ks_ref, vs_ref, kw_ref, vw_ref, gate_ref, o_ref, seq):
    tq = ATT_TQ
    reps = N_HEADS_NSA // N_KV_NSA
    q0 = c * tq
    q4 = jnp.concatenate([q_ref[rows, r * LANES:(r + 1) * LANES] for r in range(reps)], axis=0)
    t_rows = _row_positions(q0, tq, reps)

    nb = kc_ref.shape[0]
    n_cmp = (seq - CMP_LEN) // CMP_STRIDE + 1
    s = lax.dot_general(q4, kc_ref[...], (((1,), (1,)), ((), ())), preferred_element_type=F32)
    cidx = lax.broadcasted_iota(jnp.int32, (1, nb), 1)
    valid = (cidx * CMP_STRIDE + (CMP_LEN - 1) <= t_rows) & (cidx < n_cmp)
    s = jnp.where(valid, s, NEG_INF)
    e = jnp.where(valid, jnp.exp(s - jnp.max(s, axis=1, keepdims=True)), 0.0)
    den = jnp.sum(e, axis=1, keepdims=True)
    p_cmp = e / jnp.where(den > 0, den, 1.0)
    o_cmp = jnp.dot(p_cmp.astype(BF16), vc_ref[...], preferred_element_type=F32)

    n_sel = seq // SEL_LEN
    n_sel_pad = LANES
    p_sum = p_cmp[:tq]
    for r in range(1, reps):
        p_sum = p_sum + p_cmp[r * tq:(r + 1) * tq]
    p_hi = p_sum.astype(BF16)
    p_lo = (p_sum - p_hi.astype(F32)).astype(BF16)
    srow = lax.broadcasted_iota(jnp.int32, (n_sel_pad, nb), 0)
    ccol = lax.broadcasted_iota(jnp.int32, (n_sel_pad, nb), 1)
    overlap = ((ccol * CMP_STRIDE < srow * SEL_LEN + SEL_LEN) &
               (ccol * CMP_STRIDE + CMP_LEN > srow * SEL_LEN) & (ccol < n_cmp) & (srow < n_sel))
    ov = jnp.where(overlap, 1.0, 0.0).astype(BF16)
    nt = (((1,), (1,)), ((), ()))
    imp_t = (lax.dot_general(ov, p_hi, nt, preferred_element_type=F32) +
             lax.dot_general(ov, p_lo, nt, preferred_element_type=F32))
    n_rank = ((n_sel + 7) // 8) * 8
    imp_t = imp_t[:n_rank]
    sblk = lax.broadcasted_iota(jnp.int32, (n_rank, tq), 0)
    tpos = q0 + lax.broadcasted_iota(jnp.int32, (n_rank, tq), 1)
    cur = tpos // SEL_LEN
    forced = (sblk == 0) | (sblk == cur) | (sblk == cur - 1)
    val = jnp.where(forced, POS_BIG, jnp.where(sblk * SEL_LEN <= tpos, imp_t, NEG_INF))
    val = jnp.where(sblk < n_sel, val, -2.0 * POS_BIG)
    rank = jnp.zeros((n_rank, tq), F32)
    for s2 in range(n_sel):
        other = val[s2:s2 + 1, :]
        tie = jnp.where(sblk > s2, 1.0, 0.0)
        rank = rank + jnp.where(other > val, 1.0, jnp.where(other == val, tie, 0.0))
    n_top = min(SEL_TOPN, n_sel)
    sel_t = jnp.where((rank < n_top) & (sblk < n_sel), 1.0, 0.0)
    if n_rank < n_sel_pad:
        sel_t = jnp.concatenate([sel_t, jnp.zeros((n_sel_pad - n_rank, tq), F32)], axis=0)
    sel = jnp.transpose(sel_t).astype(BF16)

    def sel_mask(t_r, kpos, n0, n1):
        erow = lax.broadcasted_iota(jnp.int32, (n_sel_pad, n1 - n0), 0)
        ecol = n0 + lax.broadcasted_iota(jnp.int32, (n_sel_pad, n1 - n0), 1)
        expand = jnp.where(ecol // SEL_LEN == erow, 1.0, 0.0).astype(BF16)
        chosen = jnp.dot(sel, expand, preferred_element_type=F32)
        chosen = jnp.concatenate([chosen] * reps, axis=0)
        return (chosen > 0.5) & (kpos <= t_r)

    _, l_s, acc_s = _attend(q4, ks_ref, vs_ref, 0, 0, c + 1, t_rows, sel_mask)
    o_sel = acc_s / l_s

    def win_mask(t_r, kpos, n0, n1):
        return (kpos <= t_r) & (kpos > t_r - WIN_LEN)

    lo = _first_chunk(c, WIN_LEN)
    _, l_w, acc_w = _attend(q4, kw_ref, vw_ref, lo, lo, c + 1, t_rows, win_mask)
    o_win = acc_w / l_w

    gates = gate_ref[rows, :]
    lane = lax.broadcasted_iota(jnp.int32, gates.shape, 1)
    n_fb = N_HEADS_FOX
    outs = []
    for r in range(reps):
        col = n_fb + 3 * (g * reps + r)
        rs = slice(r * tq, (r + 1) * tq)
        o_r = jnp.zeros((tq, LANES), F32)
        for br, o_br in enumerate((o_cmp, o_sel, o_win)):
            gcol = jnp.sum(jnp.where(lane == col + br, gates, 0.0), axis=1, keepdims=True)
            o_r = o_r + gcol * o_br[rs]
        outs.append(o_r)
    o_ref[rows, :] = jnp.concatenate(outs, axis=1).astype(o_ref.dtype)


def _nsa_kernel(*refs, seq):
    g = pl.program_id(1)
    _per_query_tile(seq // ATT_TQ, lambda c, rows: _nsa_tile(c, rows, g, *refs, seq), qi=pl.program_id(2))


def _nsa_attention(z3, kc, vc, gates):
    b, t, _ = z3.shape
    tq = ATT_TQ
    reps = N_HEADS_NSA // N_KV_NSA
    nb = kc.shape[2]
    kv = lambda col: pl.BlockSpec((None, t, LANES), lambda bi, g, qi: (bi, 0, col + g))
    return pl.pallas_call(
        functools.partial(_nsa_kernel, seq=t),
        grid=(b, N_KV_NSA, t // tq),
        in_specs=[pl.BlockSpec((None, tq, reps * LANES), lambda bi, g, qi: (bi, qi, COL_QC // reps + g)),
                  pl.BlockSpec((None, None, nb, LANES), lambda bi, g, qi: (bi, g, 0, 0)),
                  pl.BlockSpec((None, None, nb, LANES), lambda bi, g, qi: (bi, g, 0, 0)),
                  kv(COL_KSC), kv(COL_VSC), kv(COL_KWC), kv(COL_VWC),
                  pl.BlockSpec((None, tq, LANES), lambda bi, g, qi: (bi, qi, 0))],
        out_specs=pl.BlockSpec((None, tq, reps * LANES), lambda bi, g, qi: (bi, qi, g)),
        out_shape=jax.ShapeDtypeStruct((b, t, N_HEADS_NSA * HEAD_DIM), BF16),
        compiler_params=_cparams(("parallel", "parallel", "arbitrary")),
        name="nsa_attention",
    )(z3, kc, vc, z3, z3, z3, z3, gates)


def _dil_kernel(*refs):
    n_g = len(DIL_PATTERNS)
    q_refs, k_refs, v_refs = refs[:n_g], refs[n_g:2 * n_g], refs[2 * n_g:3 * n_g]
    o_ref = refs[3 * n_g]
    tq = ATT_TQ

    def tile(c, rows):
        t_rows = _row_positions(c * tq, tq, 1)
        parts = []
        for gi, (w, r) in enumerate(DIL_PATTERNS):
            def mask(t_r, kpos, n0, n1, w=w, r=r):
                dist = t_r - kpos
                return (dist >= 0) & (dist <= w) & ((dist & (r - 1)) == 0)

            lo = _first_chunk(c, w)
            parts.append(_attend(q_refs[gi][rows, :], k_refs[gi], v_refs[gi], lo, lo, c + 1, t_rows, mask))
        lses = [m + jnp.log(l) for (m, l, _) in parts]
        top = functools.reduce(jnp.maximum, lses)
        ws = [jnp.exp(x - top) for x in lses]
        tot = functools.reduce(lambda a, b2: a + b2, ws)
        out = jnp.zeros((tq, LANES), F32)
        for wgt, (m, l, acc) in zip(ws, parts):
            out = out + (wgt / tot) * (acc / l)
        o_ref[rows, :] = out.astype(o_ref.dtype)

    _per_query_tile(k_refs[0].shape[0] // tq, tile)


def _dilated_attention(z3):
    b, t, _ = z3.shape
    n_g = len(DIL_PATTERNS)
    hg = N_HEADS_DIL // n_g
    specs = [_head_spec(t, col + gi * hg) for col in (COL_QD, COL_KD, COL_VD) for gi in range(n_g)]
    return pl.pallas_call(
        _dil_kernel,
        grid=(b, hg),
        in_specs=specs,
        out_specs=_head_spec(t, 0),
        out_shape=jax.ShapeDtypeStruct((b, t, hg * HEAD_DIM), BF16),
        compiler_params=_cparams(("parallel", "parallel")),
        name="dilated_attention",
    )(*([z3] * (3 * n_g)))


def _in_proj_sizes():
    hd = HEAD_DIM
    kv = N_KV_NSA * hd
    return (N_HEADS_DIFF * hd, N_HEADS_DIFF * hd, N_HEADS_DIFF * hd,
            N_HEADS_FOX * hd, N_HEADS_FOX * hd, N_HEADS_FOX * hd, N_HEADS_FOX,
            N_HEADS_NSA * hd, kv, kv, kv, kv, kv, kv, 3 * N_HEADS_NSA,
            N_HEADS_DIL * hd, N_HEADS_DIL * hd, N_HEADS_DIL * hd)


_SEC_NAMES = ("qa", "ka", "va", "qb", "kb", "vb", "fb", "qc", "kcc", "vcc", "ksc", "vsc", "kwc", "vwc", "gc",
              "qd", "kd", "vd")
_SEC_ORDER = ("qa", "ka", "qc", "kcc", "ksc", "kwc", "qd", "kd", "va", "qb", "kb", "vb", "vcc", "vsc", "vwc", "vd")


def _pack_cols_kernel(blk_ref, cls_ref, a_ref, b_ref, o_ref, *, shifts):
    j = pl.program_id(0)
    for ci, s in enumerate(shifts):
        @pl.when(cls_ref[j] == ci)
        def _(s=s):
            a = a_ref[...]
            if s:
                lane = lax.broadcasted_iota(jnp.int32, a.shape, 1)
                a = jnp.where(lane < LANES - s, pltpu.roll(a, LANES - s, axis=1),
                              pltpu.roll(b_ref[...], LANES - s, axis=1))
            o_ref[...] = a.astype(o_ref.dtype)


def _pack_small_kernel(*refs, parts):
    o_ref = refs[-1]
    lane = lax.broadcasted_iota(jnp.int32, o_ref.shape, 1)
    out = jnp.zeros(o_ref.shape, F32)
    for ref, (_, src_lane, dst_lane, width) in zip(refs[:-1], parts):
        x = ref[...]
        if (dst_lane - src_lane) % LANES:
            x = pltpu.roll(x, (dst_lane - src_lane) % LANES, axis=1)
        out = jnp.where((lane >= dst_lane) & (lane < dst_lane + width), x, out)
    o_ref[...] = out.astype(o_ref.dtype)


def _pack_w_in(w_stack, li):
    _, rows, cols = w_stack.shape
    sizes = _in_proj_sizes()
    offs = np.concatenate([[0], np.cumsum(sizes)])
    start = {n: int(offs[i]) for i, n in enumerate(_SEC_NAMES)}
    size = {n: int(sizes[i]) for i, n in enumerate(_SEC_NAMES)}
    src = np.concatenate([start[n] + np.arange(0, size[n], LANES) for n in _SEC_ORDER])
    shifts = tuple(sorted(set(int(s) for s in src % LANES)))
    blk = jnp.asarray(src // LANES, jnp.int32)
    cls = jnp.asarray([shifts.index(int(s)) for s in src % LANES], jnp.int32)
    last = (cols - 1) // LANES
    main = pl.pallas_call(
        functools.partial(_pack_cols_kernel, shifts=shifts),
        grid_spec=pltpu.PrefetchScalarGridSpec(
            num_scalar_prefetch=2,
            grid=(len(src),),
            in_specs=[pl.BlockSpec((None, rows, LANES), lambda j, blk, cls: (li, 0, blk[j])),
                      pl.BlockSpec((None, rows, LANES), lambda j, blk, cls: (li, 0, jnp.minimum(blk[j] + 1, last)))],
            out_specs=pl.BlockSpec((rows, LANES), lambda j, blk, cls: (0, j))),
        out_shape=jax.ShapeDtypeStruct((rows, len(src) * LANES), BF16),
        compiler_params=_cparams(("parallel",)),
        name="w_in_pack",
    )(blk, cls, w_stack, w_stack)
    parts, dest = [], 0
    for n in ("fb", "gc"):
        assert start[n] // LANES == (start[n] + size[n] - 1) // LANES
        parts.append((start[n] // LANES, start[n] % LANES, dest, size[n]))
        dest += size[n]
    small = pl.pallas_call(
        functools.partial(_pack_small_kernel, parts=tuple(parts)),
        grid=(1,),
        in_specs=[pl.BlockSpec((None, rows, LANES), lambda i, b=b: (li, 0, b)) for b, _, _, _ in parts],
        out_specs=pl.BlockSpec((rows, LANES), lambda i: (0, 0)),
        out_shape=jax.ShapeDtypeStruct((rows, LANES), BF16),
        compiler_params=_cparams(("arbitrary",)),
        name="w_in_pack_narrow",
    )(*([w_stack] * len(parts)))
    scales = {"qa": (HEAD_DIM // 2) ** -0.5, "qb": HEAD_DIM ** -0.5, "qc": HEAD_DIM ** -0.5, "qd": HEAD_DIM ** -0.5}
    scale_row = jnp.concatenate([jnp.full((1, size[n]), scales.get(n, 1.0), F32) for n in _SEC_ORDER], axis=1)
    return main, small, scale_row


def _rope_tables(t):
    pos = jnp.arange(t, dtype=F32)[:, None]
    tabs = []
    for d in (HEAD_DIM // 2, HEAD_DIM):
        half = d // 2
        inv = ROPE_THETA ** (-jnp.arange(half, dtype=F32) * 2.0 / d)
        ang = pos * inv[None, :]
        cos, sin = jnp.cos(ang), jnp.sin(ang)
        reps = LANES // d
        tabs.append(jnp.tile(jnp.concatenate([cos, cos], axis=1), (1, reps)))
        tabs.append(jnp.tile(jnp.concatenate([-sin, sin], axis=1), (1, reps)))
    return tabs


def kernel(x, p, ffn1_pre_g, ffn1_w_gate, ffn1_w_up, ffn1_w_down, ffn1_post_g, mix_pre_g, w_in, fox_bf, diff_lam_q1, diff_lam_k1, diff_lam_q2, diff_lam_k2, diff_subln_g, nsa_pe_k, nsa_pe_v, nsa_wk1, nsa_wk2, nsa_wv1, nsa_wv2, w_out, mix_post_g, ffn2_pre_g, ffn2_w_gate, ffn2_w_up, ffn2_w_down, ffn2_post_g, ple_pre_g, ple_w_gate, ple_w_proj, ple_post_g):
    b, t, d = x.shape
    depth = w_in.shape[0]
    m = b * t
    tabs = _rope_tables(t)
    h = x.reshape(m, d)
    xn = _rmsnorm(h, ffn1_pre_g[0])

    def ffn(h, xn, li, wg, wu, wd, g_post, g_next):
        act = _gateup(xn, wg, wu, li)
        return _proj_residual(act, _cast_pad(wd, li, row_mult=FF_PAD), h, g_post, g_next, 0.5)

    for li in range(depth):
        h, xn = ffn(h, xn, li, ffn1_w_gate, ffn1_w_up, ffn1_w_down, ffn1_post_g[li], mix_pre_g[li])

        w_main, w_small, scale_row = _pack_w_in(w_in, li)
        z, zs = _inproj(xn, w_main, w_small, scale_row, tabs, t)
        z3 = z.reshape(b, t, z.shape[1])
        bf_row = jnp.pad(fox_bf[li].astype(F32), (0, LANES - N_HEADS_FOX)).reshape(1, LANES)
        c_col, c_row, gates = _small_prep(zs.reshape(b, t, LANES), bf_row)

        lam_init = 0.8 - 0.6 * math.exp(-0.3 * li)
        lam = (jnp.exp(jnp.sum(diff_lam_q1[li].astype(F32) * diff_lam_k1[li].astype(F32)))
               - jnp.exp(jnp.sum(diff_lam_q2[li].astype(F32) * diff_lam_k2[li].astype(F32))) + lam_init)
        o_a = _diff_attention(z3, lam, diff_subln_g[li].astype(F32), lam_init)
        o_b = _fox_attention(z3, c_col, c_row)
        kc, vc = _nsa_compress(z3, nsa_pe_k[li], nsa_pe_v[li], nsa_wk1[li].astype(BF16), nsa_wk2[li].astype(BF16),
                               nsa_wv1[li].astype(BF16), nsa_wv2[li].astype(BF16))
        o_c = _nsa_attention(z3, kc, vc, gates)
        o_d = _dilated_attention(z3)
        o = jnp.concatenate([o_a, o_b, o_c, o_d], axis=-1).reshape(m, -1)
        h, xn = _proj_residual(o, _cast_pad(w_out, li), h, mix_post_g[li], ffn2_pre_g[li], 1.0)

        h, xn = ffn(h, xn, li, ffn2_w_gate, ffn2_w_up, ffn2_w_down, ffn2_post_g[li], ple_pre_g[li])

        g_next = ffn1_pre_g[li + 1] if li + 1 < depth else None
        h, xn = _proj_residual(xn, _cast_pad(ple_w_gate, li), h, ple_post_g[li], g_next, 1.0,
                               p=p[li].reshape(m, -1), wp=ple_w_proj[li].astype(BF16))
    return h.reshape(b, t, d)
```

```python
import functools
import math

import numpy as np
import jax
import jax.numpy as jnp
from jax import lax
from jax.experimental import pallas as pl
from jax.experimental.pallas import tpu as pltpu

F32 = jnp.float32
BF16 = jnp.bfloat16

HEAD_DIM = 128
N_HEADS_DIFF = 8
N_HEADS_FOX = 7
N_HEADS_NSA = 8
N_KV_NSA = 2
N_HEADS_DIL = 9
ROPE_THETA = 10000.0
EPS = 1e-6
CMP_LEN = 32
CMP_STRIDE = 16
SEL_LEN = 64
SEL_TOPN = 16
WIN_LEN = 512
DIL_PATTERNS = ((128, 1), (512, 4), (2048, 16))
NEG_INF = -1e30
POS_BIG = 1e30

V7X_VMEM_LIMIT_BYTES = 60 * 1024 * 1024
LANES = 128
FF_PAD = 512
ATT_TQ = 256
ATT_TK = 256

COL_QA, COL_KA = 0, 8
COL_QC, COL_KCC, COL_KSC, COL_KWC, COL_QD, COL_KD = 16, 24, 26, 28, 30, 39
COL_VA, COL_QB, COL_KB, COL_VB, COL_VCC, COL_VSC, COL_VWC, COL_VD = 48, 56, 63, 70, 77, 79, 81, 83
N_COLBLK = 92
N_ROPE64_BLK = 16
N_ROPE128_BLK = 32
IN_TN = 512


def _cparams(sem):
    return pltpu.CompilerParams(dimension_semantics=sem, vmem_limit_bytes=V7X_VMEM_LIMIT_BYTES)


def _pick(n, prefs):
    for p in prefs:
        if n % p == 0:
            return p
    return n


def _rmsnorm_kernel(x_ref, g_ref, o_ref):
    x = x_ref[...]
    ms = jnp.mean(x * x, axis=-1, keepdims=True)
    o_ref[...] = (x * lax.rsqrt(ms + EPS) * g_ref[...]).astype(o_ref.dtype)


def _rmsnorm(x, g):
    m, d = x.shape
    bm = _pick(m, (256, 128, 64, 32, 16, 8))
    return pl.pallas_call(
        _rmsnorm_kernel,
        grid=(m // bm,),
        in_specs=[pl.BlockSpec((bm, d), lambda i: (i, 0)), pl.BlockSpec((1, d), lambda i: (0, 0))],
        out_specs=pl.BlockSpec((bm, d), lambda i: (i, 0)),
        out_shape=jax.ShapeDtypeStruct((m, d), BF16),
        compiler_params=_cparams(("parallel",)),
        name="rmsnorm",
    )(x, g.reshape(1, d))


def _gateup_kernel(x_ref, wg_ref, wu_ref, o_ref, *, f_valid):
    x = x_ref[...]
    g = jnp.dot(x, wg_ref[...].astype(BF16), preferred_element_type=F32)
    u = jnp.dot(x, wu_ref[...].astype(BF16), preferred_element_type=F32)
    y = g * jax.nn.sigmoid(g) * u
    bf = y.shape[1]
    if f_valid % bf:
        col = pl.program_id(1) * bf + lax.broadcasted_iota(jnp.int32, (1, bf), 1)
        y = jnp.where(col < f_valid, y, 0.0)
    o_ref[...] = y.astype(o_ref.dtype)


def _gateup(xn, wg_stack, wu_stack, li):
    m, d = xn.shape
    f = wg_stack.shape[2]
    f_pad = -(-f // FF_PAD) * FF_PAD
    bm = _pick(m, (2048, 1024, 512, 256, 128))
    bf = _pick(f_pad, (256, 128))
    last = (f - 1) // bf
    w_spec = pl.BlockSpec((None, d, bf), lambda i, j: (li, 0, jnp.minimum(j, last)))
    return pl.pallas_call(
        functools.partial(_gateup_kernel, f_valid=f),
        grid=(m // bm, f_pad // bf),
        in_specs=[pl.BlockSpec((bm, d), lambda i, j: (i, 0)), w_spec, w_spec],
        out_specs=pl.BlockSpec((bm, bf), lambda i, j: (i, j)),
        out_shape=jax.ShapeDtypeStruct((m, f_pad), BF16),
        compiler_params=_cparams(("parallel", "arbitrary")),
        name="ffn_gateup",
    )(xn, wg_stack, wu_stack)


EPI_ROWS = 32


def _residual_epilogue(o_ref, h_ref, gpost_ref, gnext_ref, xn_ref):
    bm = o_ref.shape[0]
    rows = min(EPI_ROWS, bm)

    def body(r, carry):
        sl = pl.ds(pl.multiple_of(r * rows, rows), rows)
        f = o_ref[sl, :]
        hn = h_ref[sl, :] + f * lax.rsqrt(jnp.mean(f * f, axis=-1, keepdims=True) + EPS) * gpost_ref[...]
        o_ref[sl, :] = hn
        if xn_ref is not None:
            ms = jnp.mean(hn * hn, axis=-1, keepdims=True)
            xn_ref[sl, :] = (hn * lax.rsqrt(ms + EPS) * gnext_ref[...]).astype(xn_ref.dtype)
        return carry

    lax.fori_loop(0, bm // rows, body, 0, unroll=2)


def _proj_k_kernel(a_ref, w_ref, h_ref, gpost_ref, gnext_ref, o_ref, *rest, nk):
    k = pl.program_id(1)

    @pl.when(k == 0)
    def _():
        o_ref[...] = jnp.dot(a_ref[...], w_ref[...], preferred_element_type=F32)

    @pl.when(k > 0)
    def _():
        o_ref[...] += jnp.dot(a_ref[...], w_ref[...], preferred_element_type=F32)

    @pl.when(k == nk - 1)
    def _():
        _residual_epilogue(o_ref, h_ref, gpost_ref, gnext_ref, rest[0] if rest else None)


def _proj_w_kernel(*refs, ple, emit_next):
    a_ref, w_ref, h_ref, gpost_ref, gnext_ref = refs[:5]
    refs = refs[5:]
    if ple:
        p_ref, wp_ref = refs[:2]
        refs = refs[2:]
    o_ref = refs[0]
    f = jnp.dot(a_ref[...], w_ref[...], preferred_element_type=F32)
    if ple:
        f = jax.nn.sigmoid(f) * jnp.dot(p_ref[...].astype(BF16), wp_ref[...], preferred_element_type=F32)
    o_ref[...] = f
    _residual_epilogue(o_ref, h_ref, gpost_ref, gnext_ref, refs[1] if emit_next else None)


PROJ_RESIDENT_W_MAX_BYTES = 36 * 1024 * 1024
PROJ_RESIDENT_BUDGET_BYTES = 56 * 1024 * 1024


def _proj_residual(a, w, h, g_post, g_next, coef, p=None, wp=None):
    m, kdim = a.shape
    d = w.shape[1]
    ple = p is not None
    emit_next = g_next is not None
    row_args = [h, (coef * g_post).reshape(1, d), (g_next if emit_next else g_post).reshape(1, d)]
    out_shape = [jax.ShapeDtypeStruct((m, d), F32)] + ([jax.ShapeDtypeStruct((m, d), BF16)] if emit_next else [])
    w_bytes = w.size * w.dtype.itemsize + (wp.size * wp.dtype.itemsize if ple else 0)
    if w_bytes <= PROJ_RESIDENT_W_MAX_BYTES:
        row_bytes = 2 * (2 * kdim + 4 * d + 4 * d + 2 * d) + 4 * d
        bm = _pick(m, tuple(t for t in (512, 256, 128) if t * row_bytes + w_bytes <= PROJ_RESIDENT_BUDGET_BYTES))
        once = dict(pipeline_mode=pl.Buffered(1))
        in_specs = [pl.BlockSpec((bm, kdim), lambda i: (i, 0)),
                    pl.BlockSpec((kdim, d), lambda i: (0, 0), **once),
                    pl.BlockSpec((bm, d), lambda i: (i, 0)),
                    pl.BlockSpec((1, d), lambda i: (0, 0)),
                    pl.BlockSpec((1, d), lambda i: (0, 0))]
        args = [a, w] + row_args
        if ple:
            in_specs += [pl.BlockSpec((bm, p.shape[1]), lambda i: (i, 0)),
                         pl.BlockSpec(wp.shape, lambda i: (0, 0), **once)]
            args += [p, wp]
        res = pl.pallas_call(
            functools.partial(_proj_w_kernel, ple=ple, emit_next=emit_next),
            grid=(m // bm,),
            in_specs=in_specs,
            out_specs=[pl.BlockSpec((bm, d), lambda i: (i, 0))] * len(out_shape),
            out_shape=out_shape,
            compiler_params=_cparams(("parallel",)),
            name="ple_residual" if ple else "proj_residual",
        )(*args)
        return (res[0], res[1]) if emit_next else (res[0], None)
    bm = _pick(m, (512, 256, 128))
    row_specs = [pl.BlockSpec((bm, d), lambda i, k: (i, 0)),
                 pl.BlockSpec((1, d), lambda i, k: (0, 0)),
                 pl.BlockSpec((1, d), lambda i, k: (0, 0))]
    out_specs = [pl.BlockSpec((bm, d), lambda i, k: (i, 0))]
    if emit_next:
        out_specs.append(pl.BlockSpec((bm, d), lambda i, k: (i, 0), pipeline_mode=pl.Buffered(1)))
    assert not ple, "the gated variant needs its weights resident"
    bk = _pick(kdim, (1024, 512, 256, 128))
    nk = kdim // bk
    in_specs = [pl.BlockSpec((bm, bk), lambda i, k: (i, k)),
                pl.BlockSpec((bk, d), lambda i, k: (k, 0))] + row_specs
    args = [a, w] + row_args
    res = pl.pallas_call(
        functools.partial(_proj_k_kernel, nk=nk),
        grid=(m // bm, nk),
        in_specs=in_specs,
        out_specs=out_specs,
        out_shape=out_shape,
        compiler_params=_cparams(("parallel", "arbitrary")),
        name="ple_residual" if ple else "proj_residual",
    )(*args)
    return (res[0], res[1]) if emit_next else (res[0], None)


def _cast_pad_kernel(x_ref, o_ref, *, rows, cols, masked):
    x = x_ref[...]
    if masked:
        br, bc = x.shape
        r = pl.program_id(0) * br + lax.broadcasted_iota(jnp.int32, x.shape, 0)
        c = pl.program_id(1) * bc + lax.broadcasted_iota(jnp.int32, x.shape, 1)
        x = jnp.where((r < rows) & (c < cols), x, 0.0)
    o_ref[...] = x.astype(o_ref.dtype)


def _cast_pad(w_stack, li, row_mult=1, col_mult=1):
    _, rows, cols = w_stack.shape
    rows_p = -(-rows // row_mult) * row_mult
    cols_p = -(-cols // col_mult) * col_mult
    br = _pick(rows_p, (512, 256, 128))
    bc = _pick(cols_p, (1024, 512, 256, 128))
    last_i, last_j = (rows - 1) // br, (cols - 1) // bc
    return pl.pallas_call(
        functools.partial(_cast_pad_kernel, rows=rows, cols=cols, masked=(rows_p, cols_p) != (rows, cols)),
        grid=(rows_p // br, cols_p // bc),
        in_specs=[pl.BlockSpec((None, br, bc), lambda i, j: (li, jnp.minimum(i, last_i), jnp.minimum(j, last_j)))],
        out_specs=pl.BlockSpec((br, bc), lambda i, j: (i, j)),
        out_shape=jax.ShapeDtypeStruct((rows_p, cols_p), BF16),
        compiler_params=_cparams(("parallel", "parallel")),
        name="weight_cast_pad",
    )(w_stack)


def _rope_tile(acc, cos_ref, sin_ref, half, n_slabs):
    cos = cos_ref[...]
    sin = sin_ref[...]
    outs = []
    for s in range(n_slabs):
        a = acc[:, s * LANES:(s + 1) * LANES]
        if half == LANES // 2:
            partner = pltpu.roll(a, LANES // 2, axis=1)
        else:
            lane = lax.broadcasted_iota(jnp.int32, a.shape, 1)
            fwd = pltpu.roll(a, LANES - half, axis=1)
            bwd = pltpu.roll(a, half, axis=1)
            partner = jnp.where((lane & (2 * half - 1)) < half, fwd, bwd)
        outs.append(a * cos + partner * sin)
    return jnp.concatenate(outs, axis=1)


def _inproj_kernel(x_ref, w_ref, ws_ref, scale_ref, c64_ref, s64_ref, c128_ref, s128_ref,
                   o_ref, os_ref, *, n64, n128):
    j = pl.program_id(1)
    n_slabs = w_ref.shape[1] // LANES

    def product():
        return jnp.dot(x_ref[...], w_ref[...], preferred_element_type=F32)

    @pl.when(j == 0)
    def _():
        os_ref[...] = jnp.dot(x_ref[...], ws_ref[...], preferred_element_type=F32)

    @pl.when(j < n64)
    def _():
        o_ref[...] = (_rope_tile(product(), c64_ref, s64_ref, 32, n_slabs) * scale_ref[...]).astype(o_ref.dtype)

    @pl.when((j >= n64) & (j < n64 + n128))
    def _():
        o_ref[...] = (_rope_tile(product(), c128_ref, s128_ref, 64, n_slabs) * scale_ref[...]).astype(o_ref.dtype)

    @pl.when(j >= n64 + n128)
    def _():
        o_ref[...] = (product() * scale_ref[...]).astype(o_ref.dtype)


def _inproj(xn, w_main, w_small, scale_row, tabs, seq):
    m, d = xn.shape
    n = w_main.shape[1]
    bm = _pick(seq, (1024, 512, 256))
    bn = IN_TN
    tpb = seq // bm
    tab_spec = pl.BlockSpec((bm, LANES), lambda i, j: (i % tpb, 0))
    return pl.pallas_call(
        functools.partial(_inproj_kernel, n64=N_ROPE64_BLK * LANES // bn, n128=N_ROPE128_BLK * LANES // bn),
        grid=(m // bm, n // bn),
        in_specs=[pl.BlockSpec((bm, d), lambda i, j: (i, 0)),
                  pl.BlockSpec((d, bn), lambda i, j: (0, j)),
                  pl.BlockSpec((d, LANES), lambda i, j: (0, 0)),
                  pl.BlockSpec((1, bn), lambda i, j: (0, j)),
                  tab_spec, tab_spec, tab_spec, tab_spec],
        out_specs=[pl.BlockSpec((bm, bn), lambda i, j: (i, j)),
                   pl.BlockSpec((bm, LANES), lambda i, j: (i, 0))],
        out_shape=[jax.ShapeDtypeStruct((m, n), BF16), jax.ShapeDtypeStruct((m, LANES), F32)],
        compiler_params=_cparams(("parallel", "arbitrary")),
        name="in_proj_rope",
    )(xn, w_main, w_small, scale_row, *tabs)


def _softmax_terms(q, k_ref, c_lo, c_mask, c_hi, t_rows, mask_fn, bias_fn=None):
    n0, nm, n1 = c_lo * ATT_TK, c_mask * ATT_TK, c_hi * ATT_TK
    s = lax.dot_general(q, k_ref[n0:n1, :], (((1,), (1,)), ((), ())), preferred_element_type=F32)
    if bias_fn is not None:
        s = s + bias_fn(n0, n1)
    kpos = nm + lax.broadcasted_iota(jnp.int32, (1, n1 - nm), 1)
    valid = mask_fn(t_rows, kpos, nm, n1)
    tq, stacked = t_rows.shape[0], q.shape[0] // t_rows.shape[0]
    tail = s[:, nm - n0:].reshape(stacked, tq, n1 - nm)
    tail = jnp.where(valid[None], tail, NEG_INF).reshape(stacked * tq, n1 - nm)
    s = jnp.concatenate([s[:, :nm - n0], tail], axis=1) if nm > n0 else tail
    m = jnp.max(s, axis=1, keepdims=True)
    p = jnp.exp(s - m)
    return p, m, jnp.sum(p, axis=1, keepdims=True)


def _attend(q, k_ref, v_ref, c_lo, c_mask, c_hi, t_rows, mask_fn, bias_fn=None):
    p, m, l = _softmax_terms(q, k_ref, c_lo, c_mask, c_hi, t_rows, mask_fn, bias_fn)
    acc = jnp.dot(p.astype(BF16), v_ref[c_lo * ATT_TK:c_hi * ATT_TK, :], preferred_element_type=F32)
    return m, l, acc


def _causal_mask(t_rows, kpos, n0, n1):
    return kpos <= t_rows


def _row_positions(q0, tq):
    return q0 + lax.broadcasted_iota(jnp.int32, (tq, 1), 0)


def _per_query_tile(n_tiles, fn, qi=None):
    for c in range(n_tiles):
        if qi is None:
            pl.when(pl.program_id(0) >= 0)(functools.partial(fn, c, slice(c * ATT_TQ, (c + 1) * ATT_TQ)))
        else:
            pl.when(qi == c)(functools.partial(fn, c, slice(0, ATT_TQ)))


def _first_chunk(c, reach):
    return max(c - (reach + ATT_TK - 1) // ATT_TK, 0)


def _diff_kernel(lam_ref, q_ref, k_ref, v_ref, g_ref, o_ref, *, out_scale):
    tq = ATT_TQ

    def tile(c, rows):
        q = q_ref[rows, :]
        lane = lax.broadcasted_iota(jnp.int32, q.shape, 1)
        zero = jnp.zeros_like(q)
        q2 = jnp.concatenate([jnp.where(lane < HEAD_DIM // 2, q, zero),
                              jnp.where(lane >= HEAD_DIM // 2, q, zero)], axis=0)
        t_rows = _row_positions(c * tq, tq)
        p, _, l = _softmax_terms(q2, k_ref, 0, c, c + 1, t_rows, _causal_mask)
        r = 1.0 / l
        diff_map = p[:tq] * r[:tq] - p[tq:] * (lam_ref[0] * r[tq:])
        a = jnp.dot(diff_map.astype(BF16), v_ref[:(c + 1) * ATT_TK, :], preferred_element_type=F32)
        y = a * lax.rsqrt(jnp.mean(a * a, axis=-1, keepdims=True) + EPS) * g_ref[...]
        o_ref[rows, :] = (y * out_scale).astype(o_ref.dtype)

    _per_query_tile(k_ref.shape[0] // tq, tile)


def _head_spec(t, col):
    return pl.BlockSpec((None, t, LANES), lambda bi, h: (bi, 0, col + h))


def _diff_attention(z3, lam, subln_g, lam_init):
    b, t, _ = z3.shape
    return pl.pallas_call(
        functools.partial(_diff_kernel, out_scale=1.0 - lam_init),
        grid=(b, N_HEADS_DIFF),
        in_specs=[pl.BlockSpec(memory_space=pltpu.SMEM),
                  _head_spec(t, COL_QA), _head_spec(t, COL_KA), _head_spec(t, COL_VA),
                  pl.BlockSpec((1, LANES), lambda bi, h: (0, 0))],
        out_specs=_head_spec(t, 0),
        out_shape=jax.ShapeDtypeStruct((b, t, N_HEADS_DIFF * HEAD_DIM), BF16),
        compiler_params=_cparams(("parallel", "parallel")),
        name="diff_attention",
    )(lam.reshape(1), z3, z3, z3, subln_g.reshape(1, LANES))


def _small_prep_kernel(zs_ref, bf_ref, c_ref, ct_ref, g_ref):
    zs = zs_ref[...]
    t = zs.shape[0]
    c = jax.nn.log_sigmoid(zs + bf_ref[...])
    row = lax.broadcasted_iota(jnp.int32, c.shape, 0)
    shift = 1
    while shift < t:
        c = c + jnp.where(row >= shift, pltpu.roll(c, shift, axis=0), 0.0)
        shift *= 2
    c_ref[...] = c
    ct_ref[...] = jnp.transpose(c)[:8, :]
    g_ref[...] = jax.nn.sigmoid(zs)


def _small_prep(zs3, bf_row):
    b, t, _ = zs3.shape
    return pl.pallas_call(
        _small_prep_kernel,
        grid=(b,),
        in_specs=[pl.BlockSpec((None, t, LANES), lambda bi: (bi, 0, 0)),
                  pl.BlockSpec((1, LANES), lambda bi: (0, 0))],
        out_specs=[pl.BlockSpec((None, t, LANES), lambda bi: (bi, 0, 0)),
                   pl.BlockSpec((None, 8, t), lambda bi: (bi, 0, 0)),
                   pl.BlockSpec((None, t, LANES), lambda bi: (bi, 0, 0))],
        out_shape=[jax.ShapeDtypeStruct((b, t, LANES), F32),
                   jax.ShapeDtypeStruct((b, 8, t), F32),
                   jax.ShapeDtypeStruct((b, t, LANES), F32)],
        compiler_params=_cparams(("parallel",)),
        name="forget_cumsum_gates",
    )(zs3, bf_row)


def _fox_kernel(q_ref, k_ref, v_ref, c_ref, ct_ref, o_ref):
    h = pl.program_id(1)
    tq = ATT_TQ

    def tile(c, rows):
        cc = c_ref[rows, :]
        lane = lax.broadcasted_iota(jnp.int32, cc.shape, 1)
        cq = jnp.sum(jnp.where(lane == h, cc, 0.0), axis=1, keepdims=True)

        def bias_fn(n0, n1):
            return cq - ct_ref[pl.ds(h, 1), n0:n1]

        t_rows = _row_positions(c * tq, tq)
        _, l, acc = _attend(q_ref[rows, :], k_ref, v_ref, 0, c, c + 1, t_rows, _causal_mask, bias_fn)
        o_ref[rows, :] = (acc / l).astype(o_ref.dtype)

    _per_query_tile(k_ref.shape[0] // tq, tile)


def _fox_attention(z3, c_col, c_row):
    b, t, _ = z3.shape
    return pl.pallas_call(
        _fox_kernel,
        grid=(b, N_HEADS_FOX),
        in_specs=[_head_spec(t, COL_QB), _head_spec(t, COL_KB), _head_spec(t, COL_VB),
                  pl.BlockSpec((None, t, LANES), lambda bi, h: (bi, 0, 0)),
                  pl.BlockSpec((None, 8, t), lambda bi, h: (bi, 0, 0))],
        out_specs=_head_spec(t, 0),
        out_shape=jax.ShapeDtypeStruct((b, t, N_HEADS_FOX * HEAD_DIM), BF16),
        compiler_params=_cparams(("parallel", "parallel")),
        name="forgetting_attention",
    )(z3, z3, z3, c_col, c_row)


def _compress_one(x_ref, xs_ref, pe_ref, w1_ref, w2_ref, o_ref):
    t = x_ref.shape[0]
    nb = t // CMP_STRIDE
    xs_ref[...] = x_ref[...].astype(F32)
    pe = pe_ref[...]
    lo, hi = [], []
    for i in range(CMP_STRIDE):
        xi = xs_ref[pl.ds(i, nb, stride=CMP_STRIDE), :]
        lo.append((xi + pe[i:i + 1, :]).astype(BF16))
        hi.append((xi + pe[CMP_STRIDE + i:CMP_STRIDE + i + 1, :]).astype(BF16))
    half = CMP_STRIDE * HEAD_DIM
    a = jnp.dot(jnp.concatenate(lo, axis=1), w1_ref[:half, :], preferred_element_type=F32)
    bb = jnp.dot(jnp.concatenate(hi, axis=1), w1_ref[half:, :], preferred_element_type=F32)
    y = a + pltpu.roll(bb, nb - 1, axis=0)
    out = jnp.dot(jax.nn.gelu(y).astype(BF16), w2_ref[...], preferred_element_type=F32)
    row = lax.broadcasted_iota(jnp.int32, out.shape, 0)
    o_ref[...] = jnp.where(row < nb - 1, out, 0.0).astype(o_ref.dtype)


def _compress_kernel(k_ref, v_ref, pek_ref, pev_ref, wk1_ref, wk2_ref, wv1_ref, wv2_ref,
                     kc_ref, vc_ref, xs_ref):
    _compress_one(k_ref, xs_ref, pek_ref, wk1_ref, wk2_ref, kc_ref)
    _compress_one(v_ref, xs_ref, pev_ref, wv1_ref, wv2_ref, vc_ref)


def _nsa_compress(z3, pe_k, pe_v, wk1, wk2, wv1, wv2):
    b, t, _ = z3.shape
    nb = t // CMP_STRIDE
    full = lambda a: pl.BlockSpec(a.shape, lambda bi, g: (0,) * a.ndim)
    return pl.pallas_call(
        _compress_kernel,
        grid=(b, N_KV_NSA),
        in_specs=[pl.BlockSpec((None, t, LANES), lambda bi, g: (bi, 0, COL_KCC + g)),
                  pl.BlockSpec((None, t, LANES), lambda bi, g: (bi, 0, COL_VCC + g)),
                  full(pe_k), full(pe_v), full(wk1), full(wk2), full(wv1), full(wv2)],
        out_specs=[pl.BlockSpec((None, None, nb, LANES), lambda bi, g: (bi, g, 0, 0)),
                   pl.BlockSpec((None, None, nb, LANES), lambda bi, g: (bi, g, 0, 0))],
        out_shape=[jax.ShapeDtypeStruct((b, N_KV_NSA, nb, LANES), BF16),
                   jax.ShapeDtypeStruct((b, N_KV_NSA, nb, LANES), BF16)],
        scratch_shapes=[pltpu.VMEM((t, LANES), F32)],
        compiler_params=_cparams(("parallel", "parallel")),
        name="nsa_compress",
    )(z3, z3, pe_k, pe_v, wk1, wk2, wv1, wv2)


def _nsa_tile(c, rows, g, q_ref, kc_ref, vc_ref, ks_ref, vs_ref, kw_ref, vw_ref, gate_ref, o_ref, seq):
    tq = ATT_TQ
    reps = N_HEADS_NSA // N_KV_NSA
    q0 = c * tq
    q4 = jnp.concatenate([q_ref[rows, r * LANES:(r + 1) * LANES] for r in range(reps)], axis=0)
    t_rows = _row_positions(q0, tq)

    nb = kc_ref.shape[0]
    n_cmp = (seq - CMP_LEN) // CMP_STRIDE + 1
    s = lax.dot_general(q4, kc_ref[...], (((1,), (1,)), ((), ())), preferred_element_type=F32)
    cidx = lax.broadcasted_iota(jnp.int32, (1, nb), 1)
    valid = (cidx * CMP_STRIDE + (CMP_LEN - 1) <= t_rows) & (cidx < n_cmp)
    valid = jnp.concatenate([valid] * reps, axis=0)
    s = jnp.where(valid, s, NEG_INF)
    e = jnp.where(valid, jnp.exp(s - jnp.max(s, axis=1, keepdims=True)), 0.0)
    den = jnp.sum(e, axis=1, keepdims=True)
    p_cmp = e / jnp.where(den > 0, den, 1.0)
    o_cmp = jnp.dot(p_cmp.astype(BF16), vc_ref[...], preferred_element_type=F32)

    n_sel = seq // SEL_LEN
    n_sel_pad = LANES
    p_sum = p_cmp[:tq]
    for r in range(1, reps):
        p_sum = p_sum + p_cmp[r * tq:(r + 1) * tq]
    p_hi = p_sum.astype(BF16)
    p_lo = (p_sum - p_hi.astype(F32)).astype(BF16)
    srow = lax.broadcasted_iota(jnp.int32, (n_sel_pad, nb), 0)
    ccol = lax.broadcasted_iota(jnp.int32, (n_sel_pad, nb), 1)
    overlap = ((ccol * CMP_STRIDE < srow * SEL_LEN + SEL_LEN) &
               (ccol * CMP_STRIDE + CMP_LEN > srow * SEL_LEN) & (ccol < n_cmp) & (srow < n_sel))
    ov = jnp.where(overlap, 1.0, 0.0).astype(BF16)
    nt = (((1,), (1,)), ((), ()))
    imp_t = (lax.dot_general(ov, p_hi, nt, preferred_element_type=F32) +
             lax.dot_general(ov, p_lo, nt, preferred_element_type=F32))
    n_rank = ((n_sel + 7) // 8) * 8
    imp_t = imp_t[:n_rank]
    sblk = lax.broadcasted_iota(jnp.int32, (n_rank, tq), 0)
    tpos = q0 + lax.broadcasted_iota(jnp.int32, (n_rank, tq), 1)
    cur = tpos // SEL_LEN
    forced = (sblk == 0) | (sblk == cur) | (sblk == cur - 1)
    val = jnp.where(forced, POS_BIG, jnp.where(sblk * SEL_LEN <= tpos, imp_t, NEG_INF))
    val = jnp.where(sblk < n_sel, val, -2.0 * POS_BIG)
    rank = jnp.zeros((n_rank, tq), F32)
    for s2 in range(n_sel):
        other = val[s2:s2 + 1, :]
        tie = jnp.where(sblk > s2, 1.0, 0.0)
        rank = rank + jnp.where(other > val, 1.0, jnp.where(other == val, tie, 0.0))
    n_top = min(SEL_TOPN, n_sel)
    sel_t = jnp.where((rank < n_top) & (sblk < n_sel), 1.0, 0.0)
    if n_rank < n_sel_pad:
        sel_t = jnp.concatenate([sel_t, jnp.zeros((n_sel_pad - n_rank, tq), F32)], axis=0)
    sel = jnp.transpose(sel_t).astype(BF16)

    def sel_mask(t_r, kpos, n0, n1):
        erow = lax.broadcasted_iota(jnp.int32, (n_sel_pad, n1 - n0), 0)
        ecol = n0 + lax.broadcasted_iota(jnp.int32, (n_sel_pad, n1 - n0), 1)
        expand = jnp.where(ecol // SEL_LEN == erow, 1.0, 0.0).astype(BF16)
        chosen = jnp.dot(sel, expand, preferred_element_type=F32)
        return (chosen > 0.5) & (kpos <= t_r)

    _, l_s, acc_s = _attend(q4, ks_ref, vs_ref, 0, 0, c + 1, t_rows, sel_mask)
    o_sel = acc_s / l_s

    def win_mask(t_r, kpos, n0, n1):
        return (kpos <= t_r) & (kpos > t_r - WIN_LEN)

    lo = _first_chunk(c, WIN_LEN)
    _, l_w, acc_w = _attend(q4, kw_ref, vw_ref, lo, lo, c + 1, t_rows, win_mask)
    o_win = acc_w / l_w

    gates = gate_ref[rows, :]
    lane = lax.broadcasted_iota(jnp.int32, gates.shape, 1)
    n_fb = N_HEADS_FOX
    outs = []
    for r in range(reps):
        col = n_fb + 3 * (g * reps + r)
        rs = slice(r * tq, (r + 1) * tq)
        o_r = jnp.zeros((tq, LANES), F32)
        for br, o_br in enumerate((o_cmp, o_sel, o_win)):
            gcol = jnp.sum(jnp.where(lane == col + br, gates, 0.0), axis=1, keepdims=True)
            o_r = o_r + gcol * o_br[rs]
        outs.append(o_r)
    o_ref[rows, :] = jnp.concatenate(outs, axis=1).astype(o_ref.dtype)


def _nsa_kernel(*refs, seq):
    g = pl.program_id(1)
    _per_query_tile(seq // ATT_TQ, lambda c, rows: _nsa_tile(c, rows, g, *refs, seq), qi=pl.program_id(2))


def _nsa_attention(z3, kc, vc, gates):
    b, t, _ = z3.shape
    tq = ATT_TQ
    reps = N_HEADS_NSA // N_KV_NSA
    nb = kc.shape[2]
    kv = lambda col: pl.BlockSpec((None, t, LANES), lambda bi, g, qi: (bi, 0, col + g))
    return pl.pallas_call(
        functools.partial(_nsa_kernel, seq=t),
        grid=(b, N_KV_NSA, t // tq),
        in_specs=[pl.BlockSpec((None, tq, reps * LANES), lambda bi, g, qi: (bi, qi, COL_QC // reps + g)),
                  pl.BlockSpec((None, None, nb, LANES), lambda bi, g, qi: (bi, g, 0, 0)),
                  pl.BlockSpec((None, None, nb, LANES), lambda bi, g, qi: (bi, g, 0, 0)),
                  kv(COL_KSC), kv(COL_VSC), kv(COL_KWC), kv(COL_VWC),
                  pl.BlockSpec((None, tq, LANES), lambda bi, g, qi: (bi, qi, 0))],
        out_specs=pl.BlockSpec((None, tq, reps * LANES), lambda bi, g, qi: (bi, qi, g)),
        out_shape=jax.ShapeDtypeStruct((b, t, N_HEADS_NSA * HEAD_DIM), BF16),
        compiler_params=_cparams(("parallel", "parallel", "arbitrary")),
        name="nsa_attention",
    )(z3, kc, vc, z3, z3, z3, z3, gates)


def _dil_kernel(*refs):
    n_g = len(DIL_PATTERNS)
    q_refs, k_refs, v_refs = refs[:n_g], refs[n_g:2 * n_g], refs[2 * n_g:3 * n_g]
    o_ref = refs[3 * n_g]
    tq = ATT_TQ

    def tile(c, rows):
        t_rows = _row_positions(c * tq, tq)
        parts = []
        for gi, (w, r) in enumerate(DIL_PATTERNS):
            def mask(t_r, kpos, n0, n1, w=w, r=r):
                dist = t_r - kpos
                return (dist >= 0) & (dist <= w) & ((dist & (r - 1)) == 0)

            lo = _first_chunk(c, w)
            parts.append(_attend(q_refs[gi][rows, :], k_refs[gi], v_refs[gi], lo, lo, c + 1, t_rows, mask))
        lses = [m + jnp.log(l) for (m, l, _) in parts]
        top = functools.reduce(jnp.maximum, lses)
        ws = [jnp.exp(x - top) for x in lses]
        tot = functools.reduce(lambda a, b2: a + b2, ws)
        out = jnp.zeros((tq, LANES), F32)
        for wgt, (m, l, acc) in zip(ws, parts):
            out = out + (wgt / tot) * (acc / l)
        o_ref[rows, :] = out.astype(o_ref.dtype)

    _per_query_tile(k_refs[0].shape[0] // tq, tile)


def _dilated_attention(z3):
    b, t, _ = z3.shape
    n_g = len(DIL_PATTERNS)
    hg = N_HEADS_DIL // n_g
    specs = [_head_spec(t, col + gi * hg) for col in (COL_QD, COL_KD, COL_VD) for gi in range(n_g)]
    return pl.pallas_call(
        _dil_kernel,
        grid=(b, hg),
        in_specs=specs,
        out_specs=_head_spec(t, 0),
        out_shape=jax.ShapeDtypeStruct((b, t, hg * HEAD_DIM), BF16),
        compiler_params=_cparams(("parallel", "parallel")),
        name="dilated_attention",
    )(*([z3] * (3 * n_g)))


def _in_proj_sizes():
    hd = HEAD_DIM
    kv = N_KV_NSA * hd
    return (N_HEADS_DIFF * hd, N_HEADS_DIFF * hd, N_HEADS_DIFF * hd,
            N_HEADS_FOX * hd, N_HEADS_FOX * hd, N_HEADS_FOX * hd, N_HEADS_FOX,
            N_HEADS_NSA * hd, kv, kv, kv, kv, kv, kv, 3 * N_HEADS_NSA,
            N_HEADS_DIL * hd, N_HEADS_DIL * hd, N_HEADS_DIL * hd)


_SEC_NAMES = ("qa", "ka", "va", "qb", "kb", "vb", "fb", "qc", "kcc", "vcc", "ksc", "vsc", "kwc", "vwc", "gc",
              "qd", "kd", "vd")
_SEC_ORDER = ("qa", "ka", "qc", "kcc", "ksc", "kwc", "qd", "kd", "va", "qb", "kb", "vb", "vcc", "vsc", "vwc", "vd")


def _pack_cols_kernel(blk_ref, cls_ref, a_ref, b_ref, o_ref, *, shifts):
    j = pl.program_id(0)
    for ci, s in enumerate(shifts):
        @pl.when(cls_ref[j] == ci)
        def _(s=s):
            a = a_ref[...]
            if s:
                lane = lax.broadcasted_iota(jnp.int32, a.shape, 1)
                a = jnp.where(lane < LANES - s, pltpu.roll(a, LANES - s, axis=1),
                              pltpu.roll(b_ref[...], LANES - s, axis=1))
            o_ref[...] = a.astype(o_ref.dtype)


def _pack_small_kernel(*refs, parts):
    o_ref = refs[-1]
    lane = lax.broadcasted_iota(jnp.int32, o_ref.shape, 1)
    out = jnp.zeros(o_ref.shape, F32)
    for ref, (_, src_lane, dst_lane, width) in zip(refs[:-1], parts):
        x = ref[...]
        if (dst_lane - src_lane) % LANES:
            x = pltpu.roll(x, (dst_lane - src_lane) % LANES, axis=1)
        out = jnp.where((lane >= dst_lane) & (lane < dst_lane + width), x, out)
    o_ref[...] = out.astype(o_ref.dtype)


def _pack_w_in(w_stack, li):
    _, rows, cols = w_stack.shape
    sizes = _in_proj_sizes()
    offs = np.concatenate([[0], np.cumsum(sizes)])
    start = {n: int(offs[i]) for i, n in enumerate(_SEC_NAMES)}
    size = {n: int(sizes[i]) for i, n in enumerate(_SEC_NAMES)}
    src = np.concatenate([start[n] + np.arange(0, size[n], LANES) for n in _SEC_ORDER])
    shifts = tuple(sorted(set(int(s) for s in src % LANES)))
    blk = jnp.asarray(src // LANES, jnp.int32)
    cls = jnp.asarray([shifts.index(int(s)) for s in src % LANES], jnp.int32)
    last = (cols - 1) // LANES
    main = pl.pallas_call(
        functools.partial(_pack_cols_kernel, shifts=shifts),
        grid_spec=pltpu.PrefetchScalarGridSpec(
            num_scalar_prefetch=2,
            grid=(len(src),),
            in_specs=[pl.BlockSpec((None, rows, LANES), lambda j, blk, cls: (li, 0, blk[j])),
                      pl.BlockSpec((None, rows, LANES), lambda j, blk, cls: (li, 0, jnp.minimum(blk[j] + 1, last)))],
            out_specs=pl.BlockSpec((rows, LANES), lambda j, blk, cls: (0, j))),
        out_shape=jax.ShapeDtypeStruct((rows, len(src) * LANES), BF16),
        compiler_params=_cparams(("parallel",)),
        name="w_in_pack",
    )(blk, cls, w_stack, w_stack)
    parts, dest = [], 0
    for n in ("fb", "gc"):
        assert start[n] // LANES == (start[n] + size[n] - 1) // LANES
        parts.append((start[n] // LANES, start[n] % LANES, dest, size[n]))
        dest += size[n]
    small = pl.pallas_call(
        functools.partial(_pack_small_kernel, parts=tuple(parts)),
        grid=(1,),
        in_specs=[pl.BlockSpec((None, rows, LANES), lambda i, b=b: (li, 0, b)) for b, _, _, _ in parts],
        out_specs=pl.BlockSpec((rows, LANES), lambda i: (0, 0)),
        out_shape=jax.ShapeDtypeStruct((rows, LANES), BF16),
        compiler_params=_cparams(("arbitrary",)),
        name="w_in_pack_narrow",
    )(*([w_stack] * len(parts)))
    scales = {"qa": (HEAD_DIM // 2) ** -0.5, "qb": HEAD_DIM ** -0.5, "qc": HEAD_DIM ** -0.5, "qd": HEAD_DIM ** -0.5}
    scale_row = jnp.concatenate([jnp.full((1, size[n]), scales.get(n, 1.0), F32) for n in _SEC_ORDER], axis=1)
    return main, small, scale_row


def _rope_tables(t):
    pos = jnp.arange(t, dtype=F32)[:, None]
    tabs = []
    for d in (HEAD_DIM // 2, HEAD_DIM):
        half = d // 2
        inv = ROPE_THETA ** (-jnp.arange(half, dtype=F32) * 2.0 / d)
        ang = pos * inv[None, :]
        cos, sin = jnp.cos(ang), jnp.sin(ang)
        reps = LANES // d
        tabs.append(jnp.tile(jnp.concatenate([cos, cos], axis=1), (1, reps)))
        tabs.append(jnp.tile(jnp.concatenate([-sin, sin], axis=1), (1, reps)))
    return tabs


def kernel(x, p, ffn1_pre_g, ffn1_w_gate, ffn1_w_up, ffn1_w_down, ffn1_post_g, mix_pre_g, w_in, fox_bf, diff_lam_q1, diff_lam_k1, diff_lam_q2, diff_lam_k2, diff_subln_g, nsa_pe_k, nsa_pe_v, nsa_wk1, nsa_wk2, nsa_wv1, nsa_wv2, w_out, mix_post_g, ffn2_pre_g, ffn2_w_gate, ffn2_w_up, ffn2_w_down, ffn2_post_g, ple_pre_g, ple_w_gate, ple_w_proj, ple_post_g):
    b, t, d = x.shape
    depth = w_in.shape[0]
    m = b * t
    tabs = _rope_tables(t)
    h = x.reshape(m, d)
    xn = _rmsnorm(h, ffn1_pre_g[0])

    def ffn(h, xn, li, wg, wu, wd, g_post, g_next):
        act = _gateup(xn, wg, wu, li)
        return _proj_residual(act, _cast_pad(wd, li, row_mult=FF_PAD), h, g_post, g_next, 0.5)

    for li in range(depth):
        h, xn = ffn(h, xn, li, ffn1_w_gate, ffn1_w_up, ffn1_w_down, ffn1_post_g[li], mix_pre_g[li])

        w_main, w_small, scale_row = _pack_w_in(w_in, li)
        z, zs = _inproj(xn, w_main, w_small, scale_row, tabs, t)
        z3 = z.reshape(b, t, z.shape[1])
        bf_row = jnp.pad(fox_bf[li].astype(F32), (0, LANES - N_HEADS_FOX)).reshape(1, LANES)
        c_col, c_row, gates = _small_prep(zs.reshape(b, t, LANES), bf_row)

        lam_init = 0.8 - 0.6 * math.exp(-0.3 * li)
        lam = (jnp.exp(jnp.sum(diff_lam_q1[li].astype(F32) * diff_lam_k1[li].astype(F32)))
               - jnp.exp(jnp.sum(diff_lam_q2[li].astype(F32) * diff_lam_k2[li].astype(F32))) + lam_init)
        o_a = _diff_attention(z3, lam, diff_subln_g[li].astype(F32), lam_init)
        o_b = _fox_attention(z3, c_col, c_row)
        kc, vc = _nsa_compress(z3, nsa_pe_k[li], nsa_pe_v[li], nsa_wk1[li].astype(BF16), nsa_wk2[li].astype(BF16),
                               nsa_wv1[li].astype(BF16), nsa_wv2[li].astype(BF16))
        o_c = _nsa_attention(z3, kc, vc, gates)
        o_d = _dilated_attention(z3)
        o = jnp.concatenate([o_a, o_b, o_c, o_d], axis=-1).reshape(m, -1)
        h, xn = _proj_residual(o, _cast_pad(w_out, li), h, mix_post_g[li], ffn2_pre_g[li], 1.0)

        h, xn = ffn(h, xn, li, ffn2_w_gate, ffn2_w_up, ffn2_w_down, ffn2_post_g[li], ple_pre_g[li])

        g_next = ffn1_pre_g[li + 1] if li + 1 < depth else None
        h, xn = _proj_residual(xn, _cast_pad(ple_w_gate, li), h, ple_post_g[li], g_next, 1.0,
                               p=p[li].reshape(m, -1), wp=ple_w_proj[li].astype(BF16))
    return h.reshape(b, t, d)
```

```python
import functools
import math

import numpy as np
import jax
import jax.numpy as jnp
from jax import lax
from jax.experimental import pallas as pl
from jax.experimental.pallas import tpu as pltpu

F32 = jnp.float32
BF16 = jnp.bfloat16

HEAD_DIM = 128
N_HEADS_DIFF = 8
N_HEADS_FOX = 7
N_HEADS_NSA = 8
N_KV_NSA = 2
N_HEADS_DIL = 9
ROPE_THETA = 10000.0
EPS = 1e-6
CMP_LEN = 32
CMP_STRIDE = 16
SEL_LEN = 64
SEL_TOPN = 16
WIN_LEN = 512
DIL_PATTERNS = ((128, 1), (512, 4), (2048, 16))
NEG_INF = -1e30
POS_BIG = 1e30

V7X_VMEM_LIMIT_BYTES = 60 * 1024 * 1024
LANES = 128
SUBLANES = 8
FF_PAD = 512
ATT_TQ = 256
ATT_TK = 256

COL_QA, COL_KA = 0, 8
COL_QC, COL_KCC, COL_KSC, COL_KWC, COL_QD, COL_KD = 16, 24, 26, 28, 30, 39
COL_VA, COL_QB, COL_KB, COL_VB, COL_VCC, COL_VSC, COL_VWC, COL_VD = 48, 56, 63, 70, 77, 79, 81, 83
N_COLBLK = 92
N_ROPE64_BLK = 16
N_ROPE128_BLK = 32
IN_TN = 512


def _cparams(sem):
    return pltpu.CompilerParams(dimension_semantics=sem, vmem_limit_bytes=V7X_VMEM_LIMIT_BYTES)


def _pick(n, prefs):
    for p in prefs:
        if n % p == 0:
            return p
    return n


def _rmsnorm_kernel(x_ref, g_ref, o_ref):
    x = x_ref[...]
    ms = jnp.mean(x * x, axis=-1, keepdims=True)
    o_ref[...] = (x * lax.rsqrt(ms + EPS) * g_ref[...]).astype(o_ref.dtype)


def _rmsnorm(x, g):
    m, d = x.shape
    bm = _pick(m, (256, 128, 64, 32, 16, 8))
    return pl.pallas_call(
        _rmsnorm_kernel,
        grid=(m // bm,),
        in_specs=[pl.BlockSpec((bm, d), lambda i: (i, 0)), pl.BlockSpec((1, d), lambda i: (0, 0))],
        out_specs=pl.BlockSpec((bm, d), lambda i: (i, 0)),
        out_shape=jax.ShapeDtypeStruct((m, d), BF16),
        compiler_params=_cparams(("parallel",)),
        name="rmsnorm",
    )(x, g.reshape(1, d))


def _gateup_kernel(x_ref, wg_ref, wu_ref, o_ref, *, f_valid):
    x = x_ref[...]
    g = jnp.dot(x, wg_ref[...].astype(BF16), preferred_element_type=F32)
    u = jnp.dot(x, wu_ref[...].astype(BF16), preferred_element_type=F32)
    y = g * jax.nn.sigmoid(g) * u
    bf = y.shape[1]
    if f_valid % bf:
        col = pl.program_id(1) * bf + lax.broadcasted_iota(jnp.int32, (1, bf), 1)
        y = jnp.where(col < f_valid, y, 0.0)
    o_ref[...] = y.astype(o_ref.dtype)


def _gateup(xn, wg_stack, wu_stack, li):
    m, d = xn.shape
    f = wg_stack.shape[2]
    f_pad = -(-f // FF_PAD) * FF_PAD
    bm = _pick(m, (2048, 1024, 512, 256, 128))
    bf = _pick(f_pad, (256, 128))
    last = (f - 1) // bf
    w_spec = pl.BlockSpec((None, d, bf), lambda i, j: (li, 0, jnp.minimum(j, last)))
    return pl.pallas_call(
        functools.partial(_gateup_kernel, f_valid=f),
        grid=(m // bm, f_pad // bf),
        in_specs=[pl.BlockSpec((bm, d), lambda i, j: (i, 0)), w_spec, w_spec],
        out_specs=pl.BlockSpec((bm, bf), lambda i, j: (i, j)),
        out_shape=jax.ShapeDtypeStruct((m, f_pad), BF16),
        compiler_params=_cparams(("parallel", "arbitrary")),
        name="ffn_gateup",
    )(xn, wg_stack, wu_stack)


EPI_ROWS = 32


def _residual_epilogue(o_ref, h_ref, gpost_ref, gnext_ref, xn_ref):
    bm = o_ref.shape[0]
    rows = min(EPI_ROWS, bm)

    def body(r, carry):
        sl = pl.ds(pl.multiple_of(r * rows, rows), rows)
        f = o_ref[sl, :]
        hn = h_ref[sl, :] + f * lax.rsqrt(jnp.mean(f * f, axis=-1, keepdims=True) + EPS) * gpost_ref[...]
        o_ref[sl, :] = hn
        if xn_ref is not None:
            ms = jnp.mean(hn * hn, axis=-1, keepdims=True)
            xn_ref[sl, :] = (hn * lax.rsqrt(ms + EPS) * gnext_ref[...]).astype(xn_ref.dtype)
        return carry

    lax.fori_loop(0, bm // rows, body, 0, unroll=2)


def _proj_k_kernel(a_ref, w_ref, h_ref, gpost_ref, gnext_ref, o_ref, *rest, nk):
    k = pl.program_id(1)

    @pl.when(k == 0)
    def _():
        o_ref[...] = jnp.dot(a_ref[...], w_ref[...], preferred_element_type=F32)

    @pl.when(k > 0)
    def _():
        o_ref[...] += jnp.dot(a_ref[...], w_ref[...], preferred_element_type=F32)

    @pl.when(k == nk - 1)
    def _():
        _residual_epilogue(o_ref, h_ref, gpost_ref, gnext_ref, rest[0] if rest else None)


def _proj_w_kernel(*refs, ple, emit_next):
    a_ref, w_ref, h_ref, gpost_ref, gnext_ref = refs[:5]
    refs = refs[5:]
    if ple:
        p_ref, wp_ref = refs[:2]
        refs = refs[2:]
    o_ref = refs[0]
    f = jnp.dot(a_ref[...], w_ref[...], preferred_element_type=F32)
    if ple:
        f = jax.nn.sigmoid(f) * jnp.dot(p_ref[...].astype(BF16), wp_ref[...], preferred_element_type=F32)
    o_ref[...] = f
    _residual_epilogue(o_ref, h_ref, gpost_ref, gnext_ref, refs[1] if emit_next else None)


PROJ_RESIDENT_W_MAX_BYTES = 36 * 1024 * 1024
PROJ_RESIDENT_BUDGET_BYTES = 56 * 1024 * 1024


def _proj_residual(a, w, h, g_post, g_next, coef, p=None, wp=None):
    m, kdim = a.shape
    d = w.shape[1]
    ple = p is not None
    emit_next = g_next is not None
    row_args = [h, (coef * g_post).reshape(1, d), (g_next if emit_next else g_post).reshape(1, d)]
    out_shape = [jax.ShapeDtypeStruct((m, d), F32)] + ([jax.ShapeDtypeStruct((m, d), BF16)] if emit_next else [])
    w_bytes = w.size * w.dtype.itemsize + (wp.size * wp.dtype.itemsize if ple else 0)
    if w_bytes <= PROJ_RESIDENT_W_MAX_BYTES:
        row_bytes = 2 * (2 * kdim + 4 * d + 4 * d + 2 * d) + 4 * d
        bm = _pick(m, tuple(t for t in (512, 256, 128) if t * row_bytes + w_bytes <= PROJ_RESIDENT_BUDGET_BYTES))
        once = dict(pipeline_mode=pl.Buffered(1))
        in_specs = [pl.BlockSpec((bm, kdim), lambda i: (i, 0)),
                    pl.BlockSpec((kdim, d), lambda i: (0, 0), **once),
                    pl.BlockSpec((bm, d), lambda i: (i, 0)),
                    pl.BlockSpec((1, d), lambda i: (0, 0)),
                    pl.BlockSpec((1, d), lambda i: (0, 0))]
        args = [a, w] + row_args
        if ple:
            in_specs += [pl.BlockSpec((bm, p.shape[1]), lambda i: (i, 0)),
                         pl.BlockSpec(wp.shape, lambda i: (0, 0), **once)]
            args += [p, wp]
        res = pl.pallas_call(
            functools.partial(_proj_w_kernel, ple=ple, emit_next=emit_next),
            grid=(m // bm,),
            in_specs=in_specs,
            out_specs=[pl.BlockSpec((bm, d), lambda i: (i, 0))] * len(out_shape),
            out_shape=out_shape,
            compiler_params=_cparams(("parallel",)),
            name="ple_residual" if ple else "proj_residual",
        )(*args)
        return (res[0], res[1]) if emit_next else (res[0], None)
    bm = _pick(m, (512, 256, 128))
    row_specs = [pl.BlockSpec((bm, d), lambda i, k: (i, 0)),
                 pl.BlockSpec((1, d), lambda i, k: (0, 0)),
                 pl.BlockSpec((1, d), lambda i, k: (0, 0))]
    out_specs = [pl.BlockSpec((bm, d), lambda i, k: (i, 0))]
    if emit_next:
        out_specs.append(pl.BlockSpec((bm, d), lambda i, k: (i, 0), pipeline_mode=pl.Buffered(1)))
    assert not ple, "the gated variant needs its weights resident"
    bk = _pick(kdim, (1024, 512, 256, 128))
    nk = kdim // bk
    in_specs = [pl.BlockSpec((bm, bk), lambda i, k: (i, k)),
                pl.BlockSpec((bk, d), lambda i, k: (k, 0))] + row_specs
    args = [a, w] + row_args
    res = pl.pallas_call(
        functools.partial(_proj_k_kernel, nk=nk),
        grid=(m // bm, nk),
        in_specs=in_specs,
        out_specs=out_specs,
        out_shape=out_shape,
        compiler_params=_cparams(("parallel", "arbitrary")),
        name="ple_residual" if ple else "proj_residual",
    )(*args)
    return (res[0], res[1]) if emit_next else (res[0], None)


def _cast_pad_kernel(x_ref, o_ref, *, rows, cols, masked):
    x = x_ref[...]
    if masked:
        br, bc = x.shape
        r = pl.program_id(0) * br + lax.broadcasted_iota(jnp.int32, x.shape, 0)
        c = pl.program_id(1) * bc + lax.broadcasted_iota(jnp.int32, x.shape, 1)
        x = jnp.where((r < rows) & (c < cols), x, 0.0)
    o_ref[...] = x.astype(o_ref.dtype)


def _cast_pad(w_stack, li, row_mult=1, col_mult=1):
    _, rows, cols = w_stack.shape
    rows_p = -(-rows // row_mult) * row_mult
    cols_p = -(-cols // col_mult) * col_mult
    br = _pick(rows_p, (1024, 512, 256, 128))
    bc = _pick(cols_p, (1024, 512, 256, 128))
    last_i, last_j = (rows - 1) // br, (cols - 1) // bc
    return pl.pallas_call(
        functools.partial(_cast_pad_kernel, rows=rows, cols=cols, masked=(rows_p, cols_p) != (rows, cols)),
        grid=(rows_p // br, cols_p // bc),
        in_specs=[pl.BlockSpec((None, br, bc), lambda i, j: (li, jnp.minimum(i, last_i), jnp.minimum(j, last_j)))],
        out_specs=pl.BlockSpec((br, bc), lambda i, j: (i, j)),
        out_shape=jax.ShapeDtypeStruct((rows_p, cols_p), BF16),
        compiler_params=_cparams(("parallel", "parallel")),
        name="weight_cast_pad",
    )(w_stack)


def _rope_tile(acc, cos_ref, sin_ref, half, n_slabs):
    cos = cos_ref[...]
    sin = sin_ref[...]
    outs = []
    for s in range(n_slabs):
        a = acc[:, s * LANES:(s + 1) * LANES]
        if half == LANES // 2:
            partner = pltpu.roll(a, LANES // 2, axis=1)
        else:
            lane = lax.broadcasted_iota(jnp.int32, a.shape, 1)
            fwd = pltpu.roll(a, LANES - half, axis=1)
            bwd = pltpu.roll(a, half, axis=1)
            partner = jnp.where((lane & (2 * half - 1)) < half, fwd, bwd)
        outs.append(a * cos + partner * sin)
    return jnp.concatenate(outs, axis=1)


def _inproj_kernel(x_ref, w_ref, ws_ref, scale_ref, c64_ref, s64_ref, c128_ref, s128_ref,
                   o_ref, os_ref, *, n64, n128):
    j = pl.program_id(1)
    n_slabs = w_ref.shape[1] // LANES

    def product():
        return jnp.dot(x_ref[...], w_ref[...], preferred_element_type=F32)

    @pl.when(j == 0)
    def _():
        os_ref[...] = jnp.dot(x_ref[...], ws_ref[...], preferred_element_type=F32)

    @pl.when(j < n64)
    def _():
        o_ref[...] = (_rope_tile(product(), c64_ref, s64_ref, 32, n_slabs) * scale_ref[...]).astype(o_ref.dtype)

    @pl.when((j >= n64) & (j < n64 + n128))
    def _():
        o_ref[...] = (_rope_tile(product(), c128_ref, s128_ref, 64, n_slabs) * scale_ref[...]).astype(o_ref.dtype)

    @pl.when(j >= n64 + n128)
    def _():
        o_ref[...] = (product() * scale_ref[...]).astype(o_ref.dtype)


def _inproj(xn, w_main, w_small, scale_row, tabs, seq):
    m, d = xn.shape
    n = w_main.shape[1]
    bm = _pick(seq, (1024, 512, 256))
    bn = IN_TN
    tpb = seq // bm
    tab_spec = pl.BlockSpec((bm, LANES), lambda i, j: (i % tpb, 0))
    return pl.pallas_call(
        functools.partial(_inproj_kernel, n64=N_ROPE64_BLK * LANES // bn, n128=N_ROPE128_BLK * LANES // bn),
        grid=(m // bm, n // bn),
        in_specs=[pl.BlockSpec((bm, d), lambda i, j: (i, 0)),
                  pl.BlockSpec((d, bn), lambda i, j: (0, j)),
                  pl.BlockSpec((d, LANES), lambda i, j: (0, 0)),
                  pl.BlockSpec((1, bn), lambda i, j: (0, j)),
                  tab_spec, tab_spec, tab_spec, tab_spec],
        out_specs=[pl.BlockSpec((bm, bn), lambda i, j: (i, j)),
                   pl.BlockSpec((bm, LANES), lambda i, j: (i, 0))],
        out_shape=[jax.ShapeDtypeStruct((m, n), BF16), jax.ShapeDtypeStruct((m, LANES), F32)],
        compiler_params=_cparams(("parallel", "arbitrary")),
        name="in_proj_rope",
    )(xn, w_main, w_small, scale_row, *tabs)


def _softmax_terms(q, k_ref, c_lo, c_mask, c_hi, t_rows, mask_fn, bias_fn=None):
    n0, nm, n1 = c_lo * ATT_TK, c_mask * ATT_TK, c_hi * ATT_TK
    s = lax.dot_general(q, k_ref[n0:n1, :], (((1,), (1,)), ((), ())), preferred_element_type=F32)
    if bias_fn is not None:
        s = s + bias_fn(n0, n1)
    kpos = nm + lax.broadcasted_iota(jnp.int32, (1, n1 - nm), 1)
    valid = mask_fn(t_rows, kpos, nm, n1)
    tq, stacked = t_rows.shape[0], q.shape[0] // t_rows.shape[0]
    tail = s[:, nm - n0:].reshape(stacked, tq, n1 - nm)
    tail = jnp.where(valid[None], tail, NEG_INF).reshape(stacked * tq, n1 - nm)
    s = jnp.concatenate([s[:, :nm - n0], tail], axis=1) if nm > n0 else tail
    m = jnp.max(s, axis=1, keepdims=True)
    p = jnp.exp(s - m)
    return p, m, jnp.sum(p, axis=1, keepdims=True)


def _attend(q, k_ref, v_ref, c_lo, c_mask, c_hi, t_rows, mask_fn, bias_fn=None):
    p, m, l = _softmax_terms(q, k_ref, c_lo, c_mask, c_hi, t_rows, mask_fn, bias_fn)
    acc = jnp.dot(p.astype(BF16), v_ref[c_lo * ATT_TK:c_hi * ATT_TK, :], preferred_element_type=F32)
    return m, l, acc


def _causal_mask(t_rows, kpos, n0, n1):
    return kpos <= t_rows


def _row_positions(q0, tq):
    return q0 + lax.broadcasted_iota(jnp.int32, (tq, 1), 0)


def _per_query_tile(n_tiles, fn, qi=None):
    for c in range(n_tiles):
        if qi is None:
            pl.when(pl.program_id(0) >= 0)(functools.partial(fn, c, slice(c * ATT_TQ, (c + 1) * ATT_TQ)))
        else:
            pl.when(qi == c)(functools.partial(fn, c, slice(0, ATT_TQ)))


def _first_chunk(c, reach):
    return max(c - (reach + ATT_TK - 1) // ATT_TK, 0)


def _diff_kernel(lam_ref, q_ref, k_ref, v_ref, g_ref, o_ref, *, out_scale):
    tq = ATT_TQ

    def tile(c, rows):
        q = q_ref[rows, :]
        lane = lax.broadcasted_iota(jnp.int32, q.shape, 1)
        zero = jnp.zeros_like(q)
        q2 = jnp.concatenate([jnp.where(lane < HEAD_DIM // 2, q, zero),
                              jnp.where(lane >= HEAD_DIM // 2, q, zero)], axis=0)
        t_rows = _row_positions(c * tq, tq)
        p, _, l = _softmax_terms(q2, k_ref, 0, c, c + 1, t_rows, _causal_mask)
        r = 1.0 / l
        diff_map = p[:tq] * r[:tq] - p[tq:] * (lam_ref[0] * r[tq:])
        a = jnp.dot(diff_map.astype(BF16), v_ref[:(c + 1) * ATT_TK, :], preferred_element_type=F32)
        y = a * lax.rsqrt(jnp.mean(a * a, axis=-1, keepdims=True) + EPS) * g_ref[...]
        o_ref[rows, :] = (y * out_scale).astype(o_ref.dtype)

    _per_query_tile(k_ref.shape[0] // tq, tile)


def _head_spec(t, col):
    return pl.BlockSpec((None, t, LANES), lambda bi, h: (bi, 0, col + h))


def _diff_attention(z3, lam, subln_g, lam_init):
    b, t, _ = z3.shape
    return pl.pallas_call(
        functools.partial(_diff_kernel, out_scale=1.0 - lam_init),
        grid=(b, N_HEADS_DIFF),
        in_specs=[pl.BlockSpec(memory_space=pltpu.SMEM),
                  _head_spec(t, COL_QA), _head_spec(t, COL_KA), _head_spec(t, COL_VA),
                  pl.BlockSpec((1, LANES), lambda bi, h: (0, 0))],
        out_specs=_head_spec(t, 0),
        out_shape=jax.ShapeDtypeStruct((b, t, N_HEADS_DIFF * HEAD_DIM), BF16),
        compiler_params=_cparams(("parallel", "parallel")),
        name="diff_attention",
    )(lam.reshape(1), z3, z3, z3, subln_g.reshape(1, LANES))


def _small_prep_kernel(zs_ref, bf_ref, c_ref, ct_ref, g_ref):
    zs = zs_ref[...]
    t = zs.shape[0]
    c = jax.nn.log_sigmoid(zs + bf_ref[...])
    row = lax.broadcasted_iota(jnp.int32, c.shape, 0)
    shift = 1
    while shift < t:
        c = c + jnp.where(row >= shift, pltpu.roll(c, shift, axis=0), 0.0)
        shift *= 2
    c_ref[...] = c
    ct_ref[...] = jnp.transpose(c)[:SUBLANES, :]
    g_ref[...] = jax.nn.sigmoid(zs)


def _small_prep(zs3, bf_row):
    b, t, _ = zs3.shape
    return pl.pallas_call(
        _small_prep_kernel,
        grid=(b,),
        in_specs=[pl.BlockSpec((None, t, LANES), lambda bi: (bi, 0, 0)),
                  pl.BlockSpec((1, LANES), lambda bi: (0, 0))],
        out_specs=[pl.BlockSpec((None, t, LANES), lambda bi: (bi, 0, 0)),
                   pl.BlockSpec((None, SUBLANES, t), lambda bi: (bi, 0, 0)),
                   pl.BlockSpec((None, t, LANES), lambda bi: (bi, 0, 0))],
        out_shape=[jax.ShapeDtypeStruct((b, t, LANES), F32),
                   jax.ShapeDtypeStruct((b, SUBLANES, t), F32),
                   jax.ShapeDtypeStruct((b, t, LANES), F32)],
        compiler_params=_cparams(("parallel",)),
        name="forget_cumsum_gates",
    )(zs3, bf_row)


def _fox_kernel(q_ref, k_ref, v_ref, c_ref, ct_ref, o_ref):
    h = pl.program_id(1)
    tq = ATT_TQ

    def tile(c, rows):
        cc = c_ref[rows, :]
        lane = lax.broadcasted_iota(jnp.int32, cc.shape, 1)
        cq = jnp.sum(jnp.where(lane == h, cc, 0.0), axis=1, keepdims=True)

        def bias_fn(n0, n1):
            return cq - ct_ref[pl.ds(h, 1), n0:n1]

        t_rows = _row_positions(c * tq, tq)
        _, l, acc = _attend(q_ref[rows, :], k_ref, v_ref, 0, c, c + 1, t_rows, _causal_mask, bias_fn)
        o_ref[rows, :] = (acc / l).astype(o_ref.dtype)

    _per_query_tile(k_ref.shape[0] // tq, tile)


def _fox_attention(z3, c_col, c_row):
    b, t, _ = z3.shape
    return pl.pallas_call(
        _fox_kernel,
        grid=(b, N_HEADS_FOX),
        in_specs=[_head_spec(t, COL_QB), _head_spec(t, COL_KB), _head_spec(t, COL_VB),
                  pl.BlockSpec((None, t, LANES), lambda bi, h: (bi, 0, 0)),
                  pl.BlockSpec((None, SUBLANES, t), lambda bi, h: (bi, 0, 0))],
        out_specs=_head_spec(t, 0),
        out_shape=jax.ShapeDtypeStruct((b, t, N_HEADS_FOX * HEAD_DIM), BF16),
        compiler_params=_cparams(("parallel", "parallel")),
        name="forgetting_attention",
    )(z3, z3, z3, c_col, c_row)


def _compress_one(x_ref, xs_ref, pe_ref, w1_ref, w2_ref, o_ref):
    t = x_ref.shape[0]
    nb = t // CMP_STRIDE
    xs_ref[...] = x_ref[...].astype(F32)
    pe = pe_ref[...]
    lo, hi = [], []
    for i in range(CMP_STRIDE):
        xi = xs_ref[pl.ds(i, nb, stride=CMP_STRIDE), :]
        lo.append((xi + pe[i:i + 1, :]).astype(BF16))
        hi.append((xi + pe[CMP_STRIDE + i:CMP_STRIDE + i + 1, :]).astype(BF16))
    half = CMP_STRIDE * HEAD_DIM
    a = jnp.dot(jnp.concatenate(lo, axis=1), w1_ref[:half, :], preferred_element_type=F32)
    bb = jnp.dot(jnp.concatenate(hi, axis=1), w1_ref[half:, :], preferred_element_type=F32)
    y = a + pltpu.roll(bb, nb - 1, axis=0)
    out = jnp.dot(jax.nn.gelu(y).astype(BF16), w2_ref[...], preferred_element_type=F32)
    row = lax.broadcasted_iota(jnp.int32, out.shape, 0)
    o_ref[...] = jnp.where(row < nb - 1, out, 0.0).astype(o_ref.dtype)


def _compress_kernel(k_ref, v_ref, pek_ref, pev_ref, wk1_ref, wk2_ref, wv1_ref, wv2_ref,
                     kc_ref, vc_ref, xs_ref):
    _compress_one(k_ref, xs_ref, pek_ref, wk1_ref, wk2_ref, kc_ref)
    _compress_one(v_ref, xs_ref, pev_ref, wv1_ref, wv2_ref, vc_ref)


def _nsa_compress(z3, pe_k, pe_v, wk1, wk2, wv1, wv2):
    b, t, _ = z3.shape
    nb = t // CMP_STRIDE
    full = lambda a: pl.BlockSpec(a.shape, lambda bi, g: (0,) * a.ndim)
    return pl.pallas_call(
        _compress_kernel,
        grid=(b, N_KV_NSA),
        in_specs=[pl.BlockSpec((None, t, LANES), lambda bi, g: (bi, 0, COL_KCC + g)),
                  pl.BlockSpec((None, t, LANES), lambda bi, g: (bi, 0, COL_VCC + g)),
                  full(pe_k), full(pe_v), full(wk1), full(wk2), full(wv1), full(wv2)],
        out_specs=[pl.BlockSpec((None, None, nb, LANES), lambda bi, g: (bi, g, 0, 0)),
                   pl.BlockSpec((None, None, nb, LANES), lambda bi, g: (bi, g, 0, 0))],
        out_shape=[jax.ShapeDtypeStruct((b, N_KV_NSA, nb, LANES), BF16),
                   jax.ShapeDtypeStruct((b, N_KV_NSA, nb, LANES), BF16)],
        scratch_shapes=[pltpu.VMEM((t, LANES), F32)],
        compiler_params=_cparams(("parallel", "parallel")),
        name="nsa_compress",
    )(z3, z3, pe_k, pe_v, wk1, wk2, wv1, wv2)


def _nsa_tile(c, rows, g, q_ref, kc_ref, vc_ref, ks_ref, vs_ref, kw_ref, vw_ref, gate_ref, o_ref, seq):
    tq = ATT_TQ
    reps = N_HEADS_NSA // N_KV_NSA
    q0 = c * tq
    q4 = jnp.concatenate([q_ref[rows, r * LANES:(r + 1) * LANES] for r in range(reps)], axis=0)
    t_rows = _row_positions(q0, tq)

    nb = kc_ref.shape[0]
    n_cmp = (seq - CMP_LEN) // CMP_STRIDE + 1
    s = lax.dot_general(q4, kc_ref[...], (((1,), (1,)), ((), ())), preferred_element_type=F32)
    cidx = lax.broadcasted_iota(jnp.int32, (1, nb), 1)
    valid = (cidx * CMP_STRIDE + (CMP_LEN - 1) <= t_rows) & (cidx < n_cmp)
    valid = jnp.concatenate([valid] * reps, axis=0)
    s = jnp.where(valid, s, NEG_INF)
    e = jnp.where(valid, jnp.exp(s - jnp.max(s, axis=1, keepdims=True)), 0.0)
    den = jnp.sum(e, axis=1, keepdims=True)
    p_cmp = e / jnp.where(den > 0, den, 1.0)
    o_cmp = jnp.dot(p_cmp.astype(BF16), vc_ref[...], preferred_element_type=F32)

    n_sel = seq // SEL_LEN
    n_sel_pad = LANES
    p_sum = p_cmp[:tq]
    for r in range(1, reps):
        p_sum = p_sum + p_cmp[r * tq:(r + 1) * tq]
    p_hi = p_sum.astype(BF16)
    p_lo = (p_sum - p_hi.astype(F32)).astype(BF16)
    srow = lax.broadcasted_iota(jnp.int32, (n_sel_pad, nb), 0)
    ccol = lax.broadcasted_iota(jnp.int32, (n_sel_pad, nb), 1)
    overlap = ((ccol * CMP_STRIDE < srow * SEL_LEN + SEL_LEN) &
               (ccol * CMP_STRIDE + CMP_LEN > srow * SEL_LEN) & (ccol < n_cmp) & (srow < n_sel))
    ov = jnp.where(overlap, 1.0, 0.0).astype(BF16)
    nt = (((1,), (1,)), ((), ()))
    imp_t = (lax.dot_general(ov, p_hi, nt, preferred_element_type=F32) +
             lax.dot_general(ov, p_lo, nt, preferred_element_type=F32))
    n_rank = ((n_sel + 7) // 8) * 8
    imp_t = imp_t[:n_rank]
    sblk = lax.broadcasted_iota(jnp.int32, (n_rank, tq), 0)
    tpos = q0 + lax.broadcasted_iota(jnp.int32, (n_rank, tq), 1)
    cur = tpos // SEL_LEN
    forced = (sblk == 0) | (sblk == cur) | (sblk == cur - 1)
    val = jnp.where(forced, POS_BIG, jnp.where(sblk * SEL_LEN <= tpos, imp_t, NEG_INF))
    val = jnp.where(sblk < n_sel, val, -2.0 * POS_BIG)
    rank = jnp.zeros((n_rank, tq), F32)
    for s2 in range(n_sel):
        other = val[s2:s2 + 1, :]
        tie = jnp.where(sblk > s2, 1.0, 0.0)
        rank = rank + jnp.where(other > val, 1.0, jnp.where(other == val, tie, 0.0))
    n_top = min(SEL_TOPN, n_sel)
    sel_t = jnp.where((rank < n_top) & (sblk < n_sel), 1.0, 0.0)
    if n_rank < n_sel_pad:
        sel_t = jnp.concatenate([sel_t, jnp.zeros((n_sel_pad - n_rank, tq), F32)], axis=0)
    sel = jnp.transpose(sel_t).astype(BF16)

    def sel_mask(t_r, kpos, n0, n1):
        erow = lax.broadcasted_iota(jnp.int32, (n_sel_pad, n1 - n0), 0)
        ecol = n0 + lax.broadcasted_iota(jnp.int32, (n_sel_pad, n1 - n0), 1)
        expand = jnp.where(ecol // SEL_LEN == erow, 1.0, 0.0).astype(BF16)
        chosen = jnp.dot(sel, expand, preferred_element_type=F32)
        return (chosen > 0.5) & (kpos <= t_r)

    _, l_s, acc_s = _attend(q4, ks_ref, vs_ref, 0, 0, c + 1, t_rows, sel_mask)
    o_sel = acc_s / l_s

    def win_mask(t_r, kpos, n0, n1):
        return (kpos <= t_r) & (kpos > t_r - WIN_LEN)

    lo = _first_chunk(c, WIN_LEN)
    _, l_w, acc_w = _attend(q4, kw_ref, vw_ref, lo, lo, c + 1, t_rows, win_mask)
    o_win = acc_w / l_w

    gates = gate_ref[rows, :]
    lane = lax.broadcasted_iota(jnp.int32, gates.shape, 1)
    n_fb = N_HEADS_FOX
    outs = []
    for r in range(reps):
        col = n_fb + 3 * (g * reps + r)
        rs = slice(r * tq, (r + 1) * tq)
        o_r = jnp.zeros((tq, LANES), F32)
        for br, o_br in enumerate((o_cmp, o_sel, o_win)):
            gcol = jnp.sum(jnp.where(lane == col + br, gates, 0.0), axis=1, keepdims=True)
            o_r = o_r + gcol * o_br[rs]
        outs.append(o_r)
    o_ref[rows, :] = jnp.concatenate(outs, axis=1).astype(o_ref.dtype)


def _nsa_kernel(*refs, seq):
    g = pl.program_id(1)
    _per_query_tile(seq // ATT_TQ, lambda c, rows: _nsa_tile(c, rows, g, *refs, seq), qi=pl.program_id(2))


def _nsa_attention(z3, kc, vc, gates):
    b, t, _ = z3.shape
    tq = ATT_TQ
    reps = N_HEADS_NSA // N_KV_NSA
    nb = kc.shape[2]
    kv = lambda col: pl.BlockSpec((None, t, LANES), lambda bi, g, qi: (bi, 0, col + g))
    return pl.pallas_call(
        functools.partial(_nsa_kernel, seq=t),
        grid=(b, N_KV_NSA, t // tq),
        in_specs=[pl.BlockSpec((None, tq, reps * LANES), lambda bi, g, qi: (bi, qi, COL_QC // reps + g)),
                  pl.BlockSpec((None, None, nb, LANES), lambda bi, g, qi: (bi, g, 0, 0)),
                  pl.BlockSpec((None, None, nb, LANES), lambda bi, g, qi: (bi, g, 0, 0)),
                  kv(COL_KSC), kv(COL_VSC), kv(COL_KWC), kv(COL_VWC),
                  pl.BlockSpec((None, tq, LANES), lambda bi, g, qi: (bi, qi, 0))],
        out_specs=pl.BlockSpec((None, tq, reps * LANES), lambda bi, g, qi: (bi, qi, g)),
        out_shape=jax.ShapeDtypeStruct((b, t, N_HEADS_NSA * HEAD_DIM), BF16),
        compiler_params=_cparams(("parallel", "parallel", "arbitrary")),
        name="nsa_attention",
    )(z3, kc, vc, z3, z3, z3, z3, gates)


def _dil_kernel(*refs):
    n_g = len(DIL_PATTERNS)
    q_refs, k_refs, v_refs = refs[:n_g], refs[n_g:2 * n_g], refs[2 * n_g:3 * n_g]
    o_ref = refs[3 * n_g]
    tq = ATT_TQ

    def tile(c, rows):
        t_rows = _row_positions(c * tq, tq)
        parts = []
        for gi, (w, r) in enumerate(DIL_PATTERNS):
            def mask(t_r, kpos, n0, n1, w=w, r=r):
                dist = t_r - kpos
                return (dist >= 0) & (dist <= w) & ((dist & (r - 1)) == 0)

            lo = _first_chunk(c, w)
            parts.append(_attend(q_refs[gi][rows, :], k_refs[gi], v_refs[gi], lo, lo, c + 1, t_rows, mask))
        lses = [m + jnp.log(l) for (m, l, _) in parts]
        top = functools.reduce(jnp.maximum, lses)
        ws = [jnp.exp(x - top) for x in lses]
        tot = functools.reduce(lambda a, b2: a + b2, ws)
        out = jnp.zeros((tq, LANES), F32)
        for wgt, (m, l, acc) in zip(ws, parts):
            out = out + (wgt / tot) * (acc / l)
        o_ref[rows, :] = out.astype(o_ref.dtype)

    _per_query_tile(k_refs[0].shape[0] // tq, tile)


def _dilated_attention(z3):
    b, t, _ = z3.shape
    n_g = len(DIL_PATTERNS)
    hg = N_HEADS_DIL // n_g
    specs = [_head_spec(t, col + gi * hg) for col in (COL_QD, COL_KD, COL_VD) for gi in range(n_g)]
    return pl.pallas_call(
        _dil_kernel,
        grid=(b, hg),
        in_specs=specs,
        out_specs=_head_spec(t, 0),
        out_shape=jax.ShapeDtypeStruct((b, t, hg * HEAD_DIM), BF16),
        compiler_params=_cparams(("parallel", "parallel")),
        name="dilated_attention",
    )(*([z3] * (3 * n_g)))


def _in_proj_sizes():
    hd = HEAD_DIM
    kv = N_KV_NSA * hd
    return (N_HEADS_DIFF * hd, N_HEADS_DIFF * hd, N_HEADS_DIFF * hd,
            N_HEADS_FOX * hd, N_HEADS_FOX * hd, N_HEADS_FOX * hd, N_HEADS_FOX,
            N_HEADS_NSA * hd, kv, kv, kv, kv, kv, kv, 3 * N_HEADS_NSA,
            N_HEADS_DIL * hd, N_HEADS_DIL * hd, N_HEADS_DIL * hd)


_SEC_NAMES = ("qa", "ka", "va", "qb", "kb", "vb", "fb", "qc", "kcc", "vcc", "ksc", "vsc", "kwc", "vwc", "gc",
              "qd", "kd", "vd")
_SEC_ORDER = ("qa", "ka", "qc", "kcc", "ksc", "kwc", "qd", "kd", "va", "qb", "kb", "vb", "vcc", "vsc", "vwc", "vd")


def _pack_cols_kernel(blk_ref, cls_ref, a_ref, b_ref, o_ref, *, shifts):
    j = pl.program_id(0)
    for ci, s in enumerate(shifts):
        @pl.when(cls_ref[j] == ci)
        def _(s=s):
            a = a_ref[...]
            if s:
                lane = lax.broadcasted_iota(jnp.int32, a.shape, 1)
                a = jnp.where(lane < LANES - s, pltpu.roll(a, LANES - s, axis=1),
                              pltpu.roll(b_ref[...], LANES - s, axis=1))
            o_ref[...] = a.astype(o_ref.dtype)


def _pack_small_kernel(*refs, parts):
    o_ref = refs[-1]
    lane = lax.broadcasted_iota(jnp.int32, o_ref.shape, 1)
    out = jnp.zeros(o_ref.shape, F32)
    for ref, (_, src_lane, dst_lane, width) in zip(refs[:-1], parts):
        x = ref[...]
        if (dst_lane - src_lane) % LANES:
            x = pltpu.roll(x, (dst_lane - src_lane) % LANES, axis=1)
        out = jnp.where((lane >= dst_lane) & (lane < dst_lane + width), x, out)
    o_ref[...] = out.astype(o_ref.dtype)


def _pack_w_in(w_stack, li):
    _, rows, cols = w_stack.shape
    sizes = _in_proj_sizes()
    offs = np.concatenate([[0], np.cumsum(sizes)])
    start = {n: int(offs[i]) for i, n in enumerate(_SEC_NAMES)}
    size = {n: int(sizes[i]) for i, n in enumerate(_SEC_NAMES)}
    src = np.concatenate([start[n] + np.arange(0, size[n], LANES) for n in _SEC_ORDER])
    shifts = tuple(sorted(set(int(s) for s in src % LANES)))
    blk = jnp.asarray(src // LANES, jnp.int32)
    cls = jnp.asarray([shifts.index(int(s)) for s in src % LANES], jnp.int32)
    last = (cols - 1) // LANES
    main = pl.pallas_call(
        functools.partial(_pack_cols_kernel, shifts=shifts),
        grid_spec=pltpu.PrefetchScalarGridSpec(
            num_scalar_prefetch=2,
            grid=(len(src),),
            in_specs=[pl.BlockSpec((None, rows, LANES), lambda j, blk, cls: (li, 0, blk[j])),
                      pl.BlockSpec((None, rows, LANES), lambda j, blk, cls: (li, 0, jnp.minimum(blk[j] + 1, last)))],
            out_specs=pl.BlockSpec((rows, LANES), lambda j, blk, cls: (0, j))),
        out_shape=jax.ShapeDtypeStruct((rows, len(src) * LANES), BF16),
        compiler_params=_cparams(("parallel",)),
        name="w_in_pack",
    )(blk, cls, w_stack, w_stack)
    parts, dest = [], 0
    for n in ("fb", "gc"):
        assert start[n] // LANES == (start[n] + size[n] - 1) // LANES
        parts.append((start[n] // LANES, start[n] % LANES, dest, size[n]))
        dest += size[n]
    small = pl.pallas_call(
        functools.partial(_pack_small_kernel, parts=tuple(parts)),
        grid=(1,),
        in_specs=[pl.BlockSpec((None, rows, LANES), lambda i, b=b: (li, 0, b)) for b, _, _, _ in parts],
        out_specs=pl.BlockSpec((rows, LANES), lambda i: (0, 0)),
        out_shape=jax.ShapeDtypeStruct((rows, LANES), BF16),
        compiler_params=_cparams(("arbitrary",)),
        name="w_in_pack_narrow",
    )(*([w_stack] * len(parts)))
    scales = {"qa": (HEAD_DIM // 2) ** -0.5, "qb": HEAD_DIM ** -0.5, "qc": HEAD_DIM ** -0.5, "qd": HEAD_DIM ** -0.5}
    scale_row = jnp.concatenate([jnp.full((1, size[n]), scales.get(n, 1.0), F32) for n in _SEC_ORDER], axis=1)
    return main, small, scale_row


def _rope_tables(t):
    pos = jnp.arange(t, dtype=F32)[:, None]
    tabs = []
    for d in (HEAD_DIM // 2, HEAD_DIM):
        half = d // 2
        inv = ROPE_THETA ** (-jnp.arange(half, dtype=F32) * 2.0 / d)
        ang = pos * inv[None, :]
        cos, sin = jnp.cos(ang), jnp.sin(ang)
        reps = LANES // d
        tabs.append(jnp.tile(jnp.concatenate([cos, cos], axis=1), (1, reps)))
        tabs.append(jnp.tile(jnp.concatenate([-sin, sin], axis=1), (1, reps)))
    return tabs


def kernel(x, p, ffn1_pre_g, ffn1_w_gate, ffn1_w_up, ffn1_w_down, ffn1_post_g, mix_pre_g, w_in, fox_bf, diff_lam_q1, diff_lam_k1, diff_lam_q2, diff_lam_k2, diff_subln_g, nsa_pe_k, nsa_pe_v, nsa_wk1, nsa_wk2, nsa_wv1, nsa_wv2, w_out, mix_post_g, ffn2_pre_g, ffn2_w_gate, ffn2_w_up, ffn2_w_down, ffn2_post_g, ple_pre_g, ple_w_gate, ple_w_proj, ple_post_g):
    b, t, d = x.shape
    depth = w_in.shape[0]
    m = b * t
    tabs = _rope_tables(t)
    h = x.reshape(m, d)
    xn = _rmsnorm(h, ffn1_pre_g[0])

    def ffn(h, xn, li, wg, wu, wd, g_post, g_next):
        act = _gateup(xn, wg, wu, li)
        return _proj_residual(act, _cast_pad(wd, li, row_mult=FF_PAD), h, g_post, g_next, 0.5)

    for li in range(depth):
        h, xn = ffn(h, xn, li, ffn1_w_gate, ffn1_w_up, ffn1_w_down, ffn1_post_g[li], mix_pre_g[li])

        w_main, w_small, scale_row = _pack_w_in(w_in, li)
        z, zs = _inproj(xn, w_main, w_small, scale_row, tabs, t)
        z3 = z.reshape(b, t, z.shape[1])
        bf_row = jnp.pad(fox_bf[li].astype(F32), (0, LANES - N_HEADS_FOX)).reshape(1, LANES)
        c_col, c_row, gates = _small_prep(zs.reshape(b, t, LANES), bf_row)

        lam_init = 0.8 - 0.6 * math.exp(-0.3 * li)
        lam = (jnp.exp(jnp.sum(diff_lam_q1[li].astype(F32) * diff_lam_k1[li].astype(F32)))
               - jnp.exp(jnp.sum(diff_lam_q2[li].astype(F32) * diff_lam_k2[li].astype(F32))) + lam_init)
        o_a = _diff_attention(z3, lam, diff_subln_g[li].astype(F32), lam_init)
        o_b = _fox_attention(z3, c_col, c_row)
        kc, vc = _nsa_compress(z3, nsa_pe_k[li], nsa_pe_v[li], nsa_wk1[li].astype(BF16), nsa_wk2[li].astype(BF16),
                               nsa_wv1[li].astype(BF16), nsa_wv2[li].astype(BF16))
        o_c = _nsa_attention(z3, kc, vc, gates)
        o_d = _dilated_attention(z3)
        o = jnp.concatenate([o_a, o_b, o_c, o_d], axis=-1).reshape(m, -1)
        h, xn = _proj_residual(o, _cast_pad(w_out, li), h, mix_post_g[li], ffn2_pre_g[li], 1.0)

        h, xn = ffn(h, xn, li, ffn2_w_gate, ffn2_w_up, ffn2_w_down, ffn2_post_g[li], ple_pre_g[li])

        g_next = ffn1_pre_g[li + 1] if li + 1 < depth else None
        h, xn = _proj_residual(xn, _cast_pad(ple_w_gate, li), h, ple_post_g[li], g_next, 1.0,
                               p=p[li].reshape(m, -1), wp=ple_w_proj[li].astype(BF16))
    return h.reshape(b, t, d)
```

```python
import functools
import math

import numpy as np
import jax
import jax.numpy as jnp
from jax import lax
from jax.experimental import pallas as pl
from jax.experimental.pallas import tpu as pltpu

F32 = jnp.float32
BF16 = jnp.bfloat16

HEAD_DIM = 128
N_HEADS_DIFF = 8
N_HEADS_FOX = 7
N_HEADS_NSA = 8
N_KV_NSA = 2
N_HEADS_DIL = 9
ROPE_THETA = 10000.0
EPS = 1e-6
CMP_LEN = 32
CMP_STRIDE = 16
SEL_LEN = 64
SEL_TOPN = 16
WIN_LEN = 512
DIL_PATTERNS = ((128, 1), (512, 4), (2048, 16))
NEG_INF = -1e30
POS_BIG = 1e30

V7X_VMEM_LIMIT_BYTES = 60 * 1024 * 1024
LANES = 128
SUBLANES = 8
FF_PAD = 512
ATT_TQ = 256
ATT_TK = 256

COL_QA, COL_KA = 0, 8
COL_QC, COL_KCC, COL_KSC, COL_KWC, COL_QD, COL_KD = 16, 24, 26, 28, 30, 39
COL_VA, COL_QB, COL_KB, COL_VB, COL_VCC, COL_VSC, COL_VWC, COL_VD = 48, 56, 63, 70, 77, 79, 81, 83
N_COLBLK = 92
N_ROPE64_BLK = 16
N_ROPE128_BLK = 32
IN_TN = 512


def _cparams(sem):
    return pltpu.CompilerParams(dimension_semantics=sem, vmem_limit_bytes=V7X_VMEM_LIMIT_BYTES)


def _pick(n, prefs):
    for p in prefs:
        if n % p == 0:
            return p
    return n


def _rmsnorm_kernel(x_ref, g_ref, o_ref):
    x = x_ref[...]
    ms = jnp.mean(x * x, axis=-1, keepdims=True)
    o_ref[...] = (x * lax.rsqrt(ms + EPS) * g_ref[...]).astype(o_ref.dtype)


def _rmsnorm(x, g):
    m, d = x.shape
    bm = _pick(m, (256, 128, 64, 32, 16, 8))
    return pl.pallas_call(
        _rmsnorm_kernel,
        grid=(m // bm,),
        in_specs=[pl.BlockSpec((bm, d), lambda i: (i, 0)), pl.BlockSpec((1, d), lambda i: (0, 0))],
        out_specs=pl.BlockSpec((bm, d), lambda i: (i, 0)),
        out_shape=jax.ShapeDtypeStruct((m, d), BF16),
        compiler_params=_cparams(("parallel",)),
        name="rmsnorm",
    )(x, g.reshape(1, d))


def _gateup_kernel(x_ref, wg_ref, wu_ref, o_ref, *, f_valid):
    x = x_ref[...]
    g = jnp.dot(x, wg_ref[...].astype(BF16), preferred_element_type=F32)
    u = jnp.dot(x, wu_ref[...].astype(BF16), preferred_element_type=F32)
    y = g * jax.nn.sigmoid(g) * u
    bf = y.shape[1]
    if f_valid % bf:
        col = pl.program_id(1) * bf + lax.broadcasted_iota(jnp.int32, (1, bf), 1)
        y = jnp.where(col < f_valid, y, 0.0)
    o_ref[...] = y.astype(o_ref.dtype)


def _gateup(xn, wg_stack, wu_stack, li):
    m, d = xn.shape
    f = wg_stack.shape[2]
    f_pad = -(-f // FF_PAD) * FF_PAD
    bm = _pick(m, (2048, 1024, 512, 256, 128))
    bf = _pick(f_pad, (256, 128))
    last = (f - 1) // bf
    w_spec = pl.BlockSpec((None, d, bf), lambda i, j: (li, 0, jnp.minimum(j, last)))
    return pl.pallas_call(
        functools.partial(_gateup_kernel, f_valid=f),
        grid=(m // bm, f_pad // bf),
        in_specs=[pl.BlockSpec((bm, d), lambda i, j: (i, 0)), w_spec, w_spec],
        out_specs=pl.BlockSpec((bm, bf), lambda i, j: (i, j)),
        out_shape=jax.ShapeDtypeStruct((m, f_pad), BF16),
        compiler_params=_cparams(("parallel", "arbitrary")),
        name="ffn_gateup",
    )(xn, wg_stack, wu_stack)


EPI_ROWS = 32


def _residual_epilogue(o_ref, h_ref, gpost_ref, gnext_ref, xn_ref):
    bm = o_ref.shape[0]
    rows = min(EPI_ROWS, bm)

    def body(r, carry):
        sl = pl.ds(pl.multiple_of(r * rows, rows), rows)
        f = o_ref[sl, :]
        hn = h_ref[sl, :] + f * lax.rsqrt(jnp.mean(f * f, axis=-1, keepdims=True) + EPS) * gpost_ref[...]
        o_ref[sl, :] = hn
        if xn_ref is not None:
            ms = jnp.mean(hn * hn, axis=-1, keepdims=True)
            xn_ref[sl, :] = (hn * lax.rsqrt(ms + EPS) * gnext_ref[...]).astype(xn_ref.dtype)
        return carry

    lax.fori_loop(0, bm // rows, body, 0, unroll=2)


def _proj_k_kernel(a_ref, w_ref, h_ref, gpost_ref, gnext_ref, o_ref, *rest, nk):
    k = pl.program_id(1)

    @pl.when(k == 0)
    def _():
        o_ref[...] = jnp.dot(a_ref[...], w_ref[...], preferred_element_type=F32)

    @pl.when(k > 0)
    def _():
        o_ref[...] += jnp.dot(a_ref[...], w_ref[...], preferred_element_type=F32)

    @pl.when(k == nk - 1)
    def _():
        _residual_epilogue(o_ref, h_ref, gpost_ref, gnext_ref, rest[0] if rest else None)


def _proj_w_kernel(*refs, n_parts, ple, emit_next):
    a_refs, refs = refs[:n_parts], refs[n_parts:]
    w_ref, h_ref, gpost_ref, gnext_ref = refs[:4]
    refs = refs[4:]
    if ple:
        p_ref, wp_ref = refs[:2]
        refs = refs[2:]
    o_ref = refs[0]
    f, row = None, 0
    for a_ref in a_refs:
        k = a_ref.shape[1]
        part = jnp.dot(a_ref[...], w_ref[row:row + k, :], preferred_element_type=F32)
        f = part if f is None else f + part
        row += k
    if ple:
        f = jax.nn.sigmoid(f) * jnp.dot(p_ref[...].astype(BF16), wp_ref[...], preferred_element_type=F32)
    o_ref[...] = f
    _residual_epilogue(o_ref, h_ref, gpost_ref, gnext_ref, refs[1] if emit_next else None)


PROJ_RESIDENT_W_MAX_BYTES = 36 * 1024 * 1024
PROJ_RESIDENT_BUDGET_BYTES = 56 * 1024 * 1024


def _proj_residual(a, w, h, g_post, g_next, coef, p=None, wp=None):
    parts = tuple(a) if isinstance(a, (tuple, list)) else (a,)
    m, kdim = parts[0].shape[0], sum(x.shape[1] for x in parts)
    d = w.shape[1]
    ple = p is not None
    emit_next = g_next is not None
    row_args = [h, (coef * g_post).reshape(1, d), (g_next if emit_next else g_post).reshape(1, d)]
    out_shape = [jax.ShapeDtypeStruct((m, d), F32)] + ([jax.ShapeDtypeStruct((m, d), BF16)] if emit_next else [])
    w_bytes = w.size * w.dtype.itemsize + (wp.size * wp.dtype.itemsize if ple else 0)
    if w_bytes <= PROJ_RESIDENT_W_MAX_BYTES:
        row_bytes = 2 * (2 * kdim + 4 * d + 4 * d + 2 * d) + 4 * d
        bm = _pick(m, tuple(t for t in (512, 256, 128) if t * row_bytes + w_bytes <= PROJ_RESIDENT_BUDGET_BYTES))
        once = dict(pipeline_mode=pl.Buffered(1))
        in_specs = [pl.BlockSpec((bm, x.shape[1]), lambda i: (i, 0)) for x in parts]
        in_specs += [pl.BlockSpec((kdim, d), lambda i: (0, 0), **once),
                    pl.BlockSpec((bm, d), lambda i: (i, 0)),
                    pl.BlockSpec((1, d), lambda i: (0, 0)),
                    pl.BlockSpec((1, d), lambda i: (0, 0))]
        args = list(parts) + [w] + row_args
        if ple:
            in_specs += [pl.BlockSpec((bm, p.shape[1]), lambda i: (i, 0)),
                         pl.BlockSpec(wp.shape, lambda i: (0, 0), **once)]
            args += [p, wp]
        res = pl.pallas_call(
            functools.partial(_proj_w_kernel, n_parts=len(parts), ple=ple, emit_next=emit_next),
            grid=(m // bm,),
            in_specs=in_specs,
            out_specs=[pl.BlockSpec((bm, d), lambda i: (i, 0))] * len(out_shape),
            out_shape=out_shape,
            compiler_params=_cparams(("parallel",)),
            name="ple_residual" if ple else "proj_residual",
        )(*args)
        return (res[0], res[1]) if emit_next else (res[0], None)
    bm = _pick(m, (512, 256, 128))
    row_specs = [pl.BlockSpec((bm, d), lambda i, k: (i, 0)),
                 pl.BlockSpec((1, d), lambda i, k: (0, 0)),
                 pl.BlockSpec((1, d), lambda i, k: (0, 0))]
    out_specs = [pl.BlockSpec((bm, d), lambda i, k: (i, 0))]
    if emit_next:
        out_specs.append(pl.BlockSpec((bm, d), lambda i, k: (i, 0), pipeline_mode=pl.Buffered(1)))
    assert not ple and len(parts) == 1, "gated or column-split inputs need the weights resident"
    a = parts[0]
    bk = _pick(kdim, (1024, 512, 256, 128))
    nk = kdim // bk
    in_specs = [pl.BlockSpec((bm, bk), lambda i, k: (i, k)),
                pl.BlockSpec((bk, d), lambda i, k: (k, 0))] + row_specs
    args = [a, w] + row_args
    res = pl.pallas_call(
        functools.partial(_proj_k_kernel, nk=nk),
        grid=(m // bm, nk),
        in_specs=in_specs,
        out_specs=out_specs,
        out_shape=out_shape,
        compiler_params=_cparams(("parallel", "arbitrary")),
        name="ple_residual" if ple else "proj_residual",
    )(*args)
    return (res[0], res[1]) if emit_next else (res[0], None)


def _cast_pad_kernel(x_ref, o_ref, *, rows, cols, masked):
    x = x_ref[...]
    if masked:
        br, bc = x.shape
        r = pl.program_id(0) * br + lax.broadcasted_iota(jnp.int32, x.shape, 0)
        c = pl.program_id(1) * bc + lax.broadcasted_iota(jnp.int32, x.shape, 1)
        x = jnp.where((r < rows) & (c < cols), x, 0.0)
    o_ref[...] = x.astype(o_ref.dtype)


def _cast_pad(w_stack, li, row_mult=1, col_mult=1):
    _, rows, cols = w_stack.shape
    rows_p = -(-rows // row_mult) * row_mult
    cols_p = -(-cols // col_mult) * col_mult
    br = _pick(rows_p, (1024, 512, 256, 128))
    bc = _pick(cols_p, (1024, 512, 256, 128))
    last_i, last_j = (rows - 1) // br, (cols - 1) // bc
    return pl.pallas_call(
        functools.partial(_cast_pad_kernel, rows=rows, cols=cols, masked=(rows_p, cols_p) != (rows, cols)),
        grid=(rows_p // br, cols_p // bc),
        in_specs=[pl.BlockSpec((None, br, bc), lambda i, j: (li, jnp.minimum(i, last_i), jnp.minimum(j, last_j)))],
        out_specs=pl.BlockSpec((br, bc), lambda i, j: (i, j)),
        out_shape=jax.ShapeDtypeStruct((rows_p, cols_p), BF16),
        compiler_params=_cparams(("parallel", "parallel")),
        name="weight_cast_pad",
    )(w_stack)


def _rope_tile(acc, cos_ref, sin_ref, half, n_slabs):
    cos = cos_ref[...]
    sin = sin_ref[...]
    outs = []
    for s in range(n_slabs):
        a = acc[:, s * LANES:(s + 1) * LANES]
        if half == LANES // 2:
            partner = pltpu.roll(a, LANES // 2, axis=1)
        else:
            lane = lax.broadcasted_iota(jnp.int32, a.shape, 1)
            fwd = pltpu.roll(a, LANES - half, axis=1)
            bwd = pltpu.roll(a, half, axis=1)
            partner = jnp.where((lane & (2 * half - 1)) < half, fwd, bwd)
        outs.append(a * cos + partner * sin)
    return jnp.concatenate(outs, axis=1)


def _inproj_kernel(x_ref, w_ref, ws_ref, scale_ref, c64_ref, s64_ref, c128_ref, s128_ref,
                   o_ref, os_ref, *, n64, n128):
    j = pl.program_id(1)
    n_slabs = w_ref.shape[1] // LANES

    def product():
        return jnp.dot(x_ref[...], w_ref[...], preferred_element_type=F32)

    @pl.when(j == 0)
    def _():
        os_ref[...] = jnp.dot(x_ref[...], ws_ref[...], preferred_element_type=F32)

    @pl.when(j < n64)
    def _():
        o_ref[...] = (_rope_tile(product(), c64_ref, s64_ref, 32, n_slabs) * scale_ref[...]).astype(o_ref.dtype)

    @pl.when((j >= n64) & (j < n64 + n128))
    def _():
        o_ref[...] = (_rope_tile(product(), c128_ref, s128_ref, 64, n_slabs) * scale_ref[...]).astype(o_ref.dtype)

    @pl.when(j >= n64 + n128)
    def _():
        o_ref[...] = (product() * scale_ref[...]).astype(o_ref.dtype)


def _inproj(xn, w_main, w_small, scale_row, tabs, seq):
    m, d = xn.shape
    n = w_main.shape[1]
    bm = _pick(seq, (1024, 512, 256))
    bn = IN_TN
    tpb = seq // bm
    tab_spec = pl.BlockSpec((bm, LANES), lambda i, j: (i % tpb, 0))
    return pl.pallas_call(
        functools.partial(_inproj_kernel, n64=N_ROPE64_BLK * LANES // bn, n128=N_ROPE128_BLK * LANES // bn),
        grid=(m // bm, n // bn),
        in_specs=[pl.BlockSpec((bm, d), lambda i, j: (i, 0)),
                  pl.BlockSpec((d, bn), lambda i, j: (0, j)),
                  pl.BlockSpec((d, LANES), lambda i, j: (0, 0)),
                  pl.BlockSpec((1, bn), lambda i, j: (0, j)),
                  tab_spec, tab_spec, tab_spec, tab_spec],
        out_specs=[pl.BlockSpec((bm, bn), lambda i, j: (i, j)),
                   pl.BlockSpec((bm, LANES), lambda i, j: (i, 0))],
        out_shape=[jax.ShapeDtypeStruct((m, n), BF16), jax.ShapeDtypeStruct((m, LANES), F32)],
        compiler_params=_cparams(("parallel", "arbitrary")),
        name="in_proj_rope",
    )(xn, w_main, w_small, scale_row, *tabs)


def _softmax_terms(q, k_ref, c_lo, c_mask, c_hi, t_rows, mask_fn, bias_fn=None):
    n0, nm, n1 = c_lo * ATT_TK, c_mask * ATT_TK, c_hi * ATT_TK
    s = lax.dot_general(q, k_ref[n0:n1, :], (((1,), (1,)), ((), ())), preferred_element_type=F32)
    if bias_fn is not None:
        s = s + bias_fn(n0, n1)
    kpos = nm + lax.broadcasted_iota(jnp.int32, (1, n1 - nm), 1)
    valid = mask_fn(t_rows, kpos, nm, n1)
    tq, stacked = t_rows.shape[0], q.shape[0] // t_rows.shape[0]
    tail = s[:, nm - n0:].reshape(stacked, tq, n1 - nm)
    tail = jnp.where(valid[None], tail, NEG_INF).reshape(stacked * tq, n1 - nm)
    s = jnp.concatenate([s[:, :nm - n0], tail], axis=1) if nm > n0 else tail
    m = jnp.max(s, axis=1, keepdims=True)
    p = jnp.exp(s - m)
    return p, m, jnp.sum(p, axis=1, keepdims=True)


def _attend(q, k_ref, v_ref, c_lo, c_mask, c_hi, t_rows, mask_fn, bias_fn=None):
    p, m, l = _softmax_terms(q, k_ref, c_lo, c_mask, c_hi, t_rows, mask_fn, bias_fn)
    acc = jnp.dot(p.astype(BF16), v_ref[c_lo * ATT_TK:c_hi * ATT_TK, :], preferred_element_type=F32)
    return m, l, acc


def _causal_mask(t_rows, kpos, n0, n1):
    return kpos <= t_rows


def _row_positions(q0, tq):
    return q0 + lax.broadcasted_iota(jnp.int32, (tq, 1), 0)


def _per_query_tile(n_tiles, fn, qi=None):
    for c in range(n_tiles):
        if qi is None:
            pl.when(pl.program_id(0) >= 0)(functools.partial(fn, c, slice(c * ATT_TQ, (c + 1) * ATT_TQ)))
        else:
            pl.when(qi == c)(functools.partial(fn, c, slice(0, ATT_TQ)))


def _first_chunk(c, reach):
    return max(c - (reach + ATT_TK - 1) // ATT_TK, 0)


def _diff_kernel(lam_ref, q_ref, k_ref, v_ref, g_ref, o_ref, *, out_scale):
    tq = ATT_TQ

    def tile(c, rows):
        q = q_ref[rows, :]
        lane = lax.broadcasted_iota(jnp.int32, q.shape, 1)
        zero = jnp.zeros_like(q)
        q2 = jnp.concatenate([jnp.where(lane < HEAD_DIM // 2, q, zero),
                              jnp.where(lane >= HEAD_DIM // 2, q, zero)], axis=0)
        t_rows = _row_positions(c * tq, tq)
        p, _, l = _softmax_terms(q2, k_ref, 0, c, c + 1, t_rows, _causal_mask)
        r = 1.0 / l
        diff_map = p[:tq] * r[:tq] - p[tq:] * (lam_ref[0] * r[tq:])
        a = jnp.dot(diff_map.astype(BF16), v_ref[:(c + 1) * ATT_TK, :], preferred_element_type=F32)
        y = a * lax.rsqrt(jnp.mean(a * a, axis=-1, keepdims=True) + EPS) * g_ref[...]
        o_ref[rows, :] = (y * out_scale).astype(o_ref.dtype)

    _per_query_tile(k_ref.shape[0] // tq, tile)


def _head_spec(t, col):
    return pl.BlockSpec((None, t, LANES), lambda bi, h: (bi, 0, col + h))


def _diff_attention(z3, lam, subln_g, lam_init):
    b, t, _ = z3.shape
    return pl.pallas_call(
        functools.partial(_diff_kernel, out_scale=1.0 - lam_init),
        grid=(b, N_HEADS_DIFF),
        in_specs=[pl.BlockSpec(memory_space=pltpu.SMEM),
                  _head_spec(t, COL_QA), _head_spec(t, COL_KA), _head_spec(t, COL_VA),
                  pl.BlockSpec((1, LANES), lambda bi, h: (0, 0))],
        out_specs=_head_spec(t, 0),
        out_shape=jax.ShapeDtypeStruct((b, t, N_HEADS_DIFF * HEAD_DIM), BF16),
        compiler_params=_cparams(("parallel", "parallel")),
        name="diff_attention",
    )(lam.reshape(1), z3, z3, z3, subln_g.reshape(1, LANES))


def _small_prep_kernel(zs_ref, bf_ref, c_ref, ct_ref, g_ref):
    zs = zs_ref[...]
    t = zs.shape[0]
    c = jax.nn.log_sigmoid(zs + bf_ref[...])
    row = lax.broadcasted_iota(jnp.int32, c.shape, 0)
    shift = 1
    while shift < t:
        c = c + jnp.where(row >= shift, pltpu.roll(c, shift, axis=0), 0.0)
        shift *= 2
    c_ref[...] = c
    ct_ref[...] = jnp.transpose(c)[:SUBLANES, :]
    g_ref[...] = jax.nn.sigmoid(zs)


def _small_prep(zs3, bf_row):
    b, t, _ = zs3.shape
    return pl.pallas_call(
        _small_prep_kernel,
        grid=(b,),
        in_specs=[pl.BlockSpec((None, t, LANES), lambda bi: (bi, 0, 0)),
                  pl.BlockSpec((1, LANES), lambda bi: (0, 0))],
        out_specs=[pl.BlockSpec((None, t, LANES), lambda bi: (bi, 0, 0)),
                   pl.BlockSpec((None, SUBLANES, t), lambda bi: (bi, 0, 0)),
                   pl.BlockSpec((None, t, LANES), lambda bi: (bi, 0, 0))],
        out_shape=[jax.ShapeDtypeStruct((b, t, LANES), F32),
                   jax.ShapeDtypeStruct((b, SUBLANES, t), F32),
                   jax.ShapeDtypeStruct((b, t, LANES), F32)],
        compiler_params=_cparams(("parallel",)),
        name="forget_cumsum_gates",
    )(zs3, bf_row)


def _fox_kernel(q_ref, k_ref, v_ref, c_ref, ct_ref, o_ref):
    h = pl.program_id(1)
    tq = ATT_TQ

    def tile(c, rows):
        cc = c_ref[rows, :]
        lane = lax.broadcasted_iota(jnp.int32, cc.shape, 1)
        cq = jnp.sum(jnp.where(lane == h, cc, 0.0), axis=1, keepdims=True)

        def bias_fn(n0, n1):
            return cq - ct_ref[pl.ds(h, 1), n0:n1]

        t_rows = _row_positions(c * tq, tq)
        _, l, acc = _attend(q_ref[rows, :], k_ref, v_ref, 0, c, c + 1, t_rows, _causal_mask, bias_fn)
        o_ref[rows, :] = (acc / l).astype(o_ref.dtype)

    _per_query_tile(k_ref.shape[0] // tq, tile)


def _fox_attention(z3, c_col, c_row):
    b, t, _ = z3.shape
    return pl.pallas_call(
        _fox_kernel,
        grid=(b, N_HEADS_FOX),
        in_specs=[_head_spec(t, COL_QB), _head_spec(t, COL_KB), _head_spec(t, COL_VB),
                  pl.BlockSpec((None, t, LANES), lambda bi, h: (bi, 0, 0)),
                  pl.BlockSpec((None, SUBLANES, t), lambda bi, h: (bi, 0, 0))],
        out_specs=_head_spec(t, 0),
        out_shape=jax.ShapeDtypeStruct((b, t, N_HEADS_FOX * HEAD_DIM), BF16),
        compiler_params=_cparams(("parallel", "parallel")),
        name="forgetting_attention",
    )(z3, z3, z3, c_col, c_row)


def _compress_one(x_ref, xs_ref, pe_ref, w1_ref, w2_ref, o_ref):
    t = x_ref.shape[0]
    nb = t // CMP_STRIDE
    xs_ref[...] = x_ref[...].astype(F32)
    pe = pe_ref[...]
    lo, hi = [], []
    for i in range(CMP_STRIDE):
        xi = xs_ref[pl.ds(i, nb, stride=CMP_STRIDE), :]
        lo.append((xi + pe[i:i + 1, :]).astype(BF16))
        hi.append((xi + pe[CMP_STRIDE + i:CMP_STRIDE + i + 1, :]).astype(BF16))
    half = CMP_STRIDE * HEAD_DIM
    a = jnp.dot(jnp.concatenate(lo, axis=1), w1_ref[:half, :], preferred_element_type=F32)
    bb = jnp.dot(jnp.concatenate(hi, axis=1), w1_ref[half:, :], preferred_element_type=F32)
    y = a + pltpu.roll(bb, nb - 1, axis=0)
    out = jnp.dot(jax.nn.gelu(y).astype(BF16), w2_ref[...], preferred_element_type=F32)
    row = lax.broadcasted_iota(jnp.int32, out.shape, 0)
    o_ref[...] = jnp.where(row < nb - 1, out, 0.0).astype(o_ref.dtype)


def _compress_kernel(k_ref, v_ref, pek_ref, pev_ref, wk1_ref, wk2_ref, wv1_ref, wv2_ref,
                     kc_ref, vc_ref, xs_ref):
    _compress_one(k_ref, xs_ref, pek_ref, wk1_ref, wk2_ref, kc_ref)
    _compress_one(v_ref, xs_ref, pev_ref, wv1_ref, wv2_ref, vc_ref)


def _nsa_compress(z3, pe_k, pe_v, wk1, wk2, wv1, wv2):
    b, t, _ = z3.shape
    nb = t // CMP_STRIDE
    full = lambda a: pl.BlockSpec(a.shape, lambda bi, g: (0,) * a.ndim)
    return pl.pallas_call(
        _compress_kernel,
        grid=(b, N_KV_NSA),
        in_specs=[pl.BlockSpec((None, t, LANES), lambda bi, g: (bi, 0, COL_KCC + g)),
                  pl.BlockSpec((None, t, LANES), lambda bi, g: (bi, 0, COL_VCC + g)),
                  full(pe_k), full(pe_v), full(wk1), full(wk2), full(wv1), full(wv2)],
        out_specs=[pl.BlockSpec((None, None, nb, LANES), lambda bi, g: (bi, g, 0, 0)),
                   pl.BlockSpec((None, None, nb, LANES), lambda bi, g: (bi, g, 0, 0))],
        out_shape=[jax.ShapeDtypeStruct((b, N_KV_NSA, nb, LANES), BF16),
                   jax.ShapeDtypeStruct((b, N_KV_NSA, nb, LANES), BF16)],
        scratch_shapes=[pltpu.VMEM((t, LANES), F32)],
        compiler_params=_cparams(("parallel", "parallel")),
        name="nsa_compress",
    )(z3, z3, pe_k, pe_v, wk1, wk2, wv1, wv2)


def _nsa_tile(c, rows, g, q_ref, kc_ref, vc_ref, ks_ref, vs_ref, kw_ref, vw_ref, gate_ref, o_ref, seq):
    tq = ATT_TQ
    reps = N_HEADS_NSA // N_KV_NSA
    q0 = c * tq
    q4 = jnp.concatenate([q_ref[rows, r * LANES:(r + 1) * LANES] for r in range(reps)], axis=0)
    t_rows = _row_positions(q0, tq)

    nb = kc_ref.shape[0]
    n_cmp = (seq - CMP_LEN) // CMP_STRIDE + 1
    s = lax.dot_general(q4, kc_ref[...], (((1,), (1,)), ((), ())), preferred_element_type=F32)
    cidx = lax.broadcasted_iota(jnp.int32, (1, nb), 1)
    valid = (cidx * CMP_STRIDE + (CMP_LEN - 1) <= t_rows) & (cidx < n_cmp)
    valid = jnp.concatenate([valid] * reps, axis=0)
    s = jnp.where(valid, s, NEG_INF)
    e = jnp.where(valid, jnp.exp(s - jnp.max(s, axis=1, keepdims=True)), 0.0)
    den = jnp.sum(e, axis=1, keepdims=True)
    p_cmp = e / jnp.where(den > 0, den, 1.0)
    o_cmp = jnp.dot(p_cmp.astype(BF16), vc_ref[...], preferred_element_type=F32)

    n_sel = seq // SEL_LEN
    n_sel_pad = LANES
    p_sum = p_cmp[:tq]
    for r in range(1, reps):
        p_sum = p_sum + p_cmp[r * tq:(r + 1) * tq]
    p_hi = p_sum.astype(BF16)
    p_lo = (p_sum - p_hi.astype(F32)).astype(BF16)
    srow = lax.broadcasted_iota(jnp.int32, (n_sel_pad, nb), 0)
    ccol = lax.broadcasted_iota(jnp.int32, (n_sel_pad, nb), 1)
    overlap = ((ccol * CMP_STRIDE < srow * SEL_LEN + SEL_LEN) &
               (ccol * CMP_STRIDE + CMP_LEN > srow * SEL_LEN) & (ccol < n_cmp) & (srow < n_sel))
    ov = jnp.where(overlap, 1.0, 0.0).astype(BF16)
    nt = (((1,), (1,)), ((), ()))
    imp_t = (lax.dot_general(ov, p_hi, nt, preferred_element_type=F32) +
             lax.dot_general(ov, p_lo, nt, preferred_element_type=F32))
    n_rank = ((n_sel + 7) // 8) * 8
    imp_t = imp_t[:n_rank]
    sblk = lax.broadcasted_iota(jnp.int32, (n_rank, tq), 0)
    tpos = q0 + lax.broadcasted_iota(jnp.int32, (n_rank, tq), 1)
    cur = tpos // SEL_LEN
    forced = (sblk == 0) | (sblk == cur) | (sblk == cur - 1)
    val = jnp.where(forced, POS_BIG, jnp.where(sblk * SEL_LEN <= tpos, imp_t, NEG_INF))
    val = jnp.where(sblk < n_sel, val, -2.0 * POS_BIG)
    rank = jnp.zeros((n_rank, tq), F32)
    for s2 in range(n_sel):
        other = val[s2:s2 + 1, :]
        tie = jnp.where(sblk > s2, 1.0, 0.0)
        rank = rank + jnp.where(other > val, 1.0, jnp.where(other == val, tie, 0.0))
    n_top = min(SEL_TOPN, n_sel)
    sel_t = jnp.where((rank < n_top) & (sblk < n_sel), 1.0, 0.0)
    if n_rank < n_sel_pad:
        sel_t = jnp.concatenate([sel_t, jnp.zeros((n_sel_pad - n_rank, tq), F32)], axis=0)
    sel = jnp.transpose(sel_t).astype(BF16)

    def sel_mask(t_r, kpos, n0, n1):
        erow = lax.broadcasted_iota(jnp.int32, (n_sel_pad, n1 - n0), 0)
        ecol = n0 + lax.broadcasted_iota(jnp.int32, (n_sel_pad, n1 - n0), 1)
        expand = jnp.where(ecol // SEL_LEN == erow, 1.0, 0.0).astype(BF16)
        chosen = jnp.dot(sel, expand, preferred_element_type=F32)
        return (chosen > 0.5) & (kpos <= t_r)

    _, l_s, acc_s = _attend(q4, ks_ref, vs_ref, 0, 0, c + 1, t_rows, sel_mask)
    o_sel = acc_s / l_s

    def win_mask(t_r, kpos, n0, n1):
        return (kpos <= t_r) & (kpos > t_r - WIN_LEN)

    lo = _first_chunk(c, WIN_LEN)
    _, l_w, acc_w = _attend(q4, kw_ref, vw_ref, lo, lo, c + 1, t_rows, win_mask)
    o_win = acc_w / l_w

    gates = gate_ref[rows, :]
    lane = lax.broadcasted_iota(jnp.int32, gates.shape, 1)
    n_fb = N_HEADS_FOX
    outs = []
    for r in range(reps):
        col = n_fb + 3 * (g * reps + r)
        rs = slice(r * tq, (r + 1) * tq)
        o_r = jnp.zeros((tq, LANES), F32)
        for br, o_br in enumerate((o_cmp, o_sel, o_win)):
            gcol = jnp.sum(jnp.where(lane == col + br, gates, 0.0), axis=1, keepdims=True)
            o_r = o_r + gcol * o_br[rs]
        outs.append(o_r)
    o_ref[rows, :] = jnp.concatenate(outs, axis=1).astype(o_ref.dtype)


def _nsa_kernel(*refs, seq):
    g = pl.program_id(1)
    _per_query_tile(seq // ATT_TQ, lambda c, rows: _nsa_tile(c, rows, g, *refs, seq), qi=pl.program_id(2))


def _nsa_attention(z3, kc, vc, gates):
    b, t, _ = z3.shape
    tq = ATT_TQ
    reps = N_HEADS_NSA // N_KV_NSA
    nb = kc.shape[2]
    kv = lambda col: pl.BlockSpec((None, t, LANES), lambda bi, g, qi: (bi, 0, col + g))
    return pl.pallas_call(
        functools.partial(_nsa_kernel, seq=t),
        grid=(b, N_KV_NSA, t // tq),
        in_specs=[pl.BlockSpec((None, tq, reps * LANES), lambda bi, g, qi: (bi, qi, COL_QC // reps + g)),
                  pl.BlockSpec((None, None, nb, LANES), lambda bi, g, qi: (bi, g, 0, 0)),
                  pl.BlockSpec((None, None, nb, LANES), lambda bi, g, qi: (bi, g, 0, 0)),
                  kv(COL_KSC), kv(COL_VSC), kv(COL_KWC), kv(COL_VWC),
                  pl.BlockSpec((None, tq, LANES), lambda bi, g, qi: (bi, qi, 0))],
        out_specs=pl.BlockSpec((None, tq, reps * LANES), lambda bi, g, qi: (bi, qi, g)),
        out_shape=jax.ShapeDtypeStruct((b, t, N_HEADS_NSA * HEAD_DIM), BF16),
        compiler_params=_cparams(("parallel", "parallel", "arbitrary")),
        name="nsa_attention",
    )(z3, kc, vc, z3, z3, z3, z3, gates)


def _dil_kernel(*refs):
    n_g = len(DIL_PATTERNS)
    q_refs, k_refs, v_refs = refs[:n_g], refs[n_g:2 * n_g], refs[2 * n_g:3 * n_g]
    o_ref = refs[3 * n_g]
    tq = ATT_TQ

    def tile(c, rows):
        t_rows = _row_positions(c * tq, tq)
        parts = []
        for gi, (w, r) in enumerate(DIL_PATTERNS):
            def mask(t_r, kpos, n0, n1, w=w, r=r):
                dist = t_r - kpos
                return (dist >= 0) & (dist <= w) & ((dist & (r - 1)) == 0)

            lo = _first_chunk(c, w)
            parts.append(_attend(q_refs[gi][rows, :], k_refs[gi], v_refs[gi], lo, lo, c + 1, t_rows, mask))
        lses = [m + jnp.log(l) for (m, l, _) in parts]
        top = functools.reduce(jnp.maximum, lses)
        ws = [jnp.exp(x - top) for x in lses]
        tot = functools.reduce(lambda a, b2: a + b2, ws)
        out = jnp.zeros((tq, LANES), F32)
        for wgt, (m, l, acc) in zip(ws, parts):
            out = out + (wgt / tot) * (acc / l)
        o_ref[rows, :] = out.astype(o_ref.dtype)

    _per_query_tile(k_refs[0].shape[0] // tq, tile)


def _dilated_attention(z3):
    b, t, _ = z3.shape
    n_g = len(DIL_PATTERNS)
    hg = N_HEADS_DIL // n_g
    specs = [_head_spec(t, col + gi * hg) for col in (COL_QD, COL_KD, COL_VD) for gi in range(n_g)]
    return pl.pallas_call(
        _dil_kernel,
        grid=(b, hg),
        in_specs=specs,
        out_specs=_head_spec(t, 0),
        out_shape=jax.ShapeDtypeStruct((b, t, hg * HEAD_DIM), BF16),
        compiler_params=_cparams(("parallel", "parallel")),
        name="dilated_attention",
    )(*([z3] * (3 * n_g)))


def _in_proj_sizes():
    hd = HEAD_DIM
    kv = N_KV_NSA * hd
    return (N_HEADS_DIFF * hd, N_HEADS_DIFF * hd, N_HEADS_DIFF * hd,
            N_HEADS_FOX * hd, N_HEADS_FOX * hd, N_HEADS_FOX * hd, N_HEADS_FOX,
            N_HEADS_NSA * hd, kv, kv, kv, kv, kv, kv, 3 * N_HEADS_NSA,
            N_HEADS_DIL * hd, N_HEADS_DIL * hd, N_HEADS_DIL * hd)


_SEC_NAMES = ("qa", "ka", "va", "qb", "kb", "vb", "fb", "qc", "kcc", "vcc", "ksc", "vsc", "kwc", "vwc", "gc",
              "qd", "kd", "vd")
_SEC_ORDER = ("qa", "ka", "qc", "kcc", "ksc", "kwc", "qd", "kd", "va", "qb", "kb", "vb", "vcc", "vsc", "vwc", "vd")


def _pack_cols_kernel(blk_ref, cls_ref, a_ref, b_ref, o_ref, *, shifts):
    j = pl.program_id(0)
    for ci, s in enumerate(shifts):
        @pl.when(cls_ref[j] == ci)
        def _(s=s):
            a = a_ref[...]
            if s:
                lane = lax.broadcasted_iota(jnp.int32, a.shape, 1)
                a = jnp.where(lane < LANES - s, pltpu.roll(a, LANES - s, axis=1),
                              pltpu.roll(b_ref[...], LANES - s, axis=1))
            o_ref[...] = a.astype(o_ref.dtype)


def _pack_small_kernel(*refs, parts):
    o_ref = refs[-1]
    lane = lax.broadcasted_iota(jnp.int32, o_ref.shape, 1)
    out = jnp.zeros(o_ref.shape, F32)
    for ref, (_, src_lane, dst_lane, width) in zip(refs[:-1], parts):
        x = ref[...]
        if (dst_lane - src_lane) % LANES:
            x = pltpu.roll(x, (dst_lane - src_lane) % LANES, axis=1)
        out = jnp.where((lane >= dst_lane) & (lane < dst_lane + width), x, out)
    o_ref[...] = out.astype(o_ref.dtype)


def _pack_w_in(w_stack, li):
    _, rows, cols = w_stack.shape
    sizes = _in_proj_sizes()
    offs = np.concatenate([[0], np.cumsum(sizes)])
    start = {n: int(offs[i]) for i, n in enumerate(_SEC_NAMES)}
    size = {n: int(sizes[i]) for i, n in enumerate(_SEC_NAMES)}
    src = np.concatenate([start[n] + np.arange(0, size[n], LANES) for n in _SEC_ORDER])
    shifts = tuple(sorted(set(int(s) for s in src % LANES)))
    blk = jnp.asarray(src // LANES, jnp.int32)
    cls = jnp.asarray([shifts.index(int(s)) for s in src % LANES], jnp.int32)
    last = (cols - 1) // LANES
    main = pl.pallas_call(
        functools.partial(_pack_cols_kernel, shifts=shifts),
        grid_spec=pltpu.PrefetchScalarGridSpec(
            num_scalar_prefetch=2,
            grid=(len(src),),
            in_specs=[pl.BlockSpec((None, rows, LANES), lambda j, blk, cls: (li, 0, blk[j])),
                      pl.BlockSpec((None, rows, LANES), lambda j, blk, cls: (li, 0, jnp.minimum(blk[j] + 1, last)))],
            out_specs=pl.BlockSpec((rows, LANES), lambda j, blk, cls: (0, j))),
        out_shape=jax.ShapeDtypeStruct((rows, len(src) * LANES), BF16),
        compiler_params=_cparams(("parallel",)),
        name="w_in_pack",
    )(blk, cls, w_stack, w_stack)
    parts, dest = [], 0
    for n in ("fb", "gc"):
        assert start[n] // LANES == (start[n] + size[n] - 1) // LANES
        parts.append((start[n] // LANES, start[n] % LANES, dest, size[n]))
        dest += size[n]
    small = pl.pallas_call(
        functools.partial(_pack_small_kernel, parts=tuple(parts)),
        grid=(1,),
        in_specs=[pl.BlockSpec((None, rows, LANES), lambda i, b=b: (li, 0, b)) for b, _, _, _ in parts],
        out_specs=pl.BlockSpec((rows, LANES), lambda i: (0, 0)),
        out_shape=jax.ShapeDtypeStruct((rows, LANES), BF16),
        compiler_params=_cparams(("arbitrary",)),
        name="w_in_pack_narrow",
    )(*([w_stack] * len(parts)))
    scales = {"qa": (HEAD_DIM // 2) ** -0.5, "qb": HEAD_DIM ** -0.5, "qc": HEAD_DIM ** -0.5, "qd": HEAD_DIM ** -0.5}
    scale_row = jnp.concatenate([jnp.full((1, size[n]), scales.get(n, 1.0), F32) for n in _SEC_ORDER], axis=1)
    return main, small, scale_row


def _rope_tables(t):
    pos = jnp.arange(t, dtype=F32)[:, None]
    tabs = []
    for d in (HEAD_DIM // 2, HEAD_DIM):
        half = d // 2
        inv = ROPE_THETA ** (-jnp.arange(half, dtype=F32) * 2.0 / d)
        ang = pos * inv[None, :]
        cos, sin = jnp.cos(ang), jnp.sin(ang)
        reps = LANES // d
        tabs.append(jnp.tile(jnp.concatenate([cos, cos], axis=1), (1, reps)))
        tabs.append(jnp.tile(jnp.concatenate([-sin, sin], axis=1), (1, reps)))
    return tabs


def kernel(x, p, ffn1_pre_g, ffn1_w_gate, ffn1_w_up, ffn1_w_down, ffn1_post_g, mix_pre_g, w_in, fox_bf, diff_lam_q1, diff_lam_k1, diff_lam_q2, diff_lam_k2, diff_subln_g, nsa_pe_k, nsa_pe_v, nsa_wk1, nsa_wk2, nsa_wv1, nsa_wv2, w_out, mix_post_g, ffn2_pre_g, ffn2_w_gate, ffn2_w_up, ffn2_w_down, ffn2_post_g, ple_pre_g, ple_w_gate, ple_w_proj, ple_post_g):
    b, t, d = x.shape
    depth = w_in.shape[0]
    m = b * t
    tabs = _rope_tables(t)
    h = x.reshape(m, d)
    xn = _rmsnorm(h, ffn1_pre_g[0])

    def ffn(h, xn, li, wg, wu, wd, g_post, g_next):
        act = _gateup(xn, wg, wu, li)
        return _proj_residual(act, _cast_pad(wd, li, row_mult=FF_PAD), h, g_post, g_next, 0.5)

    for li in range(depth):
        h, xn = ffn(h, xn, li, ffn1_w_gate, ffn1_w_up, ffn1_w_down, ffn1_post_g[li], mix_pre_g[li])

        w_main, w_small, scale_row = _pack_w_in(w_in, li)
        z, zs = _inproj(xn, w_main, w_small, scale_row, tabs, t)
        z3 = z.reshape(b, t, z.shape[1])
        bf_row = jnp.pad(fox_bf[li].astype(F32), (0, LANES - N_HEADS_FOX)).reshape(1, LANES)
        c_col, c_row, gates = _small_prep(zs.reshape(b, t, LANES), bf_row)

        lam_init = 0.8 - 0.6 * math.exp(-0.3 * li)
        lam = (jnp.exp(jnp.sum(diff_lam_q1[li].astype(F32) * diff_lam_k1[li].astype(F32)))
               - jnp.exp(jnp.sum(diff_lam_q2[li].astype(F32) * diff_lam_k2[li].astype(F32))) + lam_init)
        o_a = _diff_attention(z3, lam, diff_subln_g[li].astype(F32), lam_init)
        o_b = _fox_attention(z3, c_col, c_row)
        kc, vc = _nsa_compress(z3, nsa_pe_k[li], nsa_pe_v[li], nsa_wk1[li].astype(BF16), nsa_wk2[li].astype(BF16),
                               nsa_wv1[li].astype(BF16), nsa_wv2[li].astype(BF16))
        o_c = _nsa_attention(z3, kc, vc, gates)
        o_d = _dilated_attention(z3)
        o_parts = [o.reshape(m, o.shape[-1]) for o in (o_a, o_b, o_c, o_d)]
        h, xn = _proj_residual(o_parts, _cast_pad(w_out, li), h, mix_post_g[li], ffn2_pre_g[li], 1.0)

        h, xn = ffn(h, xn, li, ffn2_w_gate, ffn2_w_up, ffn2_w_down, ffn2_post_g[li], ple_pre_g[li])

        g_next = ffn1_pre_g[li + 1] if li + 1 < depth else None
        h, xn = _proj_residual(xn, _cast_pad(ple_w_gate, li), h, ple_post_g[li], g_next, 1.0,
                               p=p[li].reshape(m, -1), wp=ple_w_proj[li].astype(BF16))
    return h.reshape(b, t, d)
```
